```python
import math
import jax
import jax.numpy as jnp
from jax import lax
import numpy as np

D_MODEL = 1024
BATCH = 4
SEQ = 8192
DEPTH = 4

CTX_LEN = 256
GRID_W = 64
CONV_K = 4
NORM_EPS = 1e-6

LRU_W = 1024
LRU_BLOCKS = 8
LRU_BW = LRU_W // LRU_BLOCKS
LRU_C = 8.0

GDN_HEADS = 8
GDN_DK = 128
GDN_DV = 128
GDN_CHUNK = 64
GDN_QKV = 2 * GDN_HEADS * GDN_DK + GDN_HEADS * GDN_DV
GDN_SCALE = GDN_DK ** -0.5

SSD_INNER = 1024
SSD_HEADDIM = 64
SSD_HEADS = SSD_INNER // SSD_HEADDIM
SSD_GROUPS = 2
SSD_STATE = 128
SSD_CHUNK = 128
SSD_XBC = SSD_INNER + 2 * SSD_GROUPS * SSD_STATE

N_BRANCH = 3
BRANCH_W = 1024

N_GROUPS = 4
EXPERTS_PER_GROUP = 8
N_EXPERTS = N_GROUPS * EXPERTS_PER_GROUP
TOP_K = 2
EXPERT_FF = 512
MOE_BLOCK = 128

IN_SIZES = (LRU_W, LRU_W, GDN_QKV, GDN_HEADS * GDN_DV, 2 * GDN_HEADS, 2 * GDN_HEADS,
            SSD_INNER, SSD_XBC, 2 * SSD_HEADS, N_BRANCH * D_MODEL)
IN_WIDTH = sum(IN_SIZES)

kernel_name = "hybrid_lru_gdn_ssd_hmoe_dit"


def _rmsnorm(x, w):
    xf = x.astype(jnp.float32)
    y = xf * lax.rsqrt(jnp.mean(xf * xf, axis=-1, keepdims=True) + NORM_EPS)
    return (y * w.astype(jnp.float32)).astype(x.dtype)


def _l2norm(x):
    xf = x.astype(jnp.float32)
    return (xf * lax.rsqrt(jnp.sum(xf * xf, axis=-1, keepdims=True) + NORM_EPS)).astype(x.dtype)


def _dwconv(x, w, b=None):
    k, ch = w.shape
    y = lax.conv_general_dilated(x, w[:, None, :].astype(x.dtype), window_strides=(1,),
                                 padding=[(k // 2, k - 1 - k // 2)],
                                 dimension_numbers=("NWC", "WIO", "NWC"), feature_group_count=ch)
    return y if b is None else y + b


def _to_colmajor(t, rows):
    b, l = t.shape[:2]
    rest = t.shape[2:]
    return t.reshape(b, rows, GRID_W, *rest).swapaxes(1, 2).reshape(b, l, *rest)


def _from_colmajor(t, rows):
    b, l = t.shape[:2]
    rest = t.shape[2:]
    return t.reshape(b, GRID_W, rows, *rest).swapaxes(1, 2).reshape(b, l, *rest)


def _run_direction(scan_fn, ctx_args, lat_args, state0, reverse):
    if reverse:
        ctx_args = tuple(jnp.flip(a, axis=1) for a in ctx_args)
        lat_args = tuple(jnp.flip(a, axis=1) for a in lat_args)
    y_c, s_c = scan_fn(*ctx_args, state0)
    y_l, _ = scan_fn(*lat_args, s_c)
    if reverse:
        y_c, y_l = jnp.flip(y_c, axis=1), jnp.flip(y_l, axis=1)
    return y_c, y_l


def _lru_scan(log_a, v, h0):
    def combine(e, l):
        return e[0] * l[0], l[0] * e[1] + l[1]
    a_cum, h = lax.associative_scan(combine, (jnp.exp(log_a), v), axis=1)
    h = h + a_cum * h0[:, None, :]
    return h, h[:, -1]


def _lru_gates(u, wa, ba, wi, bi, lam):
    b, l, _ = u.shape
    ub = u.reshape(b, l, LRU_BLOCKS, LRU_BW)
    r = jax.nn.sigmoid((jnp.einsum("blnj,njk->blnk", ub, wa).reshape(b, l, LRU_W) + ba).astype(jnp.float32))
    i = jax.nn.sigmoid((jnp.einsum("blnj,njk->blnk", ub, wi).reshape(b, l, LRU_W) + bi).astype(jnp.float32))
    log_a = -LRU_C * r * jax.nn.softplus(-lam.astype(jnp.float32))
    v = jnp.sqrt(-jnp.expm1(2.0 * log_a)) * i * u.astype(jnp.float32)
    return log_a, v


def _lru_branch(x_c, y_c, x_l, y_l, conv_w, conv_b, wa, ba, wi, bi, lam):
    u_c = _dwconv(x_c, conv_w, conv_b)
    u_l = _dwconv(x_l, conv_w, conv_b)
    h0 = jnp.zeros((x_c.shape[0], LRU_W), jnp.float32)
    h_c = 0.0
    h_l = 0.0
    for d in range(2):
        c_args = _lru_gates(u_c, wa[d], ba[d], wi[d], bi[d], lam[d])
        l_args = _lru_gates(u_l, wa[d], ba[d], wi[d], bi[d], lam[d])
        o_c, o_l = _run_direction(_lru_scan, c_args, l_args, h0, d == 1)
        h_c = h_c + o_c
        h_l = h_l + o_l
    return (h_c.astype(x_c.dtype) * jax.nn.gelu(y_c),
            h_l.astype(x_l.dtype) * jax.nn.gelu(y_l))


def _gdn_chunk_scan(q, k, v, g, beta, s0):
    b, l, h, _ = q.shape
    dv = v.shape[-1]
    n = l // GDN_CHUNK

    def chunks(t):
        t = t.astype(jnp.float32).reshape(b, n, GDN_CHUNK, h, *t.shape[3:])
        return jnp.moveaxis(jnp.swapaxes(t, 2, 3), 1, 0)

    incl = jnp.tril(jnp.ones((GDN_CHUNK, GDN_CHUNK), bool))
    strict = jnp.tril(jnp.ones((GDN_CHUNK, GDN_CHUNK), bool), -1)
    eye = jnp.eye(GDN_CHUNK, dtype=jnp.float32)

    def step(s, inp):
        qc, kc, vc, gc, bc = inp
        gcs = jnp.cumsum(gc, axis=-1)
        dec = jnp.exp(jnp.where(incl, gcs[..., :, None] - gcs[..., None, :], -jnp.inf))
        kb = kc * bc[..., None]
        a_low = jnp.where(strict, jnp.einsum("bhid,bhjd->bhij", kb, kc) * dec, 0.0)
        rhs = jnp.concatenate([vc * bc[..., None], kb * jnp.exp(gcs)[..., None]], axis=-1)
        sol = lax.linalg.triangular_solve(eye + a_low, rhs, left_side=True, lower=True, unit_diagonal=True)
        u, w = sol[..., :dv], sol[..., dv:]
        v_new = u - jnp.einsum("bhcd,bhde->bhce", w, s)
        attn = jnp.einsum("bhid,bhjd->bhij", qc, kc) * dec
        o = (jnp.einsum("bhcd,bhde->bhce", qc * jnp.exp(gcs)[..., None], s)
             + jnp.einsum("bhij,bhje->bhie", attn, v_new))
        k_dec = kc * jnp.exp(gcs[..., -1:] - gcs)[..., None]
        s = s * jnp.exp(gcs[..., -1])[..., None, None] + jnp.einsum("bhcd,bhce->bhde", k_dec, v_new)
        return s, o

    s_fin, o = lax.scan(step, s0, tuple(chunks(t) for t in (q, k, v, g, beta)))
    o = jnp.moveaxis(o, 0, 1).swapaxes(2, 3).reshape(b, l, h, dv)
    return o, s_fin


def _gdn_prep(qkv, b_raw, a_raw, conv_w, a_log, dt_bias):
    b, l, _ = qkv.shape
    u = jax.nn.silu(_dwconv(qkv, conv_w))
    q, k, v = jnp.split(u, [GDN_HEADS * GDN_DK, 2 * GDN_HEADS * GDN_DK], axis=-1)
    q = _l2norm(q.reshape(b, l, GDN_HEADS, GDN_DK)) * GDN_SCALE
    k = _l2norm(k.reshape(b, l, GDN_HEADS, GDN_DK))
    v = v.reshape(b, l, GDN_HEADS, GDN_DV)
    beta = jax.nn.sigmoid(b_raw.astype(jnp.float32)).reshape(b, l, 2, GDN_HEADS)
    g = -jnp.exp(a_log.astype(jnp.float32)) * jax.nn.softplus(
        a_raw.astype(jnp.float32).reshape(b, l, 2, GDN_HEADS) + dt_bias.astype(jnp.float32))
    return q, k, v, g, beta


def _gdn_branch(qkv_c, z_c, b_c, a_c, qkv_l, z_l, b_l, a_l, conv_w, a_log, dt_bias, norm_w):
    qc, kc, vc, gc, bc = _gdn_prep(qkv_c, b_c, a_c, conv_w, a_log, dt_bias)
    ql, kl, vl, gl, bl = _gdn_prep(qkv_l, b_l, a_l, conv_w, a_log, dt_bias)
    s0 = jnp.zeros((qkv_c.shape[0], GDN_HEADS, GDN_DK, GDN_DV), jnp.float32)
    o_c = 0.0
    o_l = 0.0
    for d in range(2):
        c_args = (qc, kc, vc, gc[:, :, d], bc[:, :, d])
        l_args = (ql, kl, vl, gl[:, :, d], bl[:, :, d])
        y_c, y_l = _run_direction(_gdn_chunk_scan, c_args, l_args, s0, d == 1)
        o_c = o_c + y_c
        o_l = o_l + y_l

    def gated_norm(o, z):
        b, l, _ = z.shape
        y = _rmsnorm(o.astype(z.dtype), norm_w) * jax.nn.silu(z.reshape(b, l, GDN_HEADS, GDN_DV))
        return y.reshape(b, l, GDN_HEADS * GDN_DV)

    return gated_norm(o_c, z_c), gated_norm(o_l, z_l)


def _ssd_chunk_scan(xdt, adt, bm, cm, s0):
    b, l, h, p = xdt.shape
    g, n_st = bm.shape[2], bm.shape[3]
    e = h // g
    nc = l // SSD_CHUNK
    x = xdt.astype(jnp.float32).reshape(b, nc, SSD_CHUNK, g, e, p)
    a = adt.astype(jnp.float32).reshape(b, nc, SSD_CHUNK, g, e).transpose(0, 1, 3, 4, 2)
    bc = bm.astype(jnp.float32).reshape(b, nc, SSD_CHUNK, g, n_st)
    cc = cm.astype(jnp.float32).reshape(b, nc, SSD_CHUNK, g, n_st)
    incl = jnp.tril(jnp.ones((SSD_CHUNK, SSD_CHUNK), bool))

    def step(s, inp):
        x_c, a_c, b_c, c_c = inp
        acs = jnp.cumsum(a_c, axis=-1)
        lm = jnp.exp(jnp.where(incl, acs[..., :, None] - acs[..., None, :], -jnp.inf))
        cb = jnp.einsum("blgn,bsgn->bgls", c_c, b_c)
        y = jnp.einsum("bgls,bgels,bsgep->blgep", cb, lm, x_c)
        y = y + jnp.einsum("blgn,bgepn,bgel->blgep", c_c, s, jnp.exp(acs))
        s = (s * jnp.exp(acs[..., -1])[..., None, None]
             + jnp.einsum("bsgn,bges,bsgep->bgepn", b_c, jnp.exp(acs[..., -1:] - acs), x_c))
        return s, y

    xs = tuple(jnp.moveaxis(t, 1, 0) for t in (x, a, bc, cc))
    s_fin, y = lax.scan(step, s0, xs)
    return jnp.moveaxis(y, 0, 1).reshape(b, l, h, p), s_fin


def _ssd_prep(xbc, dt_raw, conv_w, conv_b, dt_bias):
    b, l, _ = xbc.shape
    u = jax.nn.silu(_dwconv(xbc, conv_w, conv_b))
    xs, bm, cm = jnp.split(u, [SSD_INNER, SSD_INNER + SSD_GROUPS * SSD_STATE], axis=-1)
    xs = xs.reshape(b, l, SSD_HEADS, SSD_HEADDIM)
    bm = bm.reshape(b, l, SSD_GROUPS, SSD_STATE)
    cm = cm.reshape(b, l, SSD_GROUPS, SSD_STATE)
    dt = jax.nn.softplus(dt_raw.astype(jnp.float32).reshape(b, l, 2, SSD_HEADS) + dt_bias.astype(jnp.float32))
    return xs, bm, cm, dt


def _ssd_branch(z_c, xbc_c, sdt_c, z_l, xbc_l, sdt_l, rows, conv_w, conv_b, a_log, dt_bias, d_skip, norm_w):
    xs_c, bm_c, cm_c, dt_c = _ssd_prep(xbc_c, sdt_c, conv_w, conv_b, dt_bias)
    xs_l, bm_l, cm_l, dt_l = _ssd_prep(_to_colmajor(xbc_l, rows), _to_colmajor(sdt_l, rows),
                                       conv_w, conv_b, dt_bias)
    a = -jnp.exp(a_log.astype(jnp.float32))
    e = SSD_HEADS // SSD_GROUPS
    s0 = jnp.zeros((xbc_c.shape[0], SSD_GROUPS, e, SSD_HEADDIM, SSD_STATE), jnp.float32)
    y_c = 0.0
    y_l = 0.0
    for d in range(2):
        c_args = (xs_c * dt_c[:, :, d, :, None], a[d] * dt_c[:, :, d], bm_c, cm_c)
        l_args = (xs_l * dt_l[:, :, d, :, None], a[d] * dt_l[:, :, d], bm_l, cm_l)
        o_c, o_l = _run_direction(_ssd_chunk_scan, c_args, l_args, s0, d == 1)
        y_c = y_c + o_c + d_skip[d][:, None] * xs_c
        y_l = y_l + o_l + d_skip[d][:, None] * xs_l
    y_l = _from_colmajor(y_l, rows)

    def gated_norm(y, z):
        b, l, _ = z.shape
        yz = y.reshape(b, l, SSD_INNER).astype(z.dtype) * jax.nn.silu(z)
        yz = _rmsnorm(yz.reshape(b, l, SSD_GROUPS, SSD_INNER // SSD_GROUPS),
                      norm_w.reshape(SSD_GROUPS, SSD_INNER // SSD_GROUPS))
        return yz.reshape(b, l, SSD_INNER)

    return gated_norm(y_c, z_c), gated_norm(y_l, z_l)


def _merge(branches, gate_pre, w_branch, w_out):
    b, l, gw = gate_pre.shape
    gates = jax.nn.sigmoid(gate_pre.astype(jnp.float32)).astype(gate_pre.dtype).reshape(b, l, N_BRANCH, gw // N_BRANCH)
    acc = 0.0
    for k in range(N_BRANCH):
        acc = acc + gates[:, :, k] * (branches[k] @ w_branch[k])
    return acc @ w_out


def _token_mixer(h_c, h_l, rows, need_ctx_out, w_in, lru_conv_w, lru_conv_b, lru_wa, lru_ba, lru_wi, lru_bi,
                 lru_lambda, gdn_conv_w, gdn_a_log, gdn_dt_bias, gdn_norm_w, ssd_conv_w, ssd_conv_b, ssd_a_log,
                 ssd_dt_bias, ssd_d, ssd_norm_w, w_branch, w_out):
    offs = np.cumsum(IN_SIZES)[:-1].tolist()
    w_parts = jnp.split(w_in, offs, axis=-1)
    (lx_c, ly_c, qkv_c, gz_c, gb_c, ga_c, sz_c, xbc_c, sdt_c, gate_c) = [h_c @ w for w in w_parts]
    (lx_l, ly_l, qkv_l, gz_l, gb_l, ga_l, sz_l, xbc_l, sdt_l, gate_l) = [h_l @ w for w in w_parts]
    lru_c, lru_l = _lru_branch(lx_c, ly_c, lx_l, ly_l, lru_conv_w, lru_conv_b, lru_wa, lru_ba, lru_wi, lru_bi,
                               lru_lambda)
    gdn_c, gdn_l = _gdn_branch(qkv_c, gz_c, gb_c, ga_c, qkv_l, gz_l, gb_l, ga_l, gdn_conv_w, gdn_a_log,
                               gdn_dt_bias, gdn_norm_w)
    ssd_c, ssd_l = _ssd_branch(sz_c, xbc_c, sdt_c, sz_l, xbc_l, sdt_l, rows, ssd_conv_w, ssd_conv_b, ssd_a_log,
                               ssd_dt_bias, ssd_d, ssd_norm_w)
    out_l = _merge((lru_l, gdn_l, ssd_l), gate_l, w_branch, w_out)
    out_c = _merge((lru_c, gdn_c, ssd_c), gate_c, w_branch, w_out) if need_ctx_out else None
    return out_c, out_l


def _hier_moe(h, wg, bg, we, be, w1, w3, w2):
    t, d = h.shape
    g_logits = (h @ wg).astype(jnp.float32) + bg.astype(jnp.float32)
    g_prob = jax.nn.softmax(g_logits, axis=-1)
    g_idx = jnp.argmax(g_logits, axis=-1)
    p_group = jnp.take_along_axis(g_prob, g_idx[:, None], axis=1)[:, 0]
    e_logits = ((h @ we).astype(jnp.float32) + be.astype(jnp.float32)).reshape(t, N_GROUPS, EXPERTS_PER_GROUP)
    e_in_group = jnp.take_along_axis(e_logits, g_idx[:, None, None], axis=1)[:, 0]
    top_v, top_i = lax.top_k(e_in_group, TOP_K)
    weights = jax.nn.softmax(top_v, axis=-1) * p_group[:, None]
    expert_id = (g_idx[:, None] * EXPERTS_PER_GROUP + top_i).reshape(-1).astype(jnp.int32)
    token_id = jnp.repeat(jnp.arange(t, dtype=jnp.int32), TOP_K)
    weight = weights.reshape(-1)
    n_assign = t * TOP_K
    order = jnp.argsort(expert_id)
    e_sorted = expert_id[order]
    counts = jnp.bincount(expert_id, length=N_EXPERTS)
    padded = (counts + MOE_BLOCK - 1) // MOE_BLOCK * MOE_BLOCK
    starts = jnp.cumsum(counts) - counts
    pad_ends = jnp.cumsum(padded)
    pad_starts = pad_ends - padded
    dest = pad_starts[e_sorted] + jnp.arange(n_assign, dtype=jnp.int32) - starts[e_sorted]
    n_blocks = -(-(n_assign + N_EXPERTS * (MOE_BLOCK - 1)) // MOE_BLOCK)
    n_rows = n_blocks * MOE_BLOCK
    tok_buf = jnp.full((n_rows,), t, jnp.int32).at[dest].set(token_id[order])
    w_buf = jnp.zeros((n_rows,), jnp.float32).at[dest].set(weight[order])
    block_start = jnp.arange(n_blocks, dtype=jnp.int32) * MOE_BLOCK
    block_expert = jnp.minimum(jnp.sum(block_start[:, None] >= pad_ends[None, :], axis=1), N_EXPERTS - 1)
    h_pad = jnp.concatenate([h, jnp.zeros((1, d), h.dtype)], axis=0)
    xb = h_pad[tok_buf].reshape(n_blocks, MOE_BLOCK, d)

    def expert_block(args):
        xblk, e = args
        return (jax.nn.silu(xblk @ w1[e]) * (xblk @ w3[e])) @ w2[e]

    yb = lax.map(expert_block, (xb, block_expert)).reshape(n_rows, d)
    y = jnp.zeros((t + 1, d), h.dtype).at[tok_buf].add(yb * w_buf[:, None].astype(h.dtype))
    return y[:t]


def setup_inputs(seed: int = 0) -> dict:
    key = jax.random.key(seed)
    ks = jax.random.split(key, 40)
    idx = iter(range(40))

    def nrm(shape, scale):
        return jax.random.normal(ks[next(idx)], shape, jnp.float32) * scale

    def unif(shape, lo, hi):
        return jax.random.uniform(ks[next(idx)], shape, jnp.float32, lo, hi)

    def gain(shape):
        return 1.0 + nrm(shape, 0.02)

    def dt_bias(shape):
        dt = jnp.exp(unif(shape, math.log(1e-3), math.log(1e-1)))
        return dt + jnp.log(-jnp.expm1(-dt))

    def lru_lambda(shape):
        s = unif(shape, 0.9, 0.999) ** (1.0 / LRU_C)
        return jnp.log(s) - jnp.log1p(-s)

    d = D_MODEL
    return {
        "x": nrm((BATCH, SEQ, d), 1.0),
        "c": nrm((BATCH, d), 1.0),
        "ctx": nrm((BATCH, CTX_LEN, d), 1.0),
        "c_ctx": nrm((d,), 1.0),
        "w_mod": nrm((DEPTH, d, 6 * d), 0.5 * d ** -0.5),
        "b_mod": nrm((DEPTH, 6 * d), 0.01),
        "norm1_w": gain((DEPTH, d)),
        "norm2_w": gain((DEPTH, d)),
        "w_in": nrm((DEPTH, d, IN_WIDTH), d ** -0.5),
        "lru_conv_w": nrm((DEPTH, CONV_K, LRU_W), CONV_K ** -0.5),
        "lru_conv_b": nrm((DEPTH, LRU_W), 0.01),
        "lru_wa": nrm((DEPTH, 2, LRU_BLOCKS, LRU_BW, LRU_BW), LRU_BW ** -0.5),
        "lru_ba": nrm((DEPTH, 2, LRU_W), 0.01),
        "lru_wi": nrm((DEPTH, 2, LRU_BLOCKS, LRU_BW, LRU_BW), LRU_BW ** -0.5),
        "lru_bi": nrm((DEPTH, 2, LRU_W), 0.01),
        "lru_lambda": lru_lambda((DEPTH, 2, LRU_W)),
        "gdn_conv_w": nrm((DEPTH, CONV_K, GDN_QKV), CONV_K ** -0.5),
        "gdn_a_log": jnp.log(unif((DEPTH, 2, GDN_HEADS), 1.0, 16.0)),
        "gdn_dt_bias": dt_bias((DEPTH, 2, GDN_HEADS)),
        "gdn_norm_w": gain((DEPTH, GDN_DV)),
        "ssd_conv_w": nrm((DEPTH, CONV_K, SSD_XBC), CONV_K ** -0.5),
        "ssd_conv_b": nrm((DEPTH, SSD_XBC), 0.01),
        "ssd_a_log": jnp.log(unif((DEPTH, 2, SSD_HEADS), 1.0, 16.0)),
        "ssd_dt_bias": dt_bias((DEPTH, 2, SSD_HEADS)),
        "ssd_d": 1.0 + nrm((DEPTH, 2, SSD_HEADS), 0.1),
        "ssd_norm_w": gain((DEPTH, SSD_INNER)),
        "w_branch": nrm((DEPTH, N_BRANCH, BRANCH_W, d), BRANCH_W ** -0.5),
        "w_out": nrm((DEPTH, d, d), d ** -0.5),
        "router_group_w": nrm((DEPTH, d, N_GROUPS), d ** -0.5),
        "router_group_b": nrm((DEPTH, N_GROUPS), 0.01),
        "router_expert_w": nrm((DEPTH, d, N_EXPERTS), d ** -0.5),
        "router_expert_b": nrm((DEPTH, N_EXPERTS), 0.01),
        "expert_w1": nrm((DEPTH, N_EXPERTS, d, EXPERT_FF), d ** -0.5),
        "expert_w3": nrm((DEPTH, N_EXPERTS, d, EXPERT_FF), d ** -0.5),
        "expert_w2": nrm((DEPTH, N_EXPERTS, EXPERT_FF, d), EXPERT_FF ** -0.5),
        "final_norm_w": gain((d,)),
    }


def reference(x, c, ctx, c_ctx, w_mod, b_mod, norm1_w, norm2_w, w_in, lru_conv_w, lru_conv_b, lru_wa, lru_ba,
              lru_wi, lru_bi, lru_lambda, gdn_conv_w, gdn_a_log, gdn_dt_bias, gdn_norm_w, ssd_conv_w, ssd_conv_b,
              ssd_a_log, ssd_dt_bias, ssd_d, ssd_norm_w, w_branch, w_out, router_group_w, router_group_b,
              router_expert_w, router_expert_b, expert_w1, expert_w3, expert_w2, final_norm_w):
    bsz, seq, d = x.shape
    rows = seq // GRID_W
    n_ctx_tok = bsz * ctx.shape[1]
    x_l, x_c = x, ctx
    act_l = jax.nn.silu(c)
    act_c = jax.nn.silu(c_ctx)
    for i in range(DEPTH):
        last = i == DEPTH - 1
        mod_l = jnp.split((act_l @ w_mod[i] + b_mod[i])[:, None, :], 6, axis=-1)
        mod_c = jnp.split((act_c @ w_mod[i] + b_mod[i])[None, None, :], 6, axis=-1)
        h_l = _rmsnorm(x_l, norm1_w[i]) * (1 + mod_l[1]) + mod_l[0]
        h_c = _rmsnorm(x_c, norm1_w[i]) * (1 + mod_c[1]) + mod_c[0]
        mix_c, mix_l = _token_mixer(h_c, h_l, rows, not last, w_in[i], lru_conv_w[i], lru_conv_b[i], lru_wa[i],
                                    lru_ba[i], lru_wi[i], lru_bi[i], lru_lambda[i], gdn_conv_w[i], gdn_a_log[i],
                                    gdn_dt_bias[i], gdn_norm_w[i], ssd_conv_w[i], ssd_conv_b[i], ssd_a_log[i],
                                    ssd_dt_bias[i], ssd_d[i], ssd_norm_w[i], w_branch[i], w_out[i])
        x_l = x_l + mod_l[2] * mix_l
        h_l = _rmsnorm(x_l, norm2_w[i]) * (1 + mod_l[4]) + mod_l[3]
        moe_w = (router_group_w[i], router_group_b[i], router_expert_w[i], router_expert_b[i],
                 expert_w1[i], expert_w3[i], expert_w2[i])
        if last:
            x_l = x_l + mod_l[5] * _hier_moe(h_l.reshape(-1, d), *moe_w).reshape(x_l.shape)
        else:
            x_c = x_c + mod_c[2] * mix_c
            h_c = _rmsnorm(x_c, norm2_w[i]) * (1 + mod_c[4]) + mod_c[3]
            y = _hier_moe(jnp.concatenate([h_c.reshape(-1, d), h_l.reshape(-1, d)], axis=0), *moe_w)
            x_c = x_c + mod_c[5] * y[:n_ctx_tok].reshape(x_c.shape)
            x_l = x_l + mod_l[5] * y[n_ctx_tok:].reshape(x_l.shape)
    return _rmsnorm(x_l, final_norm_w)
```

```python
import functools

import jax
import jax.numpy as jnp
from jax import lax
from jax.experimental import pallas as pl
from jax.experimental.pallas import tpu as pltpu

GRID_W = 64
CONV_K = 4
NORM_EPS = 1e-6
LRU_BLOCKS = 8
LRU_C = 8.0
GDN_HEADS = 8
GDN_DK = 128
SSD_HEADS = 16
SSD_HEADDIM = 64
SSD_GROUPS = 2
SSD_STATE = 128
N_GROUPS = 4
EXPERTS_PER_GROUP = 8
N_EXPERTS = N_GROUPS * EXPERTS_PER_GROUP
TOP_K = 2

LANE = 128
SUBLANE = 8
ROW_TILE = 256
LRU_CHUNK = 256
SCAN_CHUNK = 128
MOE_ROWS = 256
VMEM_LIMIT = 56 * 1024 * 1024

F32 = jnp.float32
BF16 = jnp.bfloat16
NEG_BIG = -1e30


def _cparams(*sem):
    return pltpu.CompilerParams(dimension_semantics=sem, vmem_limit_bytes=VMEM_LIMIT)


def _dot(a, b):
    return jnp.dot(a, b, preferred_element_type=F32)


def _dot_nt(a, b):
    return lax.dot_general(a, b, (((1,), (1,)), ((), ())), preferred_element_type=F32)


def _dot_tn(a, b):
    return lax.dot_general(a, b, (((0,), (0,)), ((), ())), preferred_element_type=F32)


def _split3(x):
    hi = x.astype(BF16)
    r1 = x - hi.astype(F32)
    mid = r1.astype(BF16)
    lo = (r1 - mid.astype(F32)).astype(BF16)
    return hi, mid, lo


def _exact_left(m_bf, x):
    hi, mid, lo = _split3(x)
    return _dot(jnp.concatenate([m_bf, m_bf, m_bf], axis=1), jnp.concatenate([hi, mid, lo], axis=0))


def _exact_right(x, m_bf):
    hi, mid, lo = _split3(x)
    return _dot(jnp.concatenate([hi, mid, lo], axis=1), jnp.concatenate([m_bf, m_bf, m_bf], axis=0))


def _exact_transpose(x, eye_bf):
    hi, mid, lo = _split3(x)
    return _dot_nt(jnp.concatenate([eye_bf, eye_bf, eye_bf], axis=1), jnp.concatenate([hi, mid, lo], axis=1))


def _softplus(x):
    return jnp.maximum(x, 0.0) + jnp.log(1.0 + jnp.exp(-jnp.abs(x)))


def _silu(x):
    return x * jax.nn.sigmoid(x)


def _iota(shape, dim):
    return lax.broadcasted_iota(jnp.int32, shape, dim)


def _fwd_chunk(i, n_lat, n_ctx):
    return jnp.where(i < n_ctx, n_lat + i, i - n_ctx)


def _chunk_of_step(i, n_lat, n_ctx, reverse):
    return (n_lat + n_ctx - 1 - i) if reverse else _fwd_chunk(i, n_lat, n_ctx)


def _segment_edges(c, n_lat, n_ctx):
    first = jnp.logical_or(c == 0, c == n_lat)
    last = jnp.logical_or(c == n_lat - 1, c == n_lat + n_ctx - 1)
    return first, last


def _seg_conv(xe_ref, x, prev8, next8, first, last, cw, rows):
    xe_ref[0:SUBLANE, :] = jnp.where(first, 0.0, prev8)
    xe_ref[SUBLANE:SUBLANE + rows, :] = x
    xe_ref[SUBLANE + rows:2 * SUBLANE + rows, :] = jnp.where(last, 0.0, next8)
    base = SUBLANE - CONV_K // 2
    u = cw[0:1, :] * xe_ref[base:base + rows, :]
    for j in range(1, CONV_K):
        u = u + cw[j:j + 1, :] * xe_ref[base + j:base + j + rows, :]
    return u


def _halo_specs(width, rows, n_rows_total, chunk_fn, col_block=0):
    per = rows // SUBLANE
    last_tile = n_rows_total // SUBLANE - 1
    prev = pl.BlockSpec((None, SUBLANE, width),
                        lambda b, i: (b, jnp.maximum(chunk_fn(i) * per - 1, 0), col_block))
    nxt = pl.BlockSpec((None, SUBLANE, width),
                       lambda b, i: (b, jnp.minimum((chunk_fn(i) + 1) * per, last_tile), col_block))
    return prev, nxt


def _mod_kernel(a_ref, w_ref, b_ref, o_ref):
    o_ref[...] = jnp.dot(a_ref[...], w_ref[...], preferred_element_type=F32,
                         precision=lax.Precision.HIGHEST) + b_ref[...]


def _modulation(act, w_mod, b_mod):
    depth, d, n = w_mod.shape
    tn = 1536
    return pl.pallas_call(
        _mod_kernel,
        grid=(depth, n // tn),
        in_specs=[pl.BlockSpec((SUBLANE, d), lambda l, j: (0, 0)),
                  pl.BlockSpec((None, d, tn), lambda l, j: (l, 0, j)),
                  pl.BlockSpec((None, 1, tn), lambda l, j: (l, 0, j))],
        out_specs=pl.BlockSpec((None, SUBLANE, tn), lambda l, j: (l, 0, j)),
        out_shape=jax.ShapeDtypeStruct((depth, SUBLANE, n), F32),
        compiler_params=_cparams("arbitrary", "arbitrary"),
        name="modulation",
    )(act, w_mod, b_mod.reshape(depth, 1, n))


def _mod_norm(x, nw, shift, scale):
    y = x * lax.rsqrt(jnp.mean(x * x, axis=-1, keepdims=True) + NORM_EPS)
    return (y * nw) * (1.0 + scale) + shift


def _norm_matmul_kernel(x_ref, mod_ref, nw_ref, w_ref, o_ref, *, shift_row, scale_row, col_step):
    h = _mod_norm(x_ref[...], nw_ref[...], mod_ref[shift_row:shift_row + 1, :],
                  mod_ref[scale_row:scale_row + 1, :]).astype(BF16)
    n = w_ref.shape[1]
    for c0 in range(0, n, col_step):
        c1 = min(c0 + col_step, n)
        o_ref[:, c0:c1] = _dot(h, w_ref[:, c0:c1]).astype(o_ref.dtype)


def _norm_matmul(x, modtab, norm_w, w, shift_row, scale_row, n_lat_tiles, name):
    bsz, l, d = x.shape
    n = w.shape[1]
    kern = functools.partial(_norm_matmul_kernel, shift_row=shift_row, scale_row=scale_row, col_step=512)
    return pl.pallas_call(
        kern,
        grid=(bsz, l // ROW_TILE),
        in_specs=[pl.BlockSpec((None, ROW_TILE, d), lambda b, j: (b, j, 0)),
                  pl.BlockSpec((None, None, SUBLANE, d), lambda b, j: (b, jnp.where(j >= n_lat_tiles, 1, 0), 0, 0)),
                  pl.BlockSpec((1, d), lambda b, j: (0, 0)),
                  pl.BlockSpec((d, n), lambda b, j: (0, 0))],
        out_specs=pl.BlockSpec((None, ROW_TILE, n), lambda b, j: (b, j, 0)),
        out_shape=jax.ShapeDtypeStruct((bsz, l, n), F32),
        compiler_params=_cparams("arbitrary", "arbitrary"),
        name=name,
    )(x, modtab, norm_w, w)


def _lru_kernel(*refs, reverse, final, n_lat, n_ctx):
    if final:
        (x_ref, xp_ref, xn_ref, cw_ref, cb_ref, wa_ref, ba_ref, wi_ref, bi_ref, lam_ref, hf_ref, y_ref,
         o_ref, xe_ref, a_ref, h_ref, carry_ref) = refs
    else:
        (x_ref, xp_ref, xn_ref, cw_ref, cb_ref, wa_ref, ba_ref, wi_ref, bi_ref, lam_ref,
         o_ref, xe_ref, a_ref, h_ref, carry_ref) = refs
    rows = LRU_CHUNK
    i = pl.program_id(1)
    c = _chunk_of_step(i, n_lat, n_ctx, reverse)
    first, last = _segment_edges(c, n_lat, n_ctx)

    @pl.when(i == 0)
    def _():
        carry_ref[...] = jnp.zeros_like(carry_ref)

    u = _seg_conv(xe_ref, x_ref[...], xp_ref[...], xn_ref[...], first, last, cw_ref[...], rows) + cb_ref[...]
    ub = u.astype(BF16)
    bw = u.shape[1] // LRU_BLOCKS
    pre_r = jnp.concatenate([_dot(ub[:, n * bw:(n + 1) * bw], wa_ref[n]) for n in range(LRU_BLOCKS)], axis=1)
    pre_i = jnp.concatenate([_dot(ub[:, n * bw:(n + 1) * bw], wi_ref[n]) for n in range(LRU_BLOCKS)], axis=1)
    r = jax.nn.sigmoid(pre_r + ba_ref[...])
    gi = jax.nn.sigmoid(pre_i + bi_ref[...])
    a = jnp.exp(-LRU_C * r * _softplus(-lam_ref[...]))
    h = jnp.sqrt(1.0 - a * a) * gi * u

    rowmod = jnp.bitwise_and(_iota(a.shape, 0), SUBLANE - 1)
    for s in (1, 2, 4):
        if reverse:
            a_s = pltpu.roll(a, rows - s, axis=0)
            h_s = pltpu.roll(h, rows - s, axis=0)
            keep = rowmod < SUBLANE - s
        else:
            a_s = pltpu.roll(a, s, axis=0)
            h_s = pltpu.roll(h, s, axis=0)
            keep = rowmod >= s
        h = h + a * jnp.where(keep, h_s, 0.0)
        a = a * jnp.where(keep, a_s, 1.0)
    a_ref[...] = a
    h_ref[...] = h

    n_tiles = rows // SUBLANE
    carry = carry_ref[...]
    order = range(n_tiles - 1, -1, -1) if reverse else range(n_tiles)
    for k in order:
        sl = slice(k * SUBLANE, (k + 1) * SUBLANE)
        hk = h_ref[sl, :] + a_ref[sl, :] * carry
        carry = hk[0:1, :] if reverse else hk[SUBLANE - 1:SUBLANE, :]
        if final:
            o_ref[sl, :] = (hk + hf_ref[sl, :]) * jax.nn.gelu(y_ref[sl, :], approximate=True)
        else:
            o_ref[sl, :] = hk
    carry_ref[...] = carry


def _lru_scan(ylru, conv_w, conv_b, wa, ba, wi, bi, lam, hf, n_lat, n_ctx, reverse):
    bsz, l, w2 = ylru.shape
    w = w2 // 2
    final = hf is not None
    chunk_fn = functools.partial(_chunk_of_step, n_lat=n_lat, n_ctx=n_ctx, reverse=reverse)
    main = pl.BlockSpec((None, LRU_CHUNK, w), lambda b, i: (b, chunk_fn(i), 0))
    prev, nxt = _halo_specs(w, LRU_CHUNK, l, chunk_fn)
    vec = pl.BlockSpec((1, w), lambda b, i: (0, 0))
    blk = pl.BlockSpec((LRU_BLOCKS, w // LRU_BLOCKS, w // LRU_BLOCKS), lambda b, i: (0, 0, 0))
    in_specs = [main, prev, nxt, pl.BlockSpec((CONV_K, w), lambda b, i: (0, 0)), vec, blk, vec, blk, vec, vec]
    args = [ylru, ylru, ylru, conv_w, conv_b, wa, ba, wi, bi, lam]
    if final:
        in_specs += [main, pl.BlockSpec((None, LRU_CHUNK, w), lambda b, i: (b, chunk_fn(i), 1))]
        args += [hf, ylru]
    return pl.pallas_call(
        functools.partial(_lru_kernel, reverse=reverse, final=final, n_lat=n_lat, n_ctx=n_ctx),
        grid=(bsz, l // LRU_CHUNK),
        in_specs=in_specs,
        out_specs=main,
        out_shape=jax.ShapeDtypeStruct((bsz, l, w), F32),
        scratch_shapes=[pltpu.VMEM((LRU_CHUNK + 2 * SUBLANE, w), F32),
                        pltpu.VMEM((LRU_CHUNK, w), F32),
                        pltpu.VMEM((LRU_CHUNK, w), F32),
                        pltpu.VMEM((1, w), F32)],
        compiler_params=_cparams("arbitrary", "arbitrary"),
        name="lru_rev" if reverse else "lru_fwd",
    )(*args)


def _gdn_prep_kernel(x_ref, xp_ref, xn_ref, sm_ref, cw_ref, alog_ref, dtb_ref,
                     q_ref, k_ref, v_ref, gcol_ref, grow_ref, xe_ref, *, n_lat, n_ctx):
    rows = ROW_TILE
    c = pl.program_id(1)
    first, last = _segment_edges(c, n_lat, n_ctx)
    u = _silu(_seg_conv(xe_ref, x_ref[...], xp_ref[...], xn_ref[...], first, last, cw_ref[...], rows))
    hw = GDN_HEADS * GDN_DK
    for h in range(GDN_HEADS):
        sl = slice(h * GDN_DK, (h + 1) * GDN_DK)
        qh = u[:, h * GDN_DK:(h + 1) * GDN_DK]
        kh = u[:, hw + h * GDN_DK:hw + (h + 1) * GDN_DK]
        q_ref[:, sl] = qh * lax.rsqrt(jnp.sum(qh * qh, axis=-1, keepdims=True) + NORM_EPS) * (GDN_DK ** -0.5)
        k_ref[:, sl] = kh * lax.rsqrt(jnp.sum(kh * kh, axis=-1, keepdims=True) + NORM_EPS)
    v_ref[...] = u[:, 2 * hw:]

    sm = sm_ref[...]
    g = -jnp.exp(alog_ref[...]) * _softplus(sm + dtb_ref[...])
    beta = jax.nn.sigmoid(sm)
    ri = _iota((rows, rows), 0)
    ci = _iota((rows, rows), 1)
    same = (ri // SCAN_CHUNK) == (ci // SCAN_CHUNK)
    tri_f = jnp.where(jnp.logical_and(same, ci <= ri), 1.0, 0.0).astype(BF16)
    tri_r = jnp.where(jnp.logical_and(same, ci >= ri), 1.0, 0.0).astype(BF16)
    gcs_f = _exact_left(tri_f, g)
    gcs_r = _exact_left(tri_r, g)
    lane = _iota((rows, LANE), 1)
    gcol = jnp.where(lane < GDN_HEADS, gcs_f,
                     jnp.where(lane < 2 * GDN_HEADS, gcs_r, jnp.where(lane < 4 * GDN_HEADS, beta, 0.0)))
    gcol_ref[...] = gcol
    eye = jnp.where(_iota((LANE, LANE), 0) == _iota((LANE, LANE), 1), 1.0, 0.0).astype(BF16)
    grow_ref[...] = _exact_transpose(gcol, eye)[0:4 * GDN_HEADS, :]


def _gdn_prep(ygdn, conv_w, alog_vec, dtb_vec, n_lat, n_ctx):
    bsz, l, _ = ygdn.shape
    hw = GDN_HEADS * GDN_DK
    ident = lambda i: i
    main = pl.BlockSpec((None, ROW_TILE, 3 * hw), lambda b, i: (b, i, 0))
    prev, nxt = _halo_specs(3 * hw, ROW_TILE, l, ident)
    tok = pl.BlockSpec((None, ROW_TILE, hw), lambda b, i: (b, i, 0))
    return pl.pallas_call(
        functools.partial(_gdn_prep_kernel, n_lat=n_lat, n_ctx=n_ctx),
        grid=(bsz, l // ROW_TILE),
        in_specs=[main, prev, nxt,
                  pl.BlockSpec((None, ROW_TILE, LANE), lambda b, i: (b, i, 4 * hw // LANE)),
                  pl.BlockSpec((CONV_K, 3 * hw), lambda b, i: (0, 0)),
                  pl.BlockSpec((1, LANE), lambda b, i: (0, 0)),
                  pl.BlockSpec((1, LANE), lambda b, i: (0, 0))],
        out_specs=[tok, tok, tok,
                   pl.BlockSpec((None, ROW_TILE, LANE), lambda b, i: (b, i, 0)),
                   pl.BlockSpec((None, 4 * GDN_HEADS, ROW_TILE), lambda b, i: (b, 0, i))],
        out_shape=[jax.ShapeDtypeStruct((bsz, l, hw), F32)] * 3
        + [jax.ShapeDtypeStruct((bsz, l, LANE), F32), jax.ShapeDtypeStruct((bsz, 4 * GDN_HEADS, l), F32)],
        scratch_shapes=[pltpu.VMEM((ROW_TILE + 2 * SUBLANE, 3 * hw), F32)],
        compiler_params=_cparams("arbitrary", "arbitrary"),
        name="gdn_prep",
    )(ygdn, ygdn, ygdn, ygdn, conv_w, alog_vec, dtb_vec)


def _unit_tri_inverse(a):
    n = a.shape[0]
    ri = _iota((n, n), 0)
    ci = _iota((n, n), 1)
    t = jnp.where(ri == ci, 1.0, 0.0) - jnp.where((ri // 2) == (ci // 2), a, 0.0)
    s = 2
    while s < n:
        off = jnp.logical_and((ri // (2 * s)) == (ci // (2 * s)), (ri // s) != (ci // s))
        a_off = jnp.where(off, a, 0.0).astype(BF16)
        tb = t.astype(BF16)
        t = t - _dot(tb, _dot(a_off, tb).astype(BF16))
        s *= 2
    return t


def _gdn_scan_kernel(*refs, reverse, final, n_lat, n_ctx):
    if final:
        q_ref, k_ref, v_ref, gcol_ref, grow_ref, of_ref, z_ref, nw_ref, o_ref, s_ref = refs
    else:
        q_ref, k_ref, v_ref, gcol_ref, grow_ref, o_ref, s_ref = refs
    rows = SCAN_CHUNK
    i = pl.program_id(1)

    @pl.when(i == 0)
    def _():
        s_ref[...] = jnp.zeros_like(s_ref)

    ri = _iota((rows, rows), 0)
    ci = _iota((rows, rows), 1)
    incl = (ri <= ci) if reverse else (ri >= ci)
    strict = (ri < ci) if reverse else (ri > ci)
    d = 1 if reverse else 0
    edge = 0 if reverse else rows - 1
    gcol = gcol_ref[...]
    grow = grow_ref[...]
    for h in range(GDN_HEADS):
        sl = slice(h * GDN_DK, (h + 1) * GDN_DK)
        qh = q_ref[:, sl]
        kh = k_ref[:, sl]
        vh = v_ref[:, sl]
        li = d * GDN_HEADS + h
        gc = gcol[:, li:li + 1]
        gr = grow[li:li + 1, :]
        beta = gcol[:, 2 * GDN_HEADS + li:2 * GDN_HEADS + li + 1]
        dec = jnp.exp(jnp.where(incl, gc - gr, NEG_BIG))
        eg = jnp.exp(gc)
        kb = kh * beta
        khb = kh.astype(BF16)
        a_mat = jnp.where(strict, _dot_nt(kb.astype(BF16), khb) * dec, 0.0)
        attn = _dot_nt(qh.astype(BF16), khb) * dec
        t_inv = _unit_tri_inverse(a_mat)
        rhs = jnp.concatenate([vh * beta, kb * eg], axis=1)
        sol = _dot(t_inv.astype(BF16), rhs.astype(BF16))
        u_mat = sol[:, :GDN_DK]
        w_mat = sol[:, GDN_DK:]
        s = s_ref[h]
        ws_qs = _dot(jnp.concatenate([w_mat, qh * eg], axis=0).astype(BF16), s.astype(BF16))
        v_new = u_mat - ws_qs[:rows]
        o = ws_qs[rows:] + _dot(attn.astype(BF16), v_new.astype(BF16))
        g_end = gc[edge:edge + 1, :]
        k_dec = kh * jnp.exp(g_end - gc)
        s_ref[h] = s * jnp.exp(g_end) + _dot_tn(k_dec.astype(BF16), v_new.astype(BF16))
        if final:
            o = o + of_ref[:, sl]
            y = o * lax.rsqrt(jnp.mean(o * o, axis=-1, keepdims=True) + NORM_EPS) * nw_ref[...]
            o_ref[:, sl] = y * _silu(z_ref[:, sl])
        else:
            o_ref[:, sl] = o


def _gdn_scan(q, k, v, gcol, grow, o_fwd, ygdn, norm_w, n_lat, n_ctx, reverse):
    bsz, l, hw = q.shape
    final = o_fwd is not None
    chunk_fn = functools.partial(_chunk_of_step, n_lat=n_lat, n_ctx=n_ctx, reverse=reverse)
    tok = pl.BlockSpec((None, SCAN_CHUNK, hw), lambda b, i: (b, chunk_fn(i), 0))
    in_specs = [tok, tok, tok,
                pl.BlockSpec((None, SCAN_CHUNK, LANE), lambda b, i: (b, chunk_fn(i), 0)),
                pl.BlockSpec((None, 4 * GDN_HEADS, SCAN_CHUNK), lambda b, i: (b, 0, chunk_fn(i)))]
    args = [q, k, v, gcol, grow]
    if final:
        in_specs += [tok, pl.BlockSpec((None, SCAN_CHUNK, hw), lambda b, i: (b, chunk_fn(i), 3)),
                     pl.BlockSpec((1, GDN_DK), lambda b, i: (0, 0))]
        args += [o_fwd, ygdn, norm_w]
    return pl.pallas_call(
        functools.partial(_gdn_scan_kernel, reverse=reverse, final=final, n_lat=n_lat, n_ctx=n_ctx),
        grid=(bsz, l // SCAN_CHUNK),
        in_specs=in_specs,
        out_specs=tok,
        out_shape=jax.ShapeDtypeStruct((bsz, l, hw), F32),
        scratch_shapes=[pltpu.VMEM((GDN_HEADS, GDN_DK, GDN_DK), F32)],
        compiler_params=_cparams("arbitrary", "arbitrary"),
        name="gdn_rev" if reverse else "gdn_fwd",
    )(*args)


def _ssd_scan_kernel(*refs, reverse, final, n_lat, n_ctx):
    if final:
        (x_ref, xp_ref, xn_ref, cw_ref, cb_ref, dtb_ref, a_ref, dsk_ref, yf_ref, nw_ref,
         o_ref, xe_ref, st_ref) = refs
    else:
        (x_ref, xp_ref, xn_ref, cw_ref, cb_ref, dtb_ref, a_ref, dsk_ref, o_ref, xe_ref, st_ref) = refs
    rows = SCAN_CHUNK
    inner = SSD_HEADS * SSD_HEADDIM
    gw = inner // SSD_GROUPS
    xbc_w = inner + 2 * SSD_GROUPS * SSD_STATE
    i = pl.program_id(1)
    c = _chunk_of_step(i, n_lat, n_ctx, reverse)
    first, last = _segment_edges(c, n_lat, n_ctx)

    @pl.when(i == 0)
    def _():
        st_ref[...] = jnp.zeros_like(st_ref)

    x_all = x_ref[...]
    u = _silu(_seg_conv(xe_ref, x_all[:, inner:inner + xbc_w], xp_ref[:, inner:inner + xbc_w],
                        xn_ref[:, inner:inner + xbc_w], first, last, cw_ref[...], rows) + cb_ref[...])
    xs = u[:, :inner]
    dt = _softplus(x_all[:, inner + xbc_w:] + dtb_ref[...])
    adt = a_ref[...] * dt

    ri = _iota((rows, rows), 0)
    ci = _iota((rows, rows), 1)
    incl = (ri <= ci) if reverse else (ri >= ci)
    tri = jnp.where(incl, 1.0, 0.0).astype(BF16)
    eye = jnp.where(ri == ci, 1.0, 0.0).astype(BF16)
    acs = _exact_left(tri, adt)
    acs_t = _exact_transpose(acs, eye)
    d = 1 if reverse else 0
    edge = 0 if reverse else rows - 1
    er = _iota((LANE, inner), 0)
    ec = _iota((LANE, inner), 1)
    expand = jnp.where(er == d * SSD_HEADS + ec // SSD_HEADDIM, 1.0, 0.0).astype(BF16)
    dt_e = _exact_right(dt, expand)
    ea_e = _exact_right(jnp.exp(acs), expand)
    dc_e = _exact_right(jnp.exp(acs[edge:edge + 1, :] - acs), expand)
    xdt = xs * dt_e
    xdt_b = xdt.astype(BF16)
    xw_b = (xdt * dc_e).astype(BF16)
    lane = _iota((rows, LANE), 1)

    y_parts = []
    for g in range(SSD_GROUPS):
        bm = u[:, inner + g * SSD_STATE:inner + (g + 1) * SSD_STATE].astype(BF16)
        cm = u[:, inner + (SSD_GROUPS + g) * SSD_STATE:inner + (SSD_GROUPS + g + 1) * SSD_STATE].astype(BF16)
        cb = _dot_nt(cm, bm)
        st = st_ref[g]
        y_off = _dot(cm, st.astype(BF16)) * ea_e[:, g * gw:(g + 1) * gw]
        heads_per_group = SSD_HEADS // SSD_GROUPS
        diag = []
        for hp in range(heads_per_group // 2):
            h0 = g * heads_per_group + 2 * hp
            mats = []
            for hh in (h0, h0 + 1):
                li = d * SSD_HEADS + hh
                lm = jnp.exp(jnp.where(incl, acs[:, li:li + 1] - acs_t[li:li + 1, :], NEG_BIG))
                mats.append((cb * lm).astype(BF16))
            xp = xdt_b[:, h0 * SSD_HEADDIM:(h0 + 2) * SSD_HEADDIM]
            x_lo = jnp.where(lane < SSD_HEADDIM, xp, jnp.zeros_like(xp))
            x_hi = jnp.where(lane >= SSD_HEADDIM, xp, jnp.zeros_like(xp))
            diag.append(_dot(jnp.concatenate(mats, axis=1), jnp.concatenate([x_lo, x_hi], axis=0)))
        y_parts.append(jnp.concatenate(diag, axis=1) + y_off)
        st_ref[g] = (st * ea_e[edge:edge + 1, g * gw:(g + 1) * gw]
                     + _dot_tn(bm, xw_b[:, g * gw:(g + 1) * gw]))
    y = jnp.concatenate(y_parts, axis=1) + dsk_ref[...] * xs
    if final:
        y = y + yf_ref[...]
        yz = y * _silu(x_all[:, :inner])
        outs = []
        for g in range(SSD_GROUPS):
            yg = yz[:, g * gw:(g + 1) * gw]
            outs.append(yg * lax.rsqrt(jnp.mean(yg * yg, axis=-1, keepdims=True) + NORM_EPS))
        o_ref[...] = jnp.concatenate(outs, axis=1) * nw_ref[...]
    else:
        o_ref[...] = y


def _ssd_scan(yssd, conv_w, conv_b, dtb_vec, a_vec, dskip, y_fwd, norm_w, n_lat, n_ctx, reverse):
    bsz, l, wtot = yssd.shape
    inner = SSD_HEADS * SSD_HEADDIM
    xbc_w = inner + 2 * SSD_GROUPS * SSD_STATE
    final = y_fwd is not None
    chunk_fn = functools.partial(_chunk_of_step, n_lat=n_lat, n_ctx=n_ctx, reverse=reverse)
    main = pl.BlockSpec((None, SCAN_CHUNK, wtot), lambda b, i: (b, chunk_fn(i), 0))
    prev, nxt = _halo_specs(wtot, SCAN_CHUNK, l, chunk_fn)
    tok = pl.BlockSpec((None, SCAN_CHUNK, inner), lambda b, i: (b, chunk_fn(i), 0))
    row = lambda n: pl.BlockSpec((1, n), lambda b, i: (0, 0))
    in_specs = [main, prev, nxt, pl.BlockSpec((CONV_K, xbc_w), lambda b, i: (0, 0)), row(xbc_w),
                row(LANE), row(LANE), row(inner)]
    args = [yssd, yssd, yssd, conv_w, conv_b, dtb_vec, a_vec, dskip]
    if final:
        in_specs += [tok, row(inner)]
        args += [y_fwd, norm_w]
    return pl.pallas_call(
        functools.partial(_ssd_scan_kernel, reverse=reverse, final=final, n_lat=n_lat, n_ctx=n_ctx),
        grid=(bsz, l // SCAN_CHUNK),
        in_specs=in_specs,
        out_specs=tok,
        out_shape=jax.ShapeDtypeStruct((bsz, l, inner), F32),
        scratch_shapes=[pltpu.VMEM((SCAN_CHUNK + 2 * SUBLANE, xbc_w), F32),
                        pltpu.VMEM((SSD_GROUPS, SSD_STATE, inner // SSD_GROUPS), F32)],
        compiler_params=_cparams("arbitrary", "arbitrary"),
        name="ssd_rev" if reverse else "ssd_fwd",
    )(*args)


def _merge_kernel(x_ref, lru_ref, gdn_ref, ssd_ref, gate_ref, mod_ref, nw_ref, wb_ref, wo_ref, wr_ref, br_ref,
                  xo_ref, h_ref, lg_ref):
    d = x_ref.shape[1]
    acc = None
    for k, b_ref in enumerate((lru_ref, gdn_ref, ssd_ref)):
        t = jax.nn.sigmoid(gate_ref[:, k * d:(k + 1) * d]) * _dot(b_ref[...].astype(BF16), wb_ref[k])
        acc = t if acc is None else acc + t
    mix = _dot(acc.astype(BF16), wo_ref[...])
    x = x_ref[...] + mod_ref[2:3, :] * mix
    xo_ref[...] = x
    h = _mod_norm(x, nw_ref[...], mod_ref[3:4, :], mod_ref[4:5, :])
    h_ref[...] = h.astype(h_ref.dtype)
    lg_ref[...] = jnp.dot(h, wr_ref[...], preferred_element_type=F32,
                          precision=lax.Precision.HIGHEST) + br_ref[...]


def _merge(x, lru, gdn, ssd, gate, modtab, norm_w, wb, wo, wr, br, n_lat_tiles):
    bsz, l, d = x.shape
    tok = pl.BlockSpec((None, ROW_TILE, d), lambda b, j: (b, j, 0))
    return pl.pallas_call(
        _merge_kernel,
        grid=(bsz, l // ROW_TILE),
        in_specs=[tok, tok, tok, tok,
                  pl.BlockSpec((None, ROW_TILE, 3 * d), lambda b, j: (b, j, 0)),
                  pl.BlockSpec((None, None, SUBLANE, d), lambda b, j: (b, jnp.where(j >= n_lat_tiles, 1, 0), 0, 0)),
                  pl.BlockSpec((1, d), lambda b, j: (0, 0)),
                  pl.BlockSpec((3, d, d), lambda b, j: (0, 0, 0)),
                  pl.BlockSpec((d, d), lambda b, j: (0, 0)),
                  pl.BlockSpec((d, LANE), lambda b, j: (0, 0)),
                  pl.BlockSpec((1, LANE), lambda b, j: (0, 0))],
        out_specs=[tok, tok, pl.BlockSpec((None, ROW_TILE, LANE), lambda b, j: (b, j, 0))],
        out_shape=[jax.ShapeDtypeStruct((bsz, l, d), F32), jax.ShapeDtypeStruct((bsz, l, d), BF16),
                   jax.ShapeDtypeStruct((bsz, l, LANE), F32)],
        compiler_params=_cparams("arbitrary", "arbitrary"),
        name="merge",
    )(x, lru, gdn, ssd, gate, modtab, norm_w, wb, wo, wr, br)


def _moe_kernel(be_ref, na_ref, x_ref, w1_ref, w3_ref, w2_ref, o_ref):
    i = pl.program_id(0)

    @pl.when(i < na_ref[0])
    def _():
        x = x_ref[...]
        mid = _silu(_dot(x, w1_ref[...])) * _dot(x, w3_ref[...])
        o_ref[...] = _dot(mid.astype(BF16), w2_ref[...])

    @pl.when(i >= na_ref[0])
    def _():
        o_ref[...] = jnp.zeros_like(o_ref)


def _moe_experts(xb, block_expert, n_active, w1, w3, w2):
    n_rows, d = xb.shape
    ff = w1.shape[-1]
    grid_spec = pltpu.PrefetchScalarGridSpec(
        num_scalar_prefetch=2,
        grid=(n_rows // MOE_ROWS,),
        in_specs=[pl.BlockSpec((MOE_ROWS, d), lambda i, be, na: (i, 0)),
                  pl.BlockSpec((None, d, ff), lambda i, be, na: (be[i], 0, 0)),
                  pl.BlockSpec((None, d, ff), lambda i, be, na: (be[i], 0, 0)),
                  pl.BlockSpec((None, ff, d), lambda i, be, na: (be[i], 0, 0))],
        out_specs=pl.BlockSpec((MOE_ROWS, d), lambda i, be, na: (i, 0)),
    )
    return pl.pallas_call(
        _moe_kernel,
        grid_spec=grid_spec,
        out_shape=jax.ShapeDtypeStruct((n_rows, d), F32),
        compiler_params=_cparams("arbitrary"),
        name="moe_experts",
    )(block_expert, n_active, xb, w1, w3, w2)


def _route(logits, bsz_tokens):
    t = logits.shape[0]
    g_logits = logits[:, :N_GROUPS]
    e_logits = logits[:, N_GROUPS:N_GROUPS + N_EXPERTS].reshape(t, N_GROUPS, EXPERTS_PER_GROUP)
    g_prob = jax.nn.softmax(g_logits, axis=-1)
    g_idx = jnp.argmax(g_logits, axis=-1)
    p_group = jnp.take_along_axis(g_prob, g_idx[:, None], axis=1)[:, 0]
    e_in_group = jnp.take_along_axis(e_logits, g_idx[:, None, None], axis=1)[:, 0]
    top_v, top_i = lax.top_k(e_in_group, TOP_K)
    weights = jax.nn.softmax(top_v, axis=-1) * p_group[:, None]
    expert_id = (g_idx[:, None] * EXPERTS_PER_GROUP + top_i).reshape(-1).astype(jnp.int32)
    n_assign = t * TOP_K
    order = jnp.argsort(expert_id)
    e_sorted = expert_id[order]
    counts = jnp.bincount(expert_id, length=N_EXPERTS)
    padded = (counts + MOE_ROWS - 1) // MOE_ROWS * MOE_ROWS
    starts = jnp.cumsum(counts) - counts
    pad_ends = jnp.cumsum(padded)
    pad_starts = pad_ends - padded
    dest = (pad_starts[e_sorted] + jnp.arange(n_assign, dtype=jnp.int32) - starts[e_sorted]).astype(jnp.int32)
    n_blocks = -(-(n_assign + N_EXPERTS * (MOE_ROWS - 1)) // MOE_ROWS)
    n_rows = n_blocks * MOE_ROWS
    tok_buf = jnp.zeros((n_rows,), jnp.int32).at[dest].set((order // TOP_K).astype(jnp.int32))
    slot = jnp.zeros((n_assign,), jnp.int32).at[order].set(dest).reshape(t, TOP_K)
    block_start = jnp.arange(n_blocks, dtype=jnp.int32) * MOE_ROWS
    block_expert = jnp.minimum(jnp.sum(block_start[:, None] >= pad_ends[None, :], axis=1),
                               N_EXPERTS - 1).astype(jnp.int32)
    n_active = (pad_ends[-1] // MOE_ROWS).astype(jnp.int32).reshape(1)
    return tok_buf, block_expert, n_active, slot, weights


def _combine_kernel(x_ref, y_ref, mod_ref, nw_ref, o_ref, *, final):
    x = x_ref[...] + mod_ref[5:6, :] * y_ref[...]
    if final:
        x = x * lax.rsqrt(jnp.mean(x * x, axis=-1, keepdims=True) + NORM_EPS) * nw_ref[...]
    o_ref[...] = x


def _combine(x, y, modtab, norm_w, n_lat_tiles, final):
    bsz, l, d = x.shape
    tok = pl.BlockSpec((None, ROW_TILE, d), lambda b, j: (b, j, 0))
    return pl.pallas_call(
        functools.partial(_combine_kernel, final=final),
        grid=(bsz, l // ROW_TILE),
        in_specs=[tok, tok,
                  pl.BlockSpec((None, None, SUBLANE, d), lambda b, j: (b, jnp.where(j >= n_lat_tiles, 1, 0), 0, 0)),
                  pl.BlockSpec((1, d), lambda b, j: (0, 0))],
        out_specs=tok,
        out_shape=jax.ShapeDtypeStruct((bsz, l, d), F32),
        compiler_params=_cparams("arbitrary", "arbitrary"),
        name="combine",
    )(x, y, modtab, norm_w)


def _pad_lanes(v, n=LANE):
    v = v.reshape(1, -1).astype(F32)
    return jnp.pad(v, ((0, 0), (0, n - v.shape[1])))


def kernel(x, c, ctx, c_ctx, w_mod, b_mod, norm1_w, norm2_w, w_in, lru_conv_w, lru_conv_b, lru_wa, lru_ba,
           lru_wi, lru_bi, lru_lambda, gdn_conv_w, gdn_a_log, gdn_dt_bias, gdn_norm_w, ssd_conv_w, ssd_conv_b,
           ssd_a_log, ssd_dt_bias, ssd_d, ssd_norm_w, w_branch, w_out, router_group_w, router_group_b,
           router_expert_w, router_expert_b, expert_w1, expert_w3, expert_w2, final_norm_w):
    bsz, seq, d = x.shape
    n_ctx_tok = ctx.shape[1]
    depth = w_mod.shape[0]
    l = seq + n_ctx_tok
    rows = seq // GRID_W
    assert seq % LRU_CHUNK == 0 and n_ctx_tok % LRU_CHUNK == 0 and ROW_TILE == LRU_CHUNK
    assert bsz + 1 <= SUBLANE
    hw = GDN_HEADS * GDN_DK
    inner = SSD_HEADS * SSD_HEADDIM
    xbc_w = inner + 2 * SSD_GROUPS * SSD_STATE

    act = jnp.concatenate([_silu(c), _silu(c_ctx)[None, :],
                           jnp.zeros((SUBLANE - bsz - 1, d), F32)], axis=0)
    mod_all = _modulation(act, w_mod, b_mod).reshape(depth, SUBLANE, 6, d)
    pad2 = jnp.zeros((bsz, 2, d), F32)

    xs = jnp.concatenate([x, ctx], axis=1)
    nl_t, nc_t = seq // ROW_TILE, n_ctx_tok // ROW_TILE
    nl_s, nc_s = seq // SCAN_CHUNK, n_ctx_tok // SCAN_CHUNK

    o_lx, o_qkv, o_gz, o_gb, o_ga = 0, 2 * d, 2 * d + 3 * hw, 2 * d + 4 * hw, 2 * d + 4 * hw + 2 * GDN_HEADS
    o_sz = o_ga + 2 * GDN_HEADS
    o_xbc = o_sz + inner
    o_sdt = o_xbc + xbc_w
    o_gate = o_sdt + 2 * SSD_HEADS

    for i in range(depth):
        lat = mod_all[i, :bsz]
        cx = jnp.broadcast_to(mod_all[i, bsz][None], (bsz, 6, d))
        modtab = jnp.stack([jnp.concatenate([lat, pad2], axis=1), jnp.concatenate([cx, pad2], axis=1)], axis=1)

        wi_ = w_in[i]
        w_lru = wi_[:, o_lx:o_qkv].astype(BF16)
        zpad = jnp.zeros((d, LANE - 4 * GDN_HEADS), F32)
        w_gdn = jnp.concatenate([wi_[:, o_qkv:o_gb], wi_[:, o_ga:o_sz], wi_[:, o_gb:o_ga], zpad], axis=1).astype(BF16)
        zpad2 = jnp.zeros((d, LANE - 2 * SSD_HEADS), F32)
        w_ssd = jnp.concatenate([wi_[:, o_sz:o_gate], zpad2], axis=1).astype(BF16)
        w_gate = wi_[:, o_gate:].astype(BF16)
        n1 = norm1_w[i].reshape(1, d)

        ylru = _norm_matmul(xs, modtab, n1, w_lru, 0, 1, nl_t, "in_lru")
        ygdn = _norm_matmul(xs, modtab, n1, w_gdn, 0, 1, nl_t, "in_gdn")
        gate = _norm_matmul(xs, modtab, n1, w_gate, 0, 1, nl_t, "in_gate")
        x_cm = xs[:, :seq].reshape(bsz, rows, GRID_W, d).swapaxes(1, 2).reshape(bsz, seq, d)
        xs_scan = jnp.concatenate([x_cm, xs[:, seq:]], axis=1)
        yssd = _norm_matmul(xs_scan, modtab, n1, w_ssd, 0, 1, nl_t, "in_ssd")

        lru_args = (lru_conv_w[i], lru_conv_b[i].reshape(1, d))
        h_f = _lru_scan(ylru, *lru_args, lru_wa[i, 0].astype(BF16), lru_ba[i, 0].reshape(1, d),
                        lru_wi[i, 0].astype(BF16), lru_bi[i, 0].reshape(1, d), lru_lambda[i, 0].reshape(1, d),
                        None, nl_t, nc_t, False)
        lru_out = _lru_scan(ylru, *lru_args, lru_wa[i, 1].astype(BF16), lru_ba[i, 1].reshape(1, d),
                            lru_wi[i, 1].astype(BF16), lru_bi[i, 1].reshape(1, d), lru_lambda[i, 1].reshape(1, d),
                            h_f, nl_t, nc_t, True)

        q, k, v, gcol, grow = _gdn_prep(ygdn, gdn_conv_w[i], _pad_lanes(gdn_a_log[i]), _pad_lanes(gdn_dt_bias[i]),
                                        nl_t, nc_t)
        o_f = _gdn_scan(q, k, v, gcol, grow, None, None, None, nl_s, nc_s, False)
        gdn_out = _gdn_scan(q, k, v, gcol, grow, o_f, ygdn, gdn_norm_w[i].reshape(1, GDN_DK), nl_s, nc_s, True)

        dtb_vec = _pad_lanes(ssd_dt_bias[i])
        a_vec = _pad_lanes(-jnp.exp(ssd_a_log[i].astype(F32)))
        ssd_cw, ssd_cb = ssd_conv_w[i], ssd_conv_b[i].reshape(1, xbc_w)
        dsk = [jnp.repeat(ssd_d[i, dd], SSD_HEADDIM).reshape(1, inner) for dd in range(2)]
        y_f = _ssd_scan(yssd, ssd_cw, ssd_cb, dtb_vec, a_vec, dsk[0], None, None, nl_s, nc_s, False)
        ssd_scan_out = _ssd_scan(yssd, ssd_cw, ssd_cb, dtb_vec, a_vec, dsk[1], y_f,
                                 ssd_norm_w[i].reshape(1, inner), nl_s, nc_s, True)
        ssd_lat = ssd_scan_out[:, :seq].reshape(bsz, GRID_W, rows, inner).swapaxes(1, 2).reshape(bsz, seq, inner)
        ssd_out = jnp.concatenate([ssd_lat, ssd_scan_out[:, seq:]], axis=1)

        w_r = jnp.concatenate([router_group_w[i], router_expert_w[i],
                               jnp.zeros((d, LANE - N_GROUPS - N_EXPERTS), F32)], axis=1)
        b_r = _pad_lanes(jnp.concatenate([router_group_b[i], router_expert_b[i]]))
        x_mid, h2, logits = _merge(xs, lru_out, gdn_out, ssd_out, gate, modtab, norm2_w[i].reshape(1, d),
                                   w_branch[i].astype(BF16), w_out[i].astype(BF16), w_r, b_r, nl_t)

        t = bsz * l
        tok_buf, block_expert, n_active, slot, weights = _route(logits.reshape(t, LANE), t)
        xb = h2.reshape(t, d)[tok_buf]
        yb = _moe_experts(xb, block_expert, n_active, expert_w1[i].astype(BF16), expert_w3[i].astype(BF16),
                          expert_w2[i].astype(BF16))
        y = (yb[slot[:, 0]] * weights[:, 0:1] + yb[slot[:, 1]] * weights[:, 1:2]).reshape(bsz, l, d)
        last = i == depth - 1
        xs = _combine(x_mid, y, modtab, final_norm_w.reshape(1, d), nl_t, last)

    return xs[:, :seq]
```

```python
import functools

import jax
import jax.numpy as jnp
from jax import lax
from jax.experimental import pallas as pl
from jax.experimental.pallas import tpu as pltpu

GRID_W = 64
CONV_K = 4
NORM_EPS = 1e-6
LRU_BLOCKS = 8
LRU_C = 8.0
GDN_HEADS = 8
GDN_DK = 128
SSD_HEADS = 16
SSD_HEADDIM = 64
SSD_GROUPS = 2
SSD_STATE = 128
N_GROUPS = 4
EXPERTS_PER_GROUP = 8
N_EXPERTS = N_GROUPS * EXPERTS_PER_GROUP
TOP_K = 2

LANE = 128
SUBLANE = 8
ROW_TILE = 256
LRU_CHUNK = 256
SCAN_CHUNK = 128
MOE_ROWS = 256
VMEM_LIMIT = 56 * 1024 * 1024

F32 = jnp.float32
BF16 = jnp.bfloat16
NEG_BIG = -1e30


def _cparams(*sem):
    return pltpu.CompilerParams(dimension_semantics=sem, vmem_limit_bytes=VMEM_LIMIT)


def _dot(a, b):
    return jnp.dot(a, b, preferred_element_type=F32)


def _dot_nt(a, b):
    return lax.dot_general(a, b, (((1,), (1,)), ((), ())), preferred_element_type=F32)


def _dot_tn(a, b):
    return lax.dot_general(a, b, (((0,), (0,)), ((), ())), preferred_element_type=F32)


def _split3(x):
    hi = x.astype(BF16)
    r1 = x - hi.astype(F32)
    mid = r1.astype(BF16)
    lo = (r1 - mid.astype(F32)).astype(BF16)
    return hi, mid, lo


def _exact_left(m_bf, x):
    hi, mid, lo = _split3(x)
    return _dot(jnp.concatenate([m_bf, m_bf, m_bf], axis=1), jnp.concatenate([hi, mid, lo], axis=0))


def _exact_right(x, m_bf):
    hi, mid, lo = _split3(x)
    return _dot(jnp.concatenate([hi, mid, lo], axis=1), jnp.concatenate([m_bf, m_bf, m_bf], axis=0))


def _exact_transpose(x, eye_bf):
    hi, mid, lo = _split3(x)
    return _dot_nt(jnp.concatenate([eye_bf, eye_bf, eye_bf], axis=1), jnp.concatenate([hi, mid, lo], axis=1))


def _softplus(x):
    return jnp.maximum(x, 0.0) + jnp.log(1.0 + jnp.exp(-jnp.abs(x)))


def _silu(x):
    return x * jax.nn.sigmoid(x)


def _iota(shape, dim):
    return lax.broadcasted_iota(jnp.int32, shape, dim)


def _fwd_chunk(i, n_lat, n_ctx):
    return jnp.where(i < n_ctx, n_lat + i, i - n_ctx)


def _chunk_of_step(i, n_lat, n_ctx, reverse):
    return (n_lat + n_ctx - 1 - i) if reverse else _fwd_chunk(i, n_lat, n_ctx)


def _segment_edges(c, n_lat, n_ctx):
    first = jnp.logical_or(c == 0, c == n_lat)
    last = jnp.logical_or(c == n_lat - 1, c == n_lat + n_ctx - 1)
    return first, last


def _seg_conv(xe_ref, x, prev8, next8, first, last, cw, rows):
    xe_ref[0:SUBLANE, :] = jnp.where(first, 0.0, prev8)
    xe_ref[SUBLANE:SUBLANE + rows, :] = x
    xe_ref[SUBLANE + rows:2 * SUBLANE + rows, :] = jnp.where(last, 0.0, next8)
    base = SUBLANE - CONV_K // 2
    u = cw[0:1, :] * xe_ref[base:base + rows, :]
    for j in range(1, CONV_K):
        u = u + cw[j:j + 1, :] * xe_ref[base + j:base + j + rows, :]
    return u


def _halo_specs(width, rows, n_rows_total, chunk_fn, col_block=0):
    per = rows // SUBLANE
    last_tile = n_rows_total // SUBLANE - 1
    prev = pl.BlockSpec((None, SUBLANE, width),
                        lambda b, i: (b, jnp.maximum(chunk_fn(i) * per - 1, 0), col_block))
    nxt = pl.BlockSpec((None, SUBLANE, width),
                       lambda b, i: (b, jnp.minimum((chunk_fn(i) + 1) * per, last_tile), col_block))
    return prev, nxt


def _mod_kernel(a_ref, w_ref, b_ref, o_ref):
    o_ref[...] = _dot(a_ref[...].astype(BF16), w_ref[...].astype(BF16)) + b_ref[...]


def _modulation(act, w_mod, b_mod):
    depth, d, n = w_mod.shape
    tn = 1536
    return pl.pallas_call(
        _mod_kernel,
        grid=(depth, n // tn),
        in_specs=[pl.BlockSpec((SUBLANE, d), lambda l, j: (0, 0)),
                  pl.BlockSpec((None, d, tn), lambda l, j: (l, 0, j)),
                  pl.BlockSpec((None, 1, tn), lambda l, j: (l, 0, j))],
        out_specs=pl.BlockSpec((None, SUBLANE, tn), lambda l, j: (l, 0, j)),
        out_shape=jax.ShapeDtypeStruct((depth, SUBLANE, n), F32),
        compiler_params=_cparams("arbitrary", "arbitrary"),
        name="modulation",
    )(act, w_mod, b_mod.reshape(depth, 1, n))


def _mod_norm(x, nw, shift, scale):
    y = x * lax.rsqrt(jnp.mean(x * x, axis=-1, keepdims=True) + NORM_EPS)
    return (y * nw) * (1.0 + scale) + shift


def _norm_matmul_kernel(x_ref, mod_ref, nw_ref, w_ref, o_ref, *, shift_row, scale_row, col_step):
    h = _mod_norm(x_ref[...], nw_ref[...], mod_ref[shift_row:shift_row + 1, :],
                  mod_ref[scale_row:scale_row + 1, :]).astype(BF16)
    n = w_ref.shape[1]
    for c0 in range(0, n, col_step):
        c1 = min(c0 + col_step, n)
        o_ref[:, c0:c1] = _dot(h, w_ref[:, c0:c1]).astype(o_ref.dtype)


def _norm_matmul(x, modtab, norm_w, w, shift_row, scale_row, n_lat_tiles, name):
    bsz, l, d = x.shape
    n = w.shape[1]
    kern = functools.partial(_norm_matmul_kernel, shift_row=shift_row, scale_row=scale_row, col_step=512)
    return pl.pallas_call(
        kern,
        grid=(bsz, l // ROW_TILE),
        in_specs=[pl.BlockSpec((None, ROW_TILE, d), lambda b, j: (b, j, 0)),
                  pl.BlockSpec((None, None, SUBLANE, d), lambda b, j: (b, jnp.where(j >= n_lat_tiles, 1, 0), 0, 0)),
                  pl.BlockSpec((1, d), lambda b, j: (0, 0)),
                  pl.BlockSpec((d, n), lambda b, j: (0, 0))],
        out_specs=pl.BlockSpec((None, ROW_TILE, n), lambda b, j: (b, j, 0)),
        out_shape=jax.ShapeDtypeStruct((bsz, l, n), F32),
        compiler_params=_cparams("arbitrary", "arbitrary"),
        name=name,
    )(x, modtab, norm_w, w)


def _lru_kernel(*refs, reverse, final, n_lat, n_ctx):
    if final:
        (x_ref, xp_ref, xn_ref, cw_ref, cb_ref, wa_ref, ba_ref, wi_ref, bi_ref, lam_ref, hf_ref, y_ref,
         o_ref, xe_ref, a_ref, h_ref, carry_ref) = refs
    else:
        (x_ref, xp_ref, xn_ref, cw_ref, cb_ref, wa_ref, ba_ref, wi_ref, bi_ref, lam_ref,
         o_ref, xe_ref, a_ref, h_ref, carry_ref) = refs
    rows = LRU_CHUNK
    i = pl.program_id(1)
    c = _chunk_of_step(i, n_lat, n_ctx, reverse)
    first, last = _segment_edges(c, n_lat, n_ctx)

    @pl.when(i == 0)
    def _():
        carry_ref[...] = jnp.zeros_like(carry_ref)

    u = _seg_conv(xe_ref, x_ref[...], xp_ref[...], xn_ref[...], first, last, cw_ref[...], rows) + cb_ref[...]
    ub = u.astype(BF16)
    bw = u.shape[1] // LRU_BLOCKS
    pre_r = jnp.concatenate([_dot(ub[:, n * bw:(n + 1) * bw], wa_ref[n]) for n in range(LRU_BLOCKS)], axis=1)
    pre_i = jnp.concatenate([_dot(ub[:, n * bw:(n + 1) * bw], wi_ref[n]) for n in range(LRU_BLOCKS)], axis=1)
    r = jax.nn.sigmoid(pre_r + ba_ref[...])
    gi = jax.nn.sigmoid(pre_i + bi_ref[...])
    a = jnp.exp(-LRU_C * r * _softplus(-lam_ref[...]))
    h = jnp.sqrt(1.0 - a * a) * gi * u

    rowmod = jnp.bitwise_and(_iota(a.shape, 0), SUBLANE - 1)
    for s in (1, 2, 4):
        if reverse:
            a_s = pltpu.roll(a, rows - s, axis=0)
            h_s = pltpu.roll(h, rows - s, axis=0)
            keep = rowmod < SUBLANE - s
        else:
            a_s = pltpu.roll(a, s, axis=0)
            h_s = pltpu.roll(h, s, axis=0)
            keep = rowmod >= s
        h = h + a * jnp.where(keep, h_s, 0.0)
        a = a * jnp.where(keep, a_s, 1.0)
    a_ref[...] = a
    h_ref[...] = h

    n_tiles = rows // SUBLANE
    carry = carry_ref[...]
    order = range(n_tiles - 1, -1, -1) if reverse else range(n_tiles)
    for k in order:
        sl = slice(k * SUBLANE, (k + 1) * SUBLANE)
        hk = h_ref[sl, :] + a_ref[sl, :] * carry
        carry = hk[0:1, :] if reverse else hk[SUBLANE - 1:SUBLANE, :]
        if final:
            o_ref[sl, :] = (hk + hf_ref[sl, :]) * jax.nn.gelu(y_ref[sl, :], approximate=True)
        else:
            o_ref[sl, :] = hk
    carry_ref[...] = carry


def _lru_scan(ylru, conv_w, conv_b, wa, ba, wi, bi, lam, hf, n_lat, n_ctx, reverse):
    bsz, l, w2 = ylru.shape
    w = w2 // 2
    final = hf is not None
    chunk_fn = functools.partial(_chunk_of_step, n_lat=n_lat, n_ctx=n_ctx, reverse=reverse)
    main = pl.BlockSpec((None, LRU_CHUNK, w), lambda b, i: (b, chunk_fn(i), 0))
    prev, nxt = _halo_specs(w, LRU_CHUNK, l, chunk_fn)
    vec = pl.BlockSpec((1, w), lambda b, i: (0, 0))
    blk = pl.BlockSpec((LRU_BLOCKS, w // LRU_BLOCKS, w // LRU_BLOCKS), lambda b, i: (0, 0, 0))
    in_specs = [main, prev, nxt, pl.BlockSpec((CONV_K, w), lambda b, i: (0, 0)), vec, blk, vec, blk, vec, vec]
    args = [ylru, ylru, ylru, conv_w, conv_b, wa, ba, wi, bi, lam]
    if final:
        in_specs += [main, pl.BlockSpec((None, LRU_CHUNK, w), lambda b, i: (b, chunk_fn(i), 1))]
        args += [hf, ylru]
    return pl.pallas_call(
        functools.partial(_lru_kernel, reverse=reverse, final=final, n_lat=n_lat, n_ctx=n_ctx),
        grid=(bsz, l // LRU_CHUNK),
        in_specs=in_specs,
        out_specs=main,
        out_shape=jax.ShapeDtypeStruct((bsz, l, w), F32),
        scratch_shapes=[pltpu.VMEM((LRU_CHUNK + 2 * SUBLANE, w), F32),
                        pltpu.VMEM((LRU_CHUNK, w), F32),
                        pltpu.VMEM((LRU_CHUNK, w), F32),
                        pltpu.VMEM((1, w), F32)],
        compiler_params=_cparams("arbitrary", "arbitrary"),
        name="lru_rev" if reverse else "lru_fwd",
    )(*args)


def _gdn_prep_kernel(x_ref, xp_ref, xn_ref, sm_ref, cw_ref, alog_ref, dtb_ref,
                     q_ref, k_ref, v_ref, gcol_ref, grow_ref, xe_ref, *, n_lat, n_ctx):
    rows = ROW_TILE
    c = pl.program_id(1)
    first, last = _segment_edges(c, n_lat, n_ctx)
    u = _silu(_seg_conv(xe_ref, x_ref[...], xp_ref[...], xn_ref[...], first, last, cw_ref[...], rows))
    hw = GDN_HEADS * GDN_DK
    for h in range(GDN_HEADS):
        sl = slice(h * GDN_DK, (h + 1) * GDN_DK)
        qh = u[:, h * GDN_DK:(h + 1) * GDN_DK]
        kh = u[:, hw + h * GDN_DK:hw + (h + 1) * GDN_DK]
        q_ref[:, sl] = qh * lax.rsqrt(jnp.sum(qh * qh, axis=-1, keepdims=True) + NORM_EPS) * (GDN_DK ** -0.5)
        k_ref[:, sl] = kh * lax.rsqrt(jnp.sum(kh * kh, axis=-1, keepdims=True) + NORM_EPS)
    v_ref[...] = u[:, 2 * hw:]

    sm = sm_ref[...]
    g = -jnp.exp(alog_ref[...]) * _softplus(sm + dtb_ref[...])
    beta = jax.nn.sigmoid(sm)
    ri = _iota((rows, rows), 0)
    ci = _iota((rows, rows), 1)
    same = (ri // SCAN_CHUNK) == (ci // SCAN_CHUNK)
    tri_f = jnp.where(jnp.logical_and(same, ci <= ri), 1.0, 0.0).astype(BF16)
    tri_r = jnp.where(jnp.logical_and(same, ci >= ri), 1.0, 0.0).astype(BF16)
    gcs_f = _exact_left(tri_f, g)
    gcs_r = _exact_left(tri_r, g)
    lane = _iota((rows, LANE), 1)
    gcol = jnp.where(lane < GDN_HEADS, gcs_f,
                     jnp.where(lane < 2 * GDN_HEADS, gcs_r, jnp.where(lane < 4 * GDN_HEADS, beta, 0.0)))
    gcol_ref[...] = gcol
    eye = jnp.where(_iota((LANE, LANE), 0) == _iota((LANE, LANE), 1), 1.0, 0.0).astype(BF16)
    grow_ref[...] = _exact_transpose(gcol, eye)[0:4 * GDN_HEADS, :]


def _gdn_prep(ygdn, conv_w, alog_vec, dtb_vec, n_lat, n_ctx):
    bsz, l, _ = ygdn.shape
    hw = GDN_HEADS * GDN_DK
    ident = lambda i: i
    main = pl.BlockSpec((None, ROW_TILE, 3 * hw), lambda b, i: (b, i, 0))
    prev, nxt = _halo_specs(3 * hw, ROW_TILE, l, ident)
    tok = pl.BlockSpec((None, ROW_TILE, hw), lambda b, i: (b, i, 0))
    return pl.pallas_call(
        functools.partial(_gdn_prep_kernel, n_lat=n_lat, n_ctx=n_ctx),
        grid=(bsz, l // ROW_TILE),
        in_specs=[main, prev, nxt,
                  pl.BlockSpec((None, ROW_TILE, LANE), lambda b, i: (b, i, 4 * hw // LANE)),
                  pl.BlockSpec((CONV_K, 3 * hw), lambda b, i: (0, 0)),
                  pl.BlockSpec((1, LANE), lambda b, i: (0, 0)),
                  pl.BlockSpec((1, LANE), lambda b, i: (0, 0))],
        out_specs=[tok, tok, tok,
                   pl.BlockSpec((None, ROW_TILE, LANE), lambda b, i: (b, i, 0)),
                   pl.BlockSpec((None, 4 * GDN_HEADS, ROW_TILE), lambda b, i: (b, 0, i))],
        out_shape=[jax.ShapeDtypeStruct((bsz, l, hw), F32)] * 3
        + [jax.ShapeDtypeStruct((bsz, l, LANE), F32), jax.ShapeDtypeStruct((bsz, 4 * GDN_HEADS, l), F32)],
        scratch_shapes=[pltpu.VMEM((ROW_TILE + 2 * SUBLANE, 3 * hw), F32)],
        compiler_params=_cparams("arbitrary", "arbitrary"),
        name="gdn_prep",
    )(ygdn, ygdn, ygdn, ygdn, conv_w, alog_vec, dtb_vec)


def _pair_blockdiag(x):
    c = x.shape[0]
    z = jnp.zeros((c, c), x.dtype)
    return jnp.concatenate([jnp.concatenate([x[:, :c], z], axis=1),
                            jnp.concatenate([z, x[:, c:]], axis=1)], axis=0)


def _gdn_scan_kernel(*refs, reverse, final, n_lat, n_ctx):
    if final:
        q_ref, k_ref, v_ref, gcol_ref, grow_ref, of_ref, z_ref, nw_ref, o_ref, s_ref = refs
    else:
        q_ref, k_ref, v_ref, gcol_ref, grow_ref, o_ref, s_ref = refs
    rows = SCAN_CHUNK
    n_pairs = GDN_HEADS // 2
    pw = 2 * GDN_DK
    i = pl.program_id(1)

    @pl.when(i == 0)
    def _():
        s_ref[...] = jnp.zeros_like(s_ref)

    ri = _iota((rows, rows), 0)
    ci = _iota((rows, rows), 1)
    incl = (ri <= ci) if reverse else (ri >= ci)
    ri2 = _iota((rows, pw), 0)
    ci2 = jnp.bitwise_and(_iota((rows, pw), 1), rows - 1)
    strict2 = (ri2 < ci2) if reverse else (ri2 > ci2)
    d = 1 if reverse else 0
    edge = 0 if reverse else rows - 1
    gcol = gcol_ref[...]
    grow = grow_ref[...]

    def pair_cols(lane0):
        return jnp.concatenate([jnp.broadcast_to(gcol[:, lane0 + j:lane0 + j + 1], (rows, GDN_DK))
                                for j in range(2)], axis=1)

    qs, ks, kbs, egs, gcs, decs, rhs, sts = [], [], [], [], [], [], [], []
    for p in range(n_pairs):
        sl = slice(p * pw, (p + 1) * pw)
        li = d * GDN_HEADS + 2 * p
        q2, k2, v2 = q_ref[:, sl], k_ref[:, sl], v_ref[:, sl]
        gc2 = pair_cols(li)
        beta2 = pair_cols(2 * GDN_HEADS + li)
        eg2 = jnp.exp(gc2)
        kb2 = k2 * beta2
        dec2 = jnp.concatenate(
            [jnp.exp(jnp.where(incl, gcol[:, li + j:li + j + 1] - grow[li + j:li + j + 1, :], NEG_BIG))
             for j in range(2)], axis=1)
        qs.append(q2)
        ks.append(k2)
        kbs.append(kb2)
        egs.append(eg2)
        gcs.append(gc2)
        decs.append(dec2)
        rhs.append([jnp.concatenate([v2[:, j * GDN_DK:(j + 1) * GDN_DK] * beta2[:, j * GDN_DK:(j + 1) * GDN_DK],
                                     kb2[:, j * GDN_DK:(j + 1) * GDN_DK] * eg2[:, j * GDN_DK:(j + 1) * GDN_DK]],
                                    axis=1).astype(BF16) for j in range(2)])
        sts.append(s_ref[p])

    a_mats, attns = [], []
    for p in range(n_pairs):
        kq = _dot_nt(jnp.concatenate([kbs[p], qs[p]], axis=0).astype(BF16), _pair_blockdiag(ks[p].astype(BF16)))
        a_mats.append(jnp.where(strict2, kq[:rows] * decs[p], 0.0))
        attns.append((kq[rows:] * decs[p]).astype(BF16))

    eye2 = jnp.where(ri2 == ci2, 1.0, 0.0)
    ts = [eye2 - jnp.where((ri2 // 2) == (ci2 // 2), a_mats[p], 0.0) for p in range(n_pairs)]
    s = 2
    while s < rows:
        off = jnp.logical_and((ri2 // (2 * s)) == (ci2 // (2 * s)), (ri2 // s) != (ci2 // s))
        xs = [_dot(jnp.where(off, a_mats[p], 0.0).astype(BF16), _pair_blockdiag(ts[p].astype(BF16)))
              for p in range(n_pairs)]
        ts = [ts[p] - _dot(ts[p].astype(BF16), _pair_blockdiag(xs[p].astype(BF16))) for p in range(n_pairs)]
        s *= 2

    us, ws = [], []
    for p in range(n_pairs):
        tb = ts[p].astype(BF16)
        sol = [_dot(tb[:, j * GDN_DK:(j + 1) * GDN_DK], rhs[p][j]) for j in range(2)]
        us.append(jnp.concatenate([sol[0][:, :GDN_DK], sol[1][:, :GDN_DK]], axis=1))
        ws.append(jnp.concatenate([sol[0][:, GDN_DK:], sol[1][:, GDN_DK:]], axis=1))

    vns, outs = [], []
    for p in range(n_pairs):
        wq = jnp.concatenate([ws[p], qs[p] * egs[p]], axis=0).astype(BF16)
        ws_qs = _dot(wq, _pair_blockdiag(sts[p].astype(BF16)))
        vns.append((us[p] - ws_qs[:rows]).astype(BF16))
        outs.append(ws_qs[rows:])
    for p in range(n_pairs):
        o2 = outs[p] + _dot(attns[p], _pair_blockdiag(vns[p]))
        g_end = gcs[p][edge:edge + 1, :]
        k_dec = (ks[p] * jnp.exp(g_end - gcs[p])).astype(BF16)
        full = _dot_tn(k_dec, vns[p])
        upd = jnp.concatenate([full[:GDN_DK, :GDN_DK], full[GDN_DK:, GDN_DK:]], axis=1)
        s_ref[p] = sts[p] * jnp.exp(g_end) + upd
        sl = slice(p * pw, (p + 1) * pw)
        if final:
            o2 = o2 + of_ref[:, sl]
            ys = []
            for j in range(2):
                oj = o2[:, j * GDN_DK:(j + 1) * GDN_DK]
                ys.append(oj * lax.rsqrt(jnp.mean(oj * oj, axis=-1, keepdims=True) + NORM_EPS) * nw_ref[...])
            o_ref[:, sl] = jnp.concatenate(ys, axis=1) * _silu(z_ref[:, sl])
        else:
            o_ref[:, sl] = o2


def _gdn_scan(q, k, v, gcol, grow, o_fwd, ygdn, norm_w, n_lat, n_ctx, reverse):
    bsz, l, hw = q.shape
    final = o_fwd is not None
    chunk_fn = functools.partial(_chunk_of_step, n_lat=n_lat, n_ctx=n_ctx, reverse=reverse)
    tok = pl.BlockSpec((None, SCAN_CHUNK, hw), lambda b, i: (b, chunk_fn(i), 0))
    in_specs = [tok, tok, tok,
                pl.BlockSpec((None, SCAN_CHUNK, LANE), lambda b, i: (b, chunk_fn(i), 0)),
                pl.BlockSpec((None, 4 * GDN_HEADS, SCAN_CHUNK), lambda b, i: (b, 0, chunk_fn(i)))]
    args = [q, k, v, gcol, grow]
    if final:
        in_specs += [tok, pl.BlockSpec((None, SCAN_CHUNK, hw), lambda b, i: (b, chunk_fn(i), 3)),
                     pl.BlockSpec((1, GDN_DK), lambda b, i: (0, 0))]
        args += [o_fwd, ygdn, norm_w]
    return pl.pallas_call(
        functools.partial(_gdn_scan_kernel, reverse=reverse, final=final, n_lat=n_lat, n_ctx=n_ctx),
        grid=(bsz, l // SCAN_CHUNK),
        in_specs=in_specs,
        out_specs=tok,
        out_shape=jax.ShapeDtypeStruct((bsz, l, hw), F32),
        scratch_shapes=[pltpu.VMEM((GDN_HEADS // 2, GDN_DK, 2 * GDN_DK), F32)],
        compiler_params=_cparams("arbitrary", "arbitrary"),
        name="gdn_rev" if reverse else "gdn_fwd",
    )(*args)


def _ssd_scan_kernel(*refs, reverse, final, n_lat, n_ctx):
    if final:
        (x_ref, xp_ref, xn_ref, cw_ref, cb_ref, dtb_ref, a_ref, dsk_ref, yf_ref, nw_ref,
         o_ref, xe_ref, st_ref) = refs
    else:
        (x_ref, xp_ref, xn_ref, cw_ref, cb_ref, dtb_ref, a_ref, dsk_ref, o_ref, xe_ref, st_ref) = refs
    rows = SCAN_CHUNK
    inner = SSD_HEADS * SSD_HEADDIM
    gw = inner // SSD_GROUPS
    xbc_w = inner + 2 * SSD_GROUPS * SSD_STATE
    i = pl.program_id(1)
    c = _chunk_of_step(i, n_lat, n_ctx, reverse)
    first, last = _segment_edges(c, n_lat, n_ctx)

    @pl.when(i == 0)
    def _():
        st_ref[...] = jnp.zeros_like(st_ref)

    x_all = x_ref[...]
    u = _silu(_seg_conv(xe_ref, x_all[:, inner:inner + xbc_w], xp_ref[:, inner:inner + xbc_w],
                        xn_ref[:, inner:inner + xbc_w], first, last, cw_ref[...], rows) + cb_ref[...])
    xs = u[:, :inner]
    dt = _softplus(x_all[:, inner + xbc_w:] + dtb_ref[...])
    adt = a_ref[...] * dt

    ri = _iota((rows, rows), 0)
    ci = _iota((rows, rows), 1)
    incl = (ri <= ci) if reverse else (ri >= ci)
    tri = jnp.where(incl, 1.0, 0.0).astype(BF16)
    eye = jnp.where(ri == ci, 1.0, 0.0).astype(BF16)
    acs = _exact_left(tri, adt)
    acs_t = _exact_transpose(acs, eye)
    d = 1 if reverse else 0
    edge = 0 if reverse else rows - 1
    er = _iota((LANE, inner), 0)
    ec = _iota((LANE, inner), 1)
    expand = jnp.where(er == d * SSD_HEADS + ec // SSD_HEADDIM, 1.0, 0.0).astype(BF16)
    dt_e = _exact_right(dt, expand)
    ea_e = _exact_right(jnp.exp(acs), expand)
    dc_e = _exact_right(jnp.exp(acs[edge:edge + 1, :] - acs), expand)
    xdt = xs * dt_e
    xdt_b = xdt.astype(BF16)
    xw_b = (xdt * dc_e).astype(BF16)
    lane = _iota((rows, LANE), 1)

    y_parts = []
    for g in range(SSD_GROUPS):
        bm = u[:, inner + g * SSD_STATE:inner + (g + 1) * SSD_STATE].astype(BF16)
        cm = u[:, inner + (SSD_GROUPS + g) * SSD_STATE:inner + (SSD_GROUPS + g + 1) * SSD_STATE].astype(BF16)
        cb = _dot_nt(cm, bm)
        st = st_ref[g]
        y_off = _dot(cm, st.astype(BF16)) * ea_e[:, g * gw:(g + 1) * gw]
        heads_per_group = SSD_HEADS // SSD_GROUPS
        diag = []
        for hp in range(heads_per_group // 2):
            h0 = g * heads_per_group + 2 * hp
            mats = []
            for hh in (h0, h0 + 1):
                li = d * SSD_HEADS + hh
                lm = jnp.exp(jnp.where(incl, acs[:, li:li + 1] - acs_t[li:li + 1, :], NEG_BIG))
                mats.append((cb * lm).astype(BF16))
            xp = xdt_b[:, h0 * SSD_HEADDIM:(h0 + 2) * SSD_HEADDIM]
            x_lo = jnp.where(lane < SSD_HEADDIM, xp, jnp.zeros_like(xp))
            x_hi = jnp.where(lane >= SSD_HEADDIM, xp, jnp.zeros_like(xp))
            diag.append(_dot(jnp.concatenate(mats, axis=1), jnp.concatenate([x_lo, x_hi], axis=0)))
        y_parts.append(jnp.concatenate(diag, axis=1) + y_off)
        st_ref[g] = (st * ea_e[edge:edge + 1, g * gw:(g + 1) * gw]
                     + _dot_tn(bm, xw_b[:, g * gw:(g + 1) * gw]))
    y = jnp.concatenate(y_parts, axis=1) + dsk_ref[...] * xs
    if final:
        y = y + yf_ref[...]
        yz = y * _silu(x_all[:, :inner])
        outs = []
        for g in range(SSD_GROUPS):
            yg = yz[:, g * gw:(g + 1) * gw]
            outs.append(yg * lax.rsqrt(jnp.mean(yg * yg, axis=-1, keepdims=True) + NORM_EPS))
        o_ref[...] = jnp.concatenate(outs, axis=1) * nw_ref[...]
    else:
        o_ref[...] = y


def _ssd_scan(yssd, conv_w, conv_b, dtb_vec, a_vec, dskip, y_fwd, norm_w, n_lat, n_ctx, reverse):
    bsz, l, wtot = yssd.shape
    inner = SSD_HEADS * SSD_HEADDIM
    xbc_w = inner + 2 * SSD_GROUPS * SSD_STATE
    final = y_fwd is not None
    chunk_fn = functools.partial(_chunk_of_step, n_lat=n_lat, n_ctx=n_ctx, reverse=reverse)
    main = pl.BlockSpec((None, SCAN_CHUNK, wtot), lambda b, i: (b, chunk_fn(i), 0))
    prev, nxt = _halo_specs(wtot, SCAN_CHUNK, l, chunk_fn)
    tok = pl.BlockSpec((None, SCAN_CHUNK, inner), lambda b, i: (b, chunk_fn(i), 0))
    row = lambda n: pl.BlockSpec((1, n), lambda b, i: (0, 0))
    in_specs = [main, prev, nxt, pl.BlockSpec((CONV_K, xbc_w), lambda b, i: (0, 0)), row(xbc_w),
                row(LANE), row(LANE), row(inner)]
    args = [yssd, yssd, yssd, conv_w, conv_b, dtb_vec, a_vec, dskip]
    if final:
        in_specs += [tok, row(inner)]
        args += [y_fwd, norm_w]
    return pl.pallas_call(
        functools.partial(_ssd_scan_kernel, reverse=reverse, final=final, n_lat=n_lat, n_ctx=n_ctx),
        grid=(bsz, l // SCAN_CHUNK),
        in_specs=in_specs,
        out_specs=tok,
        out_shape=jax.ShapeDtypeStruct((bsz, l, inner), F32),
        scratch_shapes=[pltpu.VMEM((SCAN_CHUNK + 2 * SUBLANE, xbc_w), F32),
                        pltpu.VMEM((SSD_GROUPS, SSD_STATE, inner // SSD_GROUPS), F32)],
        compiler_params=_cparams("arbitrary", "arbitrary"),
        name="ssd_rev" if reverse else "ssd_fwd",
    )(*args)


def _merge_kernel(x_ref, lru_ref, gdn_ref, ssd_ref, gate_ref, mod_ref, nw_ref, wb_ref, wo_ref, wr_ref, br_ref,
                  xo_ref, h_ref, lg_ref):
    d = x_ref.shape[1]
    acc = None
    for k, b_ref in enumerate((lru_ref, gdn_ref, ssd_ref)):
        t = jax.nn.sigmoid(gate_ref[:, k * d:(k + 1) * d]) * _dot(b_ref[...].astype(BF16), wb_ref[k])
        acc = t if acc is None else acc + t
    mix = _dot(acc.astype(BF16), wo_ref[...])
    x = x_ref[...] + mod_ref[2:3, :] * mix
    xo_ref[...] = x
    h = _mod_norm(x, nw_ref[...], mod_ref[3:4, :], mod_ref[4:5, :])
    hb = h.astype(BF16)
    h_ref[...] = hb
    lg_ref[...] = _dot(hb, wr_ref[...]) + br_ref[...]


def _merge(x, lru, gdn, ssd, gate, modtab, norm_w, wb, wo, wr, br, n_lat_tiles):
    bsz, l, d = x.shape
    tok = pl.BlockSpec((None, ROW_TILE, d), lambda b, j: (b, j, 0))
    return pl.pallas_call(
        _merge_kernel,
        grid=(bsz, l // ROW_TILE),
        in_specs=[tok, tok, tok, tok,
                  pl.BlockSpec((None, ROW_TILE, 3 * d), lambda b, j: (b, j, 0)),
                  pl.BlockSpec((None, None, SUBLANE, d), lambda b, j: (b, jnp.where(j >= n_lat_tiles, 1, 0), 0, 0)),
                  pl.BlockSpec((1, d), lambda b, j: (0, 0)),
                  pl.BlockSpec((3, d, d), lambda b, j: (0, 0, 0)),
                  pl.BlockSpec((d, d), lambda b, j: (0, 0)),
                  pl.BlockSpec((d, LANE), lambda b, j: (0, 0)),
                  pl.BlockSpec((1, LANE), lambda b, j: (0, 0))],
        out_specs=[tok, tok, pl.BlockSpec((None, ROW_TILE, LANE), lambda b, j: (b, j, 0))],
        out_shape=[jax.ShapeDtypeStruct((bsz, l, d), F32), jax.ShapeDtypeStruct((bsz, l, d), BF16),
                   jax.ShapeDtypeStruct((bsz, l, LANE), F32)],
        compiler_params=_cparams("arbitrary", "arbitrary"),
        name="merge",
    )(x, lru, gdn, ssd, gate, modtab, norm_w, wb, wo, wr, br)


def _moe_kernel(be_ref, na_ref, x_ref, w1_ref, w3_ref, w2_ref, o_ref):
    i = pl.program_id(0)

    @pl.when(i < na_ref[0])
    def _():
        x = x_ref[...]
        mid = _silu(_dot(x, w1_ref[...])) * _dot(x, w3_ref[...])
        o_ref[...] = _dot(mid.astype(BF16), w2_ref[...])

    @pl.when(i >= na_ref[0])
    def _():
        o_ref[...] = jnp.zeros_like(o_ref)


def _moe_experts(xb, block_expert, n_active, w1, w3, w2):
    n_rows, d = xb.shape
    ff = w1.shape[-1]
    grid_spec = pltpu.PrefetchScalarGridSpec(
        num_scalar_prefetch=2,
        grid=(n_rows // MOE_ROWS,),
        in_specs=[pl.BlockSpec((MOE_ROWS, d), lambda i, be, na: (i, 0)),
                  pl.BlockSpec((None, d, ff), lambda i, be, na: (be[i], 0, 0)),
                  pl.BlockSpec((None, d, ff), lambda i, be, na: (be[i], 0, 0)),
                  pl.BlockSpec((None, ff, d), lambda i, be, na: (be[i], 0, 0))],
        out_specs=pl.BlockSpec((MOE_ROWS, d), lambda i, be, na: (i, 0)),
    )
    return pl.pallas_call(
        _moe_kernel,
        grid_spec=grid_spec,
        out_shape=jax.ShapeDtypeStruct((n_rows, d), F32),
        compiler_params=_cparams("arbitrary"),
        name="moe_experts",
    )(block_expert, n_active, xb, w1, w3, w2)


def _route(logits, bsz_tokens):
    t = logits.shape[0]
    g_logits = logits[:, :N_GROUPS]
    e_logits = logits[:, N_GROUPS:N_GROUPS + N_EXPERTS].reshape(t, N_GROUPS, EXPERTS_PER_GROUP)
    g_prob = jax.nn.softmax(g_logits, axis=-1)
    g_idx = jnp.argmax(g_logits, axis=-1)
    p_group = jnp.take_along_axis(g_prob, g_idx[:, None], axis=1)[:, 0]
    e_in_group = jnp.take_along_axis(e_logits, g_idx[:, None, None], axis=1)[:, 0]
    top_v, top_i = lax.top_k(e_in_group, TOP_K)
    weights = jax.nn.softmax(top_v, axis=-1) * p_group[:, None]
    expert_id = (g_idx[:, None] * EXPERTS_PER_GROUP + top_i).reshape(-1).astype(jnp.int32)
    n_assign = t * TOP_K
    order = jnp.argsort(expert_id)
    e_sorted = expert_id[order]
    counts = jnp.bincount(expert_id, length=N_EXPERTS)
    padded = (counts + MOE_ROWS - 1) // MOE_ROWS * MOE_ROWS
    starts = jnp.cumsum(counts) - counts
    pad_ends = jnp.cumsum(padded)
    pad_starts = pad_ends - padded
    dest = (pad_starts[e_sorted] + jnp.arange(n_assign, dtype=jnp.int32) - starts[e_sorted]).astype(jnp.int32)
    n_blocks = -(-(n_assign + N_EXPERTS * (MOE_ROWS - 1)) // MOE_ROWS)
    n_rows = n_blocks * MOE_ROWS
    tok_buf = jnp.zeros((n_rows,), jnp.int32).at[dest].set((order // TOP_K).astype(jnp.int32))
    slot = jnp.zeros((n_assign,), jnp.int32).at[order].set(dest).reshape(t, TOP_K)
    block_start = jnp.arange(n_blocks, dtype=jnp.int32) * MOE_ROWS
    block_expert = jnp.minimum(jnp.sum(block_start[:, None] >= pad_ends[None, :], axis=1),
                               N_EXPERTS - 1).astype(jnp.int32)
    n_active = (pad_ends[-1] // MOE_ROWS).astype(jnp.int32).reshape(1)
    return tok_buf, block_expert, n_active, slot, weights


def _combine_kernel(x_ref, y_ref, mod_ref, nw_ref, o_ref, *, final):
    x = x_ref[...] + mod_ref[5:6, :] * y_ref[...]
    if final:
        x = x * lax.rsqrt(jnp.mean(x * x, axis=-1, keepdims=True) + NORM_EPS) * nw_ref[...]
    o_ref[...] = x


def _combine(x, y, modtab, norm_w, n_lat_tiles, final):
    bsz, l, d = x.shape
    tok = pl.BlockSpec((None, ROW_TILE, d), lambda b, j: (b, j, 0))
    return pl.pallas_call(
        functools.partial(_combine_kernel, final=final),
        grid=(bsz, l // ROW_TILE),
        in_specs=[tok, tok,
                  pl.BlockSpec((None, None, SUBLANE, d), lambda b, j: (b, jnp.where(j >= n_lat_tiles, 1, 0), 0, 0)),
                  pl.BlockSpec((1, d), lambda b, j: (0, 0))],
        out_specs=tok,
        out_shape=jax.ShapeDtypeStruct((bsz, l, d), F32),
        compiler_params=_cparams("arbitrary", "arbitrary"),
        name="combine",
    )(x, y, modtab, norm_w)


def _pad_lanes(v, n=LANE):
    v = v.reshape(1, -1).astype(F32)
    return jnp.pad(v, ((0, 0), (0, n - v.shape[1])))


def kernel(x, c, ctx, c_ctx, w_mod, b_mod, norm1_w, norm2_w, w_in, lru_conv_w, lru_conv_b, lru_wa, lru_ba,
           lru_wi, lru_bi, lru_lambda, gdn_conv_w, gdn_a_log, gdn_dt_bias, gdn_norm_w, ssd_conv_w, ssd_conv_b,
           ssd_a_log, ssd_dt_bias, ssd_d, ssd_norm_w, w_branch, w_out, router_group_w, router_group_b,
           router_expert_w, router_expert_b, expert_w1, expert_w3, expert_w2, final_norm_w):
    bsz, seq, d = x.shape
    n_ctx_tok = ctx.shape[1]
    depth = w_mod.shape[0]
    l = seq + n_ctx_tok
    rows = seq // GRID_W
    assert seq % LRU_CHUNK == 0 and n_ctx_tok % LRU_CHUNK == 0 and ROW_TILE == LRU_CHUNK
    assert bsz + 1 <= SUBLANE
    hw = GDN_HEADS * GDN_DK
    inner = SSD_HEADS * SSD_HEADDIM
    xbc_w = inner + 2 * SSD_GROUPS * SSD_STATE

    act = jnp.concatenate([_silu(c), _silu(c_ctx)[None, :],
                           jnp.zeros((SUBLANE - bsz - 1, d), F32)], axis=0)
    mod_all = _modulation(act, w_mod, b_mod).reshape(depth, SUBLANE, 6, d)
    pad2 = jnp.zeros((bsz, 2, d), F32)

    xs = jnp.concatenate([x, ctx], axis=1)
    nl_t, nc_t = seq // ROW_TILE, n_ctx_tok // ROW_TILE
    nl_s, nc_s = seq // SCAN_CHUNK, n_ctx_tok // SCAN_CHUNK

    o_lx, o_qkv, o_gz, o_gb, o_ga = 0, 2 * d, 2 * d + 3 * hw, 2 * d + 4 * hw, 2 * d + 4 * hw + 2 * GDN_HEADS
    o_sz = o_ga + 2 * GDN_HEADS
    o_xbc = o_sz + inner
    o_sdt = o_xbc + xbc_w
    o_gate = o_sdt + 2 * SSD_HEADS

    for i in range(depth):
        lat = mod_all[i, :bsz]
        cx = jnp.broadcast_to(mod_all[i, bsz][None], (bsz, 6, d))
        modtab = jnp.stack([jnp.concatenate([lat, pad2], axis=1), jnp.concatenate([cx, pad2], axis=1)], axis=1)

        wi_ = w_in[i]
        w_lru = wi_[:, o_lx:o_qkv].astype(BF16)
        zpad = jnp.zeros((d, LANE - 4 * GDN_HEADS), F32)
        w_gdn = jnp.concatenate([wi_[:, o_qkv:o_gb], wi_[:, o_ga:o_sz], wi_[:, o_gb:o_ga], zpad], axis=1).astype(BF16)
        zpad2 = jnp.zeros((d, LANE - 2 * SSD_HEADS), F32)
        w_ssd = jnp.concatenate([wi_[:, o_sz:o_gate], zpad2], axis=1).astype(BF16)
        w_gate = wi_[:, o_gate:].astype(BF16)
        n1 = norm1_w[i].reshape(1, d)

        ylru = _norm_matmul(xs, modtab, n1, w_lru, 0, 1, nl_t, "in_lru")
        ygdn = _norm_matmul(xs, modtab, n1, w_gdn, 0, 1, nl_t, "in_gdn")
        gate = _norm_matmul(xs, modtab, n1, w_gate, 0, 1, nl_t, "in_gate")
        x_cm = xs[:, :seq].reshape(bsz, rows, GRID_W, d).swapaxes(1, 2).reshape(bsz, seq, d)
        xs_scan = jnp.concatenate([x_cm, xs[:, seq:]], axis=1)
        yssd = _norm_matmul(xs_scan, modtab, n1, w_ssd, 0, 1, nl_t, "in_ssd")

        lru_args = (lru_conv_w[i], lru_conv_b[i].reshape(1, d))
        h_f = _lru_scan(ylru, *lru_args, lru_wa[i, 0].astype(BF16), lru_ba[i, 0].reshape(1, d),
                        lru_wi[i, 0].astype(BF16), lru_bi[i, 0].reshape(1, d), lru_lambda[i, 0].reshape(1, d),
                        None, nl_t, nc_t, False)
        lru_out = _lru_scan(ylru, *lru_args, lru_wa[i, 1].astype(BF16), lru_ba[i, 1].reshape(1, d),
                            lru_wi[i, 1].astype(BF16), lru_bi[i, 1].reshape(1, d), lru_lambda[i, 1].reshape(1, d),
                            h_f, nl_t, nc_t, True)

        q, k, v, gcol, grow = _gdn_prep(ygdn, gdn_conv_w[i], _pad_lanes(gdn_a_log[i]), _pad_lanes(gdn_dt_bias[i]),
                                        nl_t, nc_t)
        o_f = _gdn_scan(q, k, v, gcol, grow, None, None, None, nl_s, nc_s, False)
        gdn_out = _gdn_scan(q, k, v, gcol, grow, o_f, ygdn, gdn_norm_w[i].reshape(1, GDN_DK), nl_s, nc_s, True)

        dtb_vec = _pad_lanes(ssd_dt_bias[i])
        a_vec = _pad_lanes(-jnp.exp(ssd_a_log[i].astype(F32)))
        ssd_cw, ssd_cb = ssd_conv_w[i], ssd_conv_b[i].reshape(1, xbc_w)
        dsk = [jnp.repeat(ssd_d[i, dd], SSD_HEADDIM).reshape(1, inner) for dd in range(2)]
        y_f = _ssd_scan(yssd, ssd_cw, ssd_cb, dtb_vec, a_vec, dsk[0], None, None, nl_s, nc_s, False)
        ssd_scan_out = _ssd_scan(yssd, ssd_cw, ssd_cb, dtb_vec, a_vec, dsk[1], y_f,
                                 ssd_norm_w[i].reshape(1, inner), nl_s, nc_s, True)
        ssd_lat = ssd_scan_out[:, :seq].reshape(bsz, GRID_W, rows, inner).swapaxes(1, 2).reshape(bsz, seq, inner)
        ssd_out = jnp.concatenate([ssd_lat, ssd_scan_out[:, seq:]], axis=1)

        w_r = jnp.concatenate([router_group_w[i], router_expert_w[i],
                               jnp.zeros((d, LANE - N_GROUPS - N_EXPERTS), F32)], axis=1).astype(BF16)
        b_r = _pad_lanes(jnp.concatenate([router_group_b[i], router_expert_b[i]]))
        x_mid, h2, logits = _merge(xs, lru_out, gdn_out, ssd_out, gate, modtab, norm2_w[i].reshape(1, d),
                                   w_branch[i].astype(BF16), w_out[i].astype(BF16), w_r, b_r, nl_t)

        t = bsz * l
        tok_buf, block_expert, n_active, slot, weights = _route(logits.reshape(t, LANE), t)
        xb = h2.reshape(t, d)[tok_buf]
        yb = _moe_experts(xb, block_expert, n_active, expert_w1[i].astype(BF16), expert_w3[i].astype(BF16),
                          expert_w2[i].astype(BF16))
        y = (yb[slot[:, 0]] * weights[:, 0:1] + yb[slot[:, 1]] * weights[:, 1:2]).reshape(bsz, l, d)
        last = i == depth - 1
        xs = _combine(x_mid, y, modtab, final_norm_w.reshape(1, d), nl_t, last)

    return xs[:, :seq]
```

```python
import functools

import jax
import jax.numpy as jnp
from jax import lax
from jax.experimental import pallas as pl
from jax.experimental.pallas import tpu as pltpu

GRID_W = 64
CONV_K = 4
NORM_EPS = 1e-6
LRU_BLOCKS = 8
LRU_C = 8.0
GDN_HEADS = 8
GDN_DK = 128
SSD_HEADS = 16
SSD_HEADDIM = 64
SSD_GROUPS = 2
SSD_STATE = 128
N_GROUPS = 4
EXPERTS_PER_GROUP = 8
N_EXPERTS = N_GROUPS * EXPERTS_PER_GROUP
TOP_K = 2

LANE = 128
SUBLANE = 8
ROW_TILE = 256
LRU_CHUNK = 256
SCAN_CHUNK = 128
MOE_ROWS = 256
VMEM_LIMIT = 56 * 1024 * 1024

F32 = jnp.float32
BF16 = jnp.bfloat16
NEG_BIG = -1e30


def _cparams(*sem):
    return pltpu.CompilerParams(dimension_semantics=sem, vmem_limit_bytes=VMEM_LIMIT)


def _dot(a, b):
    return jnp.dot(a, b, preferred_element_type=F32)


def _dot_nt(a, b):
    return lax.dot_general(a, b, (((1,), (1,)), ((), ())), preferred_element_type=F32)


def _dot_tn(a, b):
    return lax.dot_general(a, b, (((0,), (0,)), ((), ())), preferred_element_type=F32)


def _split3(x):
    hi = x.astype(BF16)
    r1 = x - hi.astype(F32)
    mid = r1.astype(BF16)
    lo = (r1 - mid.astype(F32)).astype(BF16)
    return hi, mid, lo


def _exact_left(m_bf, x):
    hi, mid, lo = _split3(x)
    return _dot(jnp.concatenate([m_bf, m_bf, m_bf], axis=1), jnp.concatenate([hi, mid, lo], axis=0))


def _exact_right(x, m_bf):
    hi, mid, lo = _split3(x)
    return _dot(jnp.concatenate([hi, mid, lo], axis=1), jnp.concatenate([m_bf, m_bf, m_bf], axis=0))


def _exact_transpose(x, eye_bf):
    hi, mid, lo = _split3(x)
    return _dot_nt(jnp.concatenate([eye_bf, eye_bf, eye_bf], axis=1), jnp.concatenate([hi, mid, lo], axis=1))


def _softplus(x):
    return jnp.maximum(x, 0.0) + jnp.log(1.0 + jnp.exp(-jnp.abs(x)))


def _silu(x):
    return x * jax.nn.sigmoid(x)


def _iota(shape, dim):
    return lax.broadcasted_iota(jnp.int32, shape, dim)


def _fwd_chunk(i, n_lat, n_ctx):
    return jnp.where(i < n_ctx, n_lat + i, i - n_ctx)


def _chunk_of_step(i, n_lat, n_ctx, reverse):
    return (n_lat + n_ctx - 1 - i) if reverse else _fwd_chunk(i, n_lat, n_ctx)


def _segment_edges(c, n_lat, n_ctx):
    first = jnp.logical_or(c == 0, c == n_lat)
    last = jnp.logical_or(c == n_lat - 1, c == n_lat + n_ctx - 1)
    return first, last


def _seg_conv(xe_ref, x, prev8, next8, first, last, cw, rows):
    xe_ref[0:SUBLANE, :] = jnp.where(first, 0.0, prev8)
    xe_ref[SUBLANE:SUBLANE + rows, :] = x
    xe_ref[SUBLANE + rows:2 * SUBLANE + rows, :] = jnp.where(last, 0.0, next8)
    base = SUBLANE - CONV_K // 2
    u = cw[0:1, :] * xe_ref[base:base + rows, :]
    for j in range(1, CONV_K):
        u = u + cw[j:j + 1, :] * xe_ref[base + j:base + j + rows, :]
    return u


def _halo_specs(width, rows, n_rows_total, chunk_fn, col_block=0):
    per = rows // SUBLANE
    last_tile = n_rows_total // SUBLANE - 1
    prev = pl.BlockSpec((None, SUBLANE, width),
                        lambda b, i: (b, jnp.maximum(chunk_fn(i) * per - 1, 0), col_block))
    nxt = pl.BlockSpec((None, SUBLANE, width),
                       lambda b, i: (b, jnp.minimum((chunk_fn(i) + 1) * per, last_tile), col_block))
    return prev, nxt


def _mod_kernel(a_ref, w_ref, b_ref, o_ref):
    o_ref[...] = _dot(a_ref[...].astype(BF16), w_ref[...].astype(BF16)) + b_ref[...]


def _modulation(act, w_mod, b_mod):
    depth, d, n = w_mod.shape
    tn = 1536
    return pl.pallas_call(
        _mod_kernel,
        grid=(depth, n // tn),
        in_specs=[pl.BlockSpec((SUBLANE, d), lambda l, j: (0, 0)),
                  pl.BlockSpec((None, d, tn), lambda l, j: (l, 0, j)),
                  pl.BlockSpec((None, 1, tn), lambda l, j: (l, 0, j))],
        out_specs=pl.BlockSpec((None, SUBLANE, tn), lambda l, j: (l, 0, j)),
        out_shape=jax.ShapeDtypeStruct((depth, SUBLANE, n), F32),
        compiler_params=_cparams("arbitrary", "arbitrary"),
        name="modulation",
    )(act, w_mod, b_mod.reshape(depth, 1, n))


def _mod_norm(x, nw, shift, scale):
    y = x * lax.rsqrt(jnp.mean(x * x, axis=-1, keepdims=True) + NORM_EPS)
    return (y * nw) * (1.0 + scale) + shift


def _norm_matmul_kernel(x_ref, mod_ref, nw_ref, w_ref, o_ref, *, shift_row, scale_row, col_step):
    h = _mod_norm(x_ref[...], nw_ref[...], mod_ref[shift_row:shift_row + 1, :],
                  mod_ref[scale_row:scale_row + 1, :]).astype(BF16)
    n = w_ref.shape[1]
    for c0 in range(0, n, col_step):
        c1 = min(c0 + col_step, n)
        o_ref[:, c0:c1] = _dot(h, w_ref[:, c0:c1]).astype(o_ref.dtype)


def _norm_matmul(x, modtab, norm_w, w, shift_row, scale_row, n_lat_tiles, name):
    bsz, l, d = x.shape
    n = w.shape[1]
    kern = functools.partial(_norm_matmul_kernel, shift_row=shift_row, scale_row=scale_row, col_step=512)
    return pl.pallas_call(
        kern,
        grid=(bsz, l // ROW_TILE),
        in_specs=[pl.BlockSpec((None, ROW_TILE, d), lambda b, j: (b, j, 0)),
                  pl.BlockSpec((None, None, SUBLANE, d), lambda b, j: (b, jnp.where(j >= n_lat_tiles, 1, 0), 0, 0)),
                  pl.BlockSpec((1, d), lambda b, j: (0, 0)),
                  pl.BlockSpec((d, n), lambda b, j: (0, 0))],
        out_specs=pl.BlockSpec((None, ROW_TILE, n), lambda b, j: (b, j, 0)),
        out_shape=jax.ShapeDtypeStruct((bsz, l, n), F32),
        compiler_params=_cparams("arbitrary", "arbitrary"),
        name=name,
    )(x, modtab, norm_w, w)


def _lru_kernel(*refs, reverse, final, n_lat, n_ctx):
    if final:
        (x_ref, xp_ref, xn_ref, cw_ref, cb_ref, wa_ref, ba_ref, wi_ref, bi_ref, lam_ref, hf_ref, y_ref,
         o_ref, xe_ref, carry_ref) = refs
    else:
        (x_ref, xp_ref, xn_ref, cw_ref, cb_ref, wa_ref, ba_ref, wi_ref, bi_ref, lam_ref,
         o_ref, xe_ref, carry_ref) = refs
    rows = LRU_CHUNK
    i = pl.program_id(1)
    c = _chunk_of_step(i, n_lat, n_ctx, reverse)
    first, last = _segment_edges(c, n_lat, n_ctx)

    @pl.when(i == 0)
    def _():
        carry_ref[...] = jnp.zeros_like(carry_ref)

    u = _seg_conv(xe_ref, x_ref[...], xp_ref[...], xn_ref[...], first, last, cw_ref[...], rows) + cb_ref[...]
    ub = u.astype(BF16)
    bw = u.shape[1] // LRU_BLOCKS
    pre_r = jnp.concatenate([_dot(ub[:, n * bw:(n + 1) * bw], wa_ref[n]) for n in range(LRU_BLOCKS)], axis=1)
    pre_i = jnp.concatenate([_dot(ub[:, n * bw:(n + 1) * bw], wi_ref[n]) for n in range(LRU_BLOCKS)], axis=1)
    r = jax.nn.sigmoid(pre_r + ba_ref[...])
    gi = jax.nn.sigmoid(pre_i + bi_ref[...])
    a = jnp.exp(-LRU_C * r * _softplus(-lam_ref[...]))
    h = jnp.sqrt(1.0 - a * a) * gi * u

    srow = _iota((SUBLANE, a.shape[1]), 0)
    n_tiles = rows // SUBLANE
    carry = carry_ref[...]
    for k in (range(n_tiles - 1, -1, -1) if reverse else range(n_tiles)):
        sl = slice(k * SUBLANE, (k + 1) * SUBLANE)
        at, ht = a[sl], h[sl]
        for s in (1, 2, 4):
            if reverse:
                keep = srow < SUBLANE - s
                sh = SUBLANE - s
            else:
                keep = srow >= s
                sh = s
            h_s = jnp.where(keep, pltpu.roll(ht, sh, axis=0), 0.0)
            a_s = jnp.where(keep, pltpu.roll(at, sh, axis=0), 1.0)
            ht = ht + at * h_s
            at = at * a_s
        hk = ht + at * carry
        carry = hk[0:1, :] if reverse else hk[SUBLANE - 1:SUBLANE, :]
        if final:
            o_ref[sl, :] = (hk + hf_ref[sl, :]) * jax.nn.gelu(y_ref[sl, :], approximate=True)
        else:
            o_ref[sl, :] = hk
    carry_ref[...] = carry


def _lru_scan(ylru, conv_w, conv_b, wa, ba, wi, bi, lam, hf, n_lat, n_ctx, reverse):
    bsz, l, w2 = ylru.shape
    w = w2 // 2
    final = hf is not None
    chunk_fn = functools.partial(_chunk_of_step, n_lat=n_lat, n_ctx=n_ctx, reverse=reverse)
    main = pl.BlockSpec((None, LRU_CHUNK, w), lambda b, i: (b, chunk_fn(i), 0))
    prev, nxt = _halo_specs(w, LRU_CHUNK, l, chunk_fn)
    vec = pl.BlockSpec((1, w), lambda b, i: (0, 0))
    blk = pl.BlockSpec((LRU_BLOCKS, w // LRU_BLOCKS, w // LRU_BLOCKS), lambda b, i: (0, 0, 0))
    in_specs = [main, prev, nxt, pl.BlockSpec((CONV_K, w), lambda b, i: (0, 0)), vec, blk, vec, blk, vec, vec]
    args = [ylru, ylru, ylru, conv_w, conv_b, wa, ba, wi, bi, lam]
    if final:
        in_specs += [main, pl.BlockSpec((None, LRU_CHUNK, w), lambda b, i: (b, chunk_fn(i), 1))]
        args += [hf, ylru]
    return pl.pallas_call(
        functools.partial(_lru_kernel, reverse=reverse, final=final, n_lat=n_lat, n_ctx=n_ctx),
        grid=(bsz, l // LRU_CHUNK),
        in_specs=in_specs,
        out_specs=main,
        out_shape=jax.ShapeDtypeStruct((bsz, l, w), F32),
        scratch_shapes=[pltpu.VMEM((LRU_CHUNK + 2 * SUBLANE, w), F32),
                        pltpu.VMEM((1, w), F32)],
        compiler_params=_cparams("arbitrary", "arbitrary"),
        name="lru_rev" if reverse else "lru_fwd",
    )(*args)


def _gdn_prep_kernel(x_ref, xp_ref, xn_ref, sm_ref, cw_ref, alog_ref, dtb_ref,
                     q_ref, k_ref, v_ref, gcol_ref, grow_ref, xe_ref, *, n_lat, n_ctx):
    rows = ROW_TILE
    c = pl.program_id(1)
    first, last = _segment_edges(c, n_lat, n_ctx)
    u = _silu(_seg_conv(xe_ref, x_ref[...], xp_ref[...], xn_ref[...], first, last, cw_ref[...], rows))
    hw = GDN_HEADS * GDN_DK
    for h in range(GDN_HEADS):
        sl = slice(h * GDN_DK, (h + 1) * GDN_DK)
        qh = u[:, h * GDN_DK:(h + 1) * GDN_DK]
        kh = u[:, hw + h * GDN_DK:hw + (h + 1) * GDN_DK]
        q_ref[:, sl] = qh * lax.rsqrt(jnp.sum(qh * qh, axis=-1, keepdims=True) + NORM_EPS) * (GDN_DK ** -0.5)
        k_ref[:, sl] = kh * lax.rsqrt(jnp.sum(kh * kh, axis=-1, keepdims=True) + NORM_EPS)
    v_ref[...] = u[:, 2 * hw:]

    sm = sm_ref[...]
    g = -jnp.exp(alog_ref[...]) * _softplus(sm + dtb_ref[...])
    beta = jax.nn.sigmoid(sm)
    ri = _iota((rows, rows), 0)
    ci = _iota((rows, rows), 1)
    same = (ri // SCAN_CHUNK) == (ci // SCAN_CHUNK)
    tri_f = jnp.where(jnp.logical_and(same, ci <= ri), 1.0, 0.0).astype(BF16)
    tri_r = jnp.where(jnp.logical_and(same, ci >= ri), 1.0, 0.0).astype(BF16)
    gcs_f = _exact_left(tri_f, g)
    gcs_r = _exact_left(tri_r, g)
    lane = _iota((rows, LANE), 1)
    gcol = jnp.where(lane < GDN_HEADS, gcs_f,
                     jnp.where(lane < 2 * GDN_HEADS, gcs_r, jnp.where(lane < 4 * GDN_HEADS, beta, 0.0)))
    gcol_ref[...] = gcol
    eye = jnp.where(_iota((LANE, LANE), 0) == _iota((LANE, LANE), 1), 1.0, 0.0).astype(BF16)
    grow_ref[...] = _exact_transpose(gcol, eye)[0:4 * GDN_HEADS, :]


def _gdn_prep(ygdn, conv_w, alog_vec, dtb_vec, n_lat, n_ctx):
    bsz, l, _ = ygdn.shape
    hw = GDN_HEADS * GDN_DK
    ident = lambda i: i
    main = pl.BlockSpec((None, ROW_TILE, 3 * hw), lambda b, i: (b, i, 0))
    prev, nxt = _halo_specs(3 * hw, ROW_TILE, l, ident)
    tok = pl.BlockSpec((None, ROW_TILE, hw), lambda b, i: (b, i, 0))
    return pl.pallas_call(
        functools.partial(_gdn_prep_kernel, n_lat=n_lat, n_ctx=n_ctx),
        grid=(bsz, l // ROW_TILE),
        in_specs=[main, prev, nxt,
                  pl.BlockSpec((None, ROW_TILE, LANE), lambda b, i: (b, i, 4 * hw // LANE)),
                  pl.BlockSpec((CONV_K, 3 * hw), lambda b, i: (0, 0)),
                  pl.BlockSpec((1, LANE), lambda b, i: (0, 0)),
                  pl.BlockSpec((1, LANE), lambda b, i: (0, 0))],
        out_specs=[tok, tok, tok,
                   pl.BlockSpec((None, ROW_TILE, LANE), lambda b, i: (b, i, 0)),
                   pl.BlockSpec((None, 4 * GDN_HEADS, ROW_TILE), lambda b, i: (b, 0, i))],
        out_shape=[jax.ShapeDtypeStruct((bsz, l, hw), F32)] * 3
        + [jax.ShapeDtypeStruct((bsz, l, LANE), F32), jax.ShapeDtypeStruct((bsz, 4 * GDN_HEADS, l), F32)],
        scratch_shapes=[pltpu.VMEM((ROW_TILE + 2 * SUBLANE, 3 * hw), F32)],
        compiler_params=_cparams("arbitrary", "arbitrary"),
        name="gdn_prep",
    )(ygdn, ygdn, ygdn, ygdn, conv_w, alog_vec, dtb_vec)


def _pair_blockdiag(x):
    c = x.shape[0]
    z = jnp.zeros((c, c), x.dtype)
    return jnp.concatenate([jnp.concatenate([x[:, :c], z], axis=1),
                            jnp.concatenate([z, x[:, c:]], axis=1)], axis=0)


def _gdn_scan_kernel(*refs, reverse, final, n_lat, n_ctx):
    if final:
        q_ref, k_ref, v_ref, gcol_ref, grow_ref, of_ref, z_ref, nw_ref, o_ref, s_ref = refs
    else:
        q_ref, k_ref, v_ref, gcol_ref, grow_ref, o_ref, s_ref = refs
    rows = SCAN_CHUNK
    n_pairs = GDN_HEADS // 2
    pw = 2 * GDN_DK
    chains = [(bb, p) for bb in range(q_ref.shape[0]) for p in range(n_pairs)]
    n_chains = len(chains)
    i = pl.program_id(1)

    @pl.when(i == 0)
    def _():
        s_ref[...] = jnp.zeros_like(s_ref)

    ri = _iota((rows, rows), 0)
    ci = _iota((rows, rows), 1)
    incl = (ri <= ci) if reverse else (ri >= ci)
    ri2 = _iota((rows, pw), 0)
    ci2 = jnp.bitwise_and(_iota((rows, pw), 1), rows - 1)
    strict2 = (ri2 < ci2) if reverse else (ri2 > ci2)
    d = 1 if reverse else 0
    edge = 0 if reverse else rows - 1

    def pair_cols(gcol, lane0):
        return jnp.concatenate([jnp.broadcast_to(gcol[:, lane0 + j:lane0 + j + 1], (rows, GDN_DK))
                                for j in range(2)], axis=1)

    qs, ks, kbs, egs, gcs, decs, rhs, sts = [], [], [], [], [], [], [], []
    for c, (bb, p) in enumerate(chains):
        sl = slice(p * pw, (p + 1) * pw)
        li = d * GDN_HEADS + 2 * p
        gcol = gcol_ref[bb]
        grow = grow_ref[bb]
        q2, k2, v2 = q_ref[bb, :, sl], k_ref[bb, :, sl], v_ref[bb, :, sl]
        gc2 = pair_cols(gcol, li)
        beta2 = pair_cols(gcol, 2 * GDN_HEADS + li)
        eg2 = jnp.exp(gc2)
        kb2 = k2 * beta2
        dec2 = jnp.concatenate(
            [jnp.exp(jnp.where(incl, gcol[:, li + j:li + j + 1] - grow[li + j:li + j + 1, :], NEG_BIG))
             for j in range(2)], axis=1)
        qs.append(q2)
        ks.append(k2)
        kbs.append(kb2)
        egs.append(eg2)
        gcs.append(gc2)
        decs.append(dec2)
        rhs.append([jnp.concatenate([v2[:, j * GDN_DK:(j + 1) * GDN_DK] * beta2[:, j * GDN_DK:(j + 1) * GDN_DK],
                                     kb2[:, j * GDN_DK:(j + 1) * GDN_DK] * eg2[:, j * GDN_DK:(j + 1) * GDN_DK]],
                                    axis=1).astype(BF16) for j in range(2)])
        sts.append(s_ref[c])

    a_mats, attns = [], []
    for p in range(n_chains):
        kq = _dot_nt(jnp.concatenate([kbs[p], qs[p]], axis=0).astype(BF16), _pair_blockdiag(ks[p].astype(BF16)))
        a_mats.append(jnp.where(strict2, kq[:rows] * decs[p], 0.0))
        attns.append((kq[rows:] * decs[p]).astype(BF16))

    eye2 = jnp.where(ri2 == ci2, 1.0, 0.0)
    ts = [eye2 - jnp.where((ri2 // 2) == (ci2 // 2), a_mats[p], 0.0) for p in range(n_chains)]
    s = 2
    while s < rows:
        off = jnp.logical_and((ri2 // (2 * s)) == (ci2 // (2 * s)), (ri2 // s) != (ci2 // s))
        xs = [_dot(jnp.where(off, a_mats[p], 0.0).astype(BF16), _pair_blockdiag(ts[p].astype(BF16)))
              for p in range(n_chains)]
        ts = [ts[p] - _dot(ts[p].astype(BF16), _pair_blockdiag(xs[p].astype(BF16))) for p in range(n_chains)]
        s *= 2

    us, ws = [], []
    for p in range(n_chains):
        tb = ts[p].astype(BF16)
        sol = [_dot(tb[:, j * GDN_DK:(j + 1) * GDN_DK], rhs[p][j]) for j in range(2)]
        us.append(jnp.concatenate([sol[0][:, :GDN_DK], sol[1][:, :GDN_DK]], axis=1))
        ws.append(jnp.concatenate([sol[0][:, GDN_DK:], sol[1][:, GDN_DK:]], axis=1))

    vns, outs = [], []
    for p in range(n_chains):
        wq = jnp.concatenate([ws[p], qs[p] * egs[p]], axis=0).astype(BF16)
        ws_qs = _dot(wq, _pair_blockdiag(sts[p].astype(BF16)))
        vns.append((us[p] - ws_qs[:rows]).astype(BF16))
        outs.append(ws_qs[rows:])
    for c, (bb, p) in enumerate(chains):
        o2 = outs[c] + _dot(attns[c], _pair_blockdiag(vns[c]))
        g_end = gcs[c][edge:edge + 1, :]
        k_dec = (ks[c] * jnp.exp(g_end - gcs[c])).astype(BF16)
        full = _dot_tn(k_dec, vns[c])
        upd = jnp.concatenate([full[:GDN_DK, :GDN_DK], full[GDN_DK:, GDN_DK:]], axis=1)
        s_ref[c] = sts[c] * jnp.exp(g_end) + upd
        sl = slice(p * pw, (p + 1) * pw)
        if final:
            o2 = o2 + of_ref[bb, :, sl]
            ys = []
            for j in range(2):
                oj = o2[:, j * GDN_DK:(j + 1) * GDN_DK]
                ys.append(oj * lax.rsqrt(jnp.mean(oj * oj, axis=-1, keepdims=True) + NORM_EPS) * nw_ref[...])
            o_ref[bb, :, sl] = jnp.concatenate(ys, axis=1) * _silu(z_ref[bb, :, sl])
        else:
            o_ref[bb, :, sl] = o2


def _gdn_scan(q, k, v, gcol, grow, o_fwd, ygdn, norm_w, n_lat, n_ctx, reverse):
    bsz, l, hw = q.shape
    final = o_fwd is not None
    chunk_fn = functools.partial(_chunk_of_step, n_lat=n_lat, n_ctx=n_ctx, reverse=reverse)
    nb = 2 if bsz % 2 == 0 else 1
    tok = pl.BlockSpec((nb, SCAN_CHUNK, hw), lambda b, i: (b, chunk_fn(i), 0))
    in_specs = [tok, tok, tok,
                pl.BlockSpec((nb, SCAN_CHUNK, LANE), lambda b, i: (b, chunk_fn(i), 0)),
                pl.BlockSpec((nb, 4 * GDN_HEADS, SCAN_CHUNK), lambda b, i: (b, 0, chunk_fn(i)))]
    args = [q, k, v, gcol, grow]
    if final:
        in_specs += [tok, pl.BlockSpec((nb, SCAN_CHUNK, hw), lambda b, i: (b, chunk_fn(i), 3)),
                     pl.BlockSpec((1, GDN_DK), lambda b, i: (0, 0))]
        args += [o_fwd, ygdn, norm_w]
    return pl.pallas_call(
        functools.partial(_gdn_scan_kernel, reverse=reverse, final=final, n_lat=n_lat, n_ctx=n_ctx),
        grid=(bsz // nb, l // SCAN_CHUNK),
        in_specs=in_specs,
        out_specs=tok,
        out_shape=jax.ShapeDtypeStruct((bsz, l, hw), F32),
        scratch_shapes=[pltpu.VMEM((nb * GDN_HEADS // 2, GDN_DK, 2 * GDN_DK), F32)],
        compiler_params=_cparams("arbitrary", "arbitrary"),
        name="gdn_rev" if reverse else "gdn_fwd",
    )(*args)


def _ssd_scan_kernel(*refs, reverse, final, n_lat, n_ctx):
    if final:
        (x_ref, xp_ref, xn_ref, cw_ref, cb_ref, dtb_ref, a_ref, dsk_ref, yf_ref, nw_ref,
         o_ref, xe_ref, st_ref) = refs
    else:
        (x_ref, xp_ref, xn_ref, cw_ref, cb_ref, dtb_ref, a_ref, dsk_ref, o_ref, xe_ref, st_ref) = refs
    rows = SCAN_CHUNK
    inner = SSD_HEADS * SSD_HEADDIM
    gw = inner // SSD_GROUPS
    xbc_w = inner + 2 * SSD_GROUPS * SSD_STATE
    i = pl.program_id(1)
    c = _chunk_of_step(i, n_lat, n_ctx, reverse)
    first, last = _segment_edges(c, n_lat, n_ctx)

    @pl.when(i == 0)
    def _():
        st_ref[...] = jnp.zeros_like(st_ref)

    x_all = x_ref[...]
    u = _silu(_seg_conv(xe_ref, x_all[:, inner:inner + xbc_w], xp_ref[:, inner:inner + xbc_w],
                        xn_ref[:, inner:inner + xbc_w], first, last, cw_ref[...], rows) + cb_ref[...])
    xs = u[:, :inner]
    dt = _softplus(x_all[:, inner + xbc_w:] + dtb_ref[...])
    adt = a_ref[...] * dt

    ri = _iota((rows, rows), 0)
    ci = _iota((rows, rows), 1)
    incl = (ri <= ci) if reverse else (ri >= ci)
    tri = jnp.where(incl, 1.0, 0.0).astype(BF16)
    eye = jnp.where(ri == ci, 1.0, 0.0).astype(BF16)
    acs = _exact_left(tri, adt)
    acs_t = _exact_transpose(acs, eye)
    d = 1 if reverse else 0
    edge = 0 if reverse else rows - 1
    er = _iota((LANE, inner), 0)
    ec = _iota((LANE, inner), 1)
    expand = jnp.where(er == d * SSD_HEADS + ec // SSD_HEADDIM, 1.0, 0.0).astype(BF16)
    dt_e = _exact_right(dt, expand)
    ea_e = _exact_right(jnp.exp(acs), expand)
    dc_e = _exact_right(jnp.exp(acs[edge:edge + 1, :] - acs), expand)
    xdt = xs * dt_e
    xdt_b = xdt.astype(BF16)
    xw_b = (xdt * dc_e).astype(BF16)
    lane = _iota((rows, LANE), 1)

    y_parts = []
    for g in range(SSD_GROUPS):
        bm = u[:, inner + g * SSD_STATE:inner + (g + 1) * SSD_STATE].astype(BF16)
        cm = u[:, inner + (SSD_GROUPS + g) * SSD_STATE:inner + (SSD_GROUPS + g + 1) * SSD_STATE].astype(BF16)
        cb = _dot_nt(cm, bm)
        st = st_ref[g]
        y_off = _dot(cm, st.astype(BF16)) * ea_e[:, g * gw:(g + 1) * gw]
        heads_per_group = SSD_HEADS // SSD_GROUPS
        diag = []
        for hp in range(heads_per_group // 2):
            h0 = g * heads_per_group + 2 * hp
            mats = []
            for hh in (h0, h0 + 1):
                li = d * SSD_HEADS + hh
                lm = jnp.exp(jnp.where(incl, acs[:, li:li + 1] - acs_t[li:li + 1, :], NEG_BIG))
                mats.append((cb * lm).astype(BF16))
            xp = xdt_b[:, h0 * SSD_HEADDIM:(h0 + 2) * SSD_HEADDIM]
            x_lo = jnp.where(lane < SSD_HEADDIM, xp, jnp.zeros_like(xp))
            x_hi = jnp.where(lane >= SSD_HEADDIM, xp, jnp.zeros_like(xp))
            diag.append(_dot(jnp.concatenate(mats, axis=1), jnp.concatenate([x_lo, x_hi], axis=0)))
        y_parts.append(jnp.concatenate(diag, axis=1) + y_off)
        st_ref[g] = (st * ea_e[edge:edge + 1, g * gw:(g + 1) * gw]
                     + _dot_tn(bm, xw_b[:, g * gw:(g + 1) * gw]))
    y = jnp.concatenate(y_parts, axis=1) + dsk_ref[...] * xs
    if final:
        y = y + yf_ref[...]
        yz = y * _silu(x_all[:, :inner])
        outs = []
        for g in range(SSD_GROUPS):
            yg = yz[:, g * gw:(g + 1) * gw]
            outs.append(yg * lax.rsqrt(jnp.mean(yg * yg, axis=-1, keepdims=True) + NORM_EPS))
        o_ref[...] = jnp.concatenate(outs, axis=1) * nw_ref[...]
    else:
        o_ref[...] = y


def _ssd_scan(yssd, conv_w, conv_b, dtb_vec, a_vec, dskip, y_fwd, norm_w, n_lat, n_ctx, reverse):
    bsz, l, wtot = yssd.shape
    inner = SSD_HEADS * SSD_HEADDIM
    xbc_w = inner + 2 * SSD_GROUPS * SSD_STATE
    final = y_fwd is not None
    chunk_fn = functools.partial(_chunk_of_step, n_lat=n_lat, n_ctx=n_ctx, reverse=reverse)
    main = pl.BlockSpec((None, SCAN_CHUNK, wtot), lambda b, i: (b, chunk_fn(i), 0))
    prev, nxt = _halo_specs(wtot, SCAN_CHUNK, l, chunk_fn)
    tok = pl.BlockSpec((None, SCAN_CHUNK, inner), lambda b, i: (b, chunk_fn(i), 0))
    row = lambda n: pl.BlockSpec((1, n), lambda b, i: (0, 0))
    in_specs = [main, prev, nxt, pl.BlockSpec((CONV_K, xbc_w), lambda b, i: (0, 0)), row(xbc_w),
                row(LANE), row(LANE), row(inner)]
    args = [yssd, yssd, yssd, conv_w, conv_b, dtb_vec, a_vec, dskip]
    if final:
        in_specs += [tok, row(inner)]
        args += [y_fwd, norm_w]
    return pl.pallas_call(
        functools.partial(_ssd_scan_kernel, reverse=reverse, final=final, n_lat=n_lat, n_ctx=n_ctx),
        grid=(bsz, l // SCAN_CHUNK),
        in_specs=in_specs,
        out_specs=tok,
        out_shape=jax.ShapeDtypeStruct((bsz, l, inner), F32),
        scratch_shapes=[pltpu.VMEM((SCAN_CHUNK + 2 * SUBLANE, xbc_w), F32),
                        pltpu.VMEM((SSD_GROUPS, SSD_STATE, inner // SSD_GROUPS), F32)],
        compiler_params=_cparams("arbitrary", "arbitrary"),
        name="ssd_rev" if reverse else "ssd_fwd",
    )(*args)


def _merge_kernel(x_ref, lru_ref, gdn_ref, ssd_ref, gate_ref, mod_ref, nw_ref, wb_ref, wo_ref, wr_ref, br_ref,
                  xo_ref, h_ref, lg_ref):
    d = x_ref.shape[1]
    acc = None
    for k, b_ref in enumerate((lru_ref, gdn_ref, ssd_ref)):
        t = jax.nn.sigmoid(gate_ref[:, k * d:(k + 1) * d]) * _dot(b_ref[...].astype(BF16), wb_ref[k])
        acc = t if acc is None else acc + t
    mix = _dot(acc.astype(BF16), wo_ref[...])
    x = x_ref[...] + mod_ref[2:3, :] * mix
    xo_ref[...] = x
    h = _mod_norm(x, nw_ref[...], mod_ref[3:4, :], mod_ref[4:5, :])
    hb = h.astype(BF16)
    h_ref[...] = hb
    lg_ref[...] = _dot(hb, wr_ref[...]) + br_ref[...]


def _merge(x, lru, gdn, ssd, gate, modtab, norm_w, wb, wo, wr, br, n_lat_tiles):
    bsz, l, d = x.shape
    tok = pl.BlockSpec((None, ROW_TILE, d), lambda b, j: (b, j, 0))
    return pl.pallas_call(
        _merge_kernel,
        grid=(bsz, l // ROW_TILE),
        in_specs=[tok, tok, tok, tok,
                  pl.BlockSpec((None, ROW_TILE, 3 * d), lambda b, j: (b, j, 0)),
                  pl.BlockSpec((None, None, SUBLANE, d), lambda b, j: (b, jnp.where(j >= n_lat_tiles, 1, 0), 0, 0)),
                  pl.BlockSpec((1, d), lambda b, j: (0, 0)),
                  pl.BlockSpec((3, d, d), lambda b, j: (0, 0, 0)),
                  pl.BlockSpec((d, d), lambda b, j: (0, 0)),
                  pl.BlockSpec((d, LANE), lambda b, j: (0, 0)),
                  pl.BlockSpec((1, LANE), lambda b, j: (0, 0))],
        out_specs=[tok, tok, pl.BlockSpec((None, ROW_TILE, LANE), lambda b, j: (b, j, 0))],
        out_shape=[jax.ShapeDtypeStruct((bsz, l, d), F32), jax.ShapeDtypeStruct((bsz, l, d), BF16),
                   jax.ShapeDtypeStruct((bsz, l, LANE), F32)],
        compiler_params=_cparams("arbitrary", "arbitrary"),
        name="merge",
    )(x, lru, gdn, ssd, gate, modtab, norm_w, wb, wo, wr, br)


def _moe_kernel(blk_ref, exp_ref, lo_ref, hi_ref, x_ref, w1_ref, w3_ref, w2_ref, o_ref, acc_ref):
    i = pl.program_id(0)
    lo = lo_ref[i]
    hi = hi_ref[i]
    base = blk_ref[i] * MOE_ROWS

    @pl.when(lo == base)
    def _():
        acc_ref[...] = jnp.zeros_like(acc_ref)

    @pl.when(hi > lo)
    def _():
        row = base + _iota((MOE_ROWS, 1), 0)
        x = jnp.where(jnp.logical_and(row >= lo, row < hi), x_ref[...], jnp.zeros_like(x_ref))
        mid = _silu(_dot(x, w1_ref[...])) * _dot(x, w3_ref[...])
        acc_ref[...] += _dot(mid.astype(BF16), w2_ref[...])

    o_ref[...] = acc_ref[...].astype(o_ref.dtype)


def _moe_experts(xb, item_block, item_expert, item_lo, item_hi, w1, w3, w2):
    n_rows, d = xb.shape
    ff = w1.shape[-1]
    rows_spec = pl.BlockSpec((MOE_ROWS, d), lambda i, blk, ex, lo, hi: (blk[i], 0))
    grid_spec = pltpu.PrefetchScalarGridSpec(
        num_scalar_prefetch=4,
        grid=(item_block.shape[0],),
        in_specs=[rows_spec,
                  pl.BlockSpec((None, d, ff), lambda i, blk, ex, lo, hi: (ex[i], 0, 0)),
                  pl.BlockSpec((None, d, ff), lambda i, blk, ex, lo, hi: (ex[i], 0, 0)),
                  pl.BlockSpec((None, ff, d), lambda i, blk, ex, lo, hi: (ex[i], 0, 0))],
        out_specs=rows_spec,
        scratch_shapes=[pltpu.VMEM((MOE_ROWS, d), F32)],
    )
    return pl.pallas_call(
        _moe_kernel,
        grid_spec=grid_spec,
        out_shape=jax.ShapeDtypeStruct((n_rows, d), BF16),
        compiler_params=_cparams("arbitrary"),
        name="moe_experts",
    )(item_block, item_expert, item_lo, item_hi, xb, w1, w3, w2)


def _route(logits):
    t = logits.shape[0]
    g_logits = logits[:, :N_GROUPS]
    e_logits = logits[:, N_GROUPS:N_GROUPS + N_EXPERTS].reshape(t, N_GROUPS, EXPERTS_PER_GROUP)
    g_prob = jax.nn.softmax(g_logits, axis=-1)
    g_idx = jnp.argmax(g_logits, axis=-1)
    p_group = jnp.take_along_axis(g_prob, g_idx[:, None], axis=1)[:, 0]
    e_in_group = jnp.take_along_axis(e_logits, g_idx[:, None, None], axis=1)[:, 0]
    top_v, top_i = lax.top_k(e_in_group, TOP_K)
    weights = jax.nn.softmax(top_v, axis=-1) * p_group[:, None]
    expert_id = (g_idx[:, None] * EXPERTS_PER_GROUP + top_i).reshape(-1).astype(jnp.int32)
    n_assign = t * TOP_K
    assert n_assign % MOE_ROWS == 0
    n_blocks = n_assign // MOE_ROWS
    ar = jnp.arange(n_assign, dtype=jnp.int32)
    e_sorted, order = lax.sort((expert_id, ar), num_keys=1)
    _, slot = lax.sort((order, ar), num_keys=1)
    ends = jnp.sum(e_sorted[None, :] <= jnp.arange(N_EXPERTS, dtype=jnp.int32)[:, None], axis=1).astype(jnp.int32)
    starts = jnp.concatenate([jnp.zeros((1,), jnp.int32), ends[:-1]])
    cuts = jnp.sort(jnp.concatenate([jnp.arange(n_blocks, dtype=jnp.int32) * MOE_ROWS, starts]))
    item_lo = cuts
    item_hi = jnp.concatenate([cuts[1:], jnp.full((1,), n_assign, jnp.int32)])
    item_block = jnp.minimum(item_lo // MOE_ROWS, n_blocks - 1).astype(jnp.int32)
    item_expert = jnp.minimum(jnp.sum(item_lo[:, None] >= ends[None, :], axis=1), N_EXPERTS - 1).astype(jnp.int32)
    return order // TOP_K, (item_block, item_expert, item_lo, item_hi), slot.reshape(t, TOP_K), weights


def _combine_kernel(x_ref, ya_ref, yb_ref, wt_ref, mod_ref, nw_ref, o_ref, *, final):
    wt = wt_ref[...]
    y = wt[:, 0:1] * ya_ref[...].astype(F32) + wt[:, 1:2] * yb_ref[...].astype(F32)
    x = x_ref[...] + mod_ref[5:6, :] * y
    if final:
        x = x * lax.rsqrt(jnp.mean(x * x, axis=-1, keepdims=True) + NORM_EPS) * nw_ref[...]
    o_ref[...] = x


def _combine(x, ya, yb, wts, modtab, norm_w, n_lat_tiles, final, out_len):
    bsz, l, d = x.shape
    tok = pl.BlockSpec((None, ROW_TILE, d), lambda b, j: (b, j, 0))
    return pl.pallas_call(
        functools.partial(_combine_kernel, final=final),
        grid=(bsz, out_len // ROW_TILE),
        in_specs=[tok, tok, tok,
                  pl.BlockSpec((None, ROW_TILE, LANE), lambda b, j: (b, j, 0)),
                  pl.BlockSpec((None, None, SUBLANE, d), lambda b, j: (b, jnp.where(j >= n_lat_tiles, 1, 0), 0, 0)),
                  pl.BlockSpec((1, d), lambda b, j: (0, 0))],
        out_specs=tok,
        out_shape=jax.ShapeDtypeStruct((bsz, out_len, d), F32),
        compiler_params=_cparams("arbitrary", "arbitrary"),
        name="combine",
    )(x, ya, yb, wts, modtab, norm_w)


def _pad_lanes(v, n=LANE):
    v = v.reshape(1, -1).astype(F32)
    return jnp.pad(v, ((0, 0), (0, n - v.shape[1])))


def kernel(x, c, ctx, c_ctx, w_mod, b_mod, norm1_w, norm2_w, w_in, lru_conv_w, lru_conv_b, lru_wa, lru_ba,
           lru_wi, lru_bi, lru_lambda, gdn_conv_w, gdn_a_log, gdn_dt_bias, gdn_norm_w, ssd_conv_w, ssd_conv_b,
           ssd_a_log, ssd_dt_bias, ssd_d, ssd_norm_w, w_branch, w_out, router_group_w, router_group_b,
           router_expert_w, router_expert_b, expert_w1, expert_w3, expert_w2, final_norm_w):
    bsz, seq, d = x.shape
    n_ctx_tok = ctx.shape[1]
    depth = w_mod.shape[0]
    l = seq + n_ctx_tok
    rows = seq // GRID_W
    assert seq % LRU_CHUNK == 0 and n_ctx_tok % LRU_CHUNK == 0 and ROW_TILE == LRU_CHUNK
    assert bsz + 1 <= SUBLANE
    hw = GDN_HEADS * GDN_DK
    inner = SSD_HEADS * SSD_HEADDIM
    xbc_w = inner + 2 * SSD_GROUPS * SSD_STATE

    act = jnp.concatenate([_silu(c), _silu(c_ctx)[None, :],
                           jnp.zeros((SUBLANE - bsz - 1, d), F32)], axis=0)
    mod_all = _modulation(act, w_mod, b_mod).reshape(depth, SUBLANE, 6, d)
    pad2 = jnp.zeros((bsz, 2, d), F32)

    xs = jnp.concatenate([x, ctx], axis=1)
    nl_t, nc_t = seq // ROW_TILE, n_ctx_tok // ROW_TILE
    nl_s, nc_s = seq // SCAN_CHUNK, n_ctx_tok // SCAN_CHUNK

    o_lx, o_qkv, o_gz, o_gb, o_ga = 0, 2 * d, 2 * d + 3 * hw, 2 * d + 4 * hw, 2 * d + 4 * hw + 2 * GDN_HEADS
    o_sz = o_ga + 2 * GDN_HEADS
    o_xbc = o_sz + inner
    o_sdt = o_xbc + xbc_w
    o_gate = o_sdt + 2 * SSD_HEADS

    for i in range(depth):
        lat = mod_all[i, :bsz]
        cx = jnp.broadcast_to(mod_all[i, bsz][None], (bsz, 6, d))
        modtab = jnp.stack([jnp.concatenate([lat, pad2], axis=1), jnp.concatenate([cx, pad2], axis=1)], axis=1)

        wi_ = w_in[i]
        w_lru = wi_[:, o_lx:o_qkv].astype(BF16)
        zpad = jnp.zeros((d, LANE - 4 * GDN_HEADS), F32)
        w_gdn = jnp.concatenate([wi_[:, o_qkv:o_gb], wi_[:, o_ga:o_sz], wi_[:, o_gb:o_ga], zpad], axis=1).astype(BF16)
        zpad2 = jnp.zeros((d, LANE - 2 * SSD_HEADS), F32)
        w_ssd = jnp.concatenate([wi_[:, o_sz:o_gate], zpad2], axis=1).astype(BF16)
        w_gate = wi_[:, o_gate:].astype(BF16)
        n1 = norm1_w[i].reshape(1, d)

        ylru = _norm_matmul(xs, modtab, n1, w_lru, 0, 1, nl_t, "in_lru")
        ygdn = _norm_matmul(xs, modtab, n1, w_gdn, 0, 1, nl_t, "in_gdn")
        gate = _norm_matmul(xs, modtab, n1, w_gate, 0, 1, nl_t, "in_gate")
        x_cm = xs[:, :seq].reshape(bsz, rows, GRID_W, d).swapaxes(1, 2).reshape(bsz, seq, d)
        xs_scan = jnp.concatenate([x_cm, xs[:, seq:]], axis=1)
        yssd = _norm_matmul(xs_scan, modtab, n1, w_ssd, 0, 1, nl_t, "in_ssd")

        lru_args = (lru_conv_w[i], lru_conv_b[i].reshape(1, d))
        h_f = _lru_scan(ylru, *lru_args, lru_wa[i, 0].astype(BF16), lru_ba[i, 0].reshape(1, d),
                        lru_wi[i, 0].astype(BF16), lru_bi[i, 0].reshape(1, d), lru_lambda[i, 0].reshape(1, d),
                        None, nl_t, nc_t, False)
        lru_out = _lru_scan(ylru, *lru_args, lru_wa[i, 1].astype(BF16), lru_ba[i, 1].reshape(1, d),
                            lru_wi[i, 1].astype(BF16), lru_bi[i, 1].reshape(1, d), lru_lambda[i, 1].reshape(1, d),
                            h_f, nl_t, nc_t, True)

        q, k, v, gcol, grow = _gdn_prep(ygdn, gdn_conv_w[i], _pad_lanes(gdn_a_log[i]), _pad_lanes(gdn_dt_bias[i]),
                                        nl_t, nc_t)
        o_f = _gdn_scan(q, k, v, gcol, grow, None, None, None, nl_s, nc_s, False)
        gdn_out = _gdn_scan(q, k, v, gcol, grow, o_f, ygdn, gdn_norm_w[i].reshape(1, GDN_DK), nl_s, nc_s, True)

        dtb_vec = _pad_lanes(ssd_dt_bias[i])
        a_vec = _pad_lanes(-jnp.exp(ssd_a_log[i].astype(F32)))
        ssd_cw, ssd_cb = ssd_conv_w[i], ssd_conv_b[i].reshape(1, xbc_w)
        dsk = [jnp.repeat(ssd_d[i, dd], SSD_HEADDIM).reshape(1, inner) for dd in range(2)]
        y_f = _ssd_scan(yssd, ssd_cw, ssd_cb, dtb_vec, a_vec, dsk[0], None, None, nl_s, nc_s, False)
        ssd_scan_out = _ssd_scan(yssd, ssd_cw, ssd_cb, dtb_vec, a_vec, dsk[1], y_f,
                                 ssd_norm_w[i].reshape(1, inner), nl_s, nc_s, True)
        ssd_lat = ssd_scan_out[:, :seq].reshape(bsz, GRID_W, rows, inner).swapaxes(1, 2).reshape(bsz, seq, inner)
        ssd_out = jnp.concatenate([ssd_lat, ssd_scan_out[:, seq:]], axis=1)

        w_r = jnp.concatenate([router_group_w[i], router_expert_w[i],
                               jnp.zeros((d, LANE - N_GROUPS - N_EXPERTS), F32)], axis=1).astype(BF16)
        b_r = _pad_lanes(jnp.concatenate([router_group_b[i], router_expert_b[i]]))
        x_mid, h2, logits = _merge(xs, lru_out, gdn_out, ssd_out, gate, modtab, norm2_w[i].reshape(1, d),
                                   w_branch[i].astype(BF16), w_out[i].astype(BF16), w_r, b_r, nl_t)

        t = bsz * l
        tok_sorted, items, slot, weights = _route(logits.reshape(t, LANE))
        xb = h2.reshape(t, d)[tok_sorted]
        yb = _moe_experts(xb, *items, expert_w1[i].astype(BF16), expert_w3[i].astype(BF16),
                          expert_w2[i].astype(BF16))
        y0 = yb[slot[:, 0]].reshape(bsz, l, d)
        y1 = yb[slot[:, 1]].reshape(bsz, l, d)
        wts = jnp.pad(weights, ((0, 0), (0, LANE - TOP_K))).reshape(bsz, l, LANE)
        last = i == depth - 1
        xs = _combine(x_mid, y0, y1, wts, modtab, final_norm_w.reshape(1, d), nl_t, last, seq if last else l)

    return xs
```

```python
import functools

import jax
import jax.numpy as jnp
from jax import lax
from jax.experimental import pallas as pl
from jax.experimental.pallas import tpu as pltpu

GRID_W = 64
CONV_K = 4
NORM_EPS = 1e-6
LRU_BLOCKS = 8
LRU_C = 8.0
GDN_HEADS = 8
GDN_DK = 128
SSD_HEADS = 16
SSD_HEADDIM = 64
SSD_GROUPS = 2
SSD_STATE = 128
N_GROUPS = 4
EXPERTS_PER_GROUP = 8
N_EXPERTS = N_GROUPS * EXPERTS_PER_GROUP
TOP_K = 2

LANE = 128
SUBLANE = 8
ROW_TILE = 256
LRU_CHUNK = 256
SCAN_CHUNK = 128
MOE_ROWS = 256
VMEM_LIMIT = 56 * 1024 * 1024

F32 = jnp.float32
BF16 = jnp.bfloat16
NEG_BIG = -1e30


def _cparams(*sem):
    return pltpu.CompilerParams(dimension_semantics=sem, vmem_limit_bytes=VMEM_LIMIT)


def _dot(a, b):
    return jnp.dot(a, b, preferred_element_type=F32)


def _dot_nt(a, b):
    return lax.dot_general(a, b, (((1,), (1,)), ((), ())), preferred_element_type=F32)


def _dot_tn(a, b):
    return lax.dot_general(a, b, (((0,), (0,)), ((), ())), preferred_element_type=F32)


def _split3(x):
    hi = x.astype(BF16)
    r1 = x - hi.astype(F32)
    mid = r1.astype(BF16)
    lo = (r1 - mid.astype(F32)).astype(BF16)
    return hi, mid, lo


def _exact_left(m_bf, x):
    hi, mid, lo = _split3(x)
    return _dot(jnp.concatenate([m_bf, m_bf, m_bf], axis=1), jnp.concatenate([hi, mid, lo], axis=0))


def _exact_right(x, m_bf):
    hi, mid, lo = _split3(x)
    return _dot(jnp.concatenate([hi, mid, lo], axis=1), jnp.concatenate([m_bf, m_bf, m_bf], axis=0))


def _exact_transpose(x, eye_bf):
    hi, mid, lo = _split3(x)
    return _dot_nt(jnp.concatenate([eye_bf, eye_bf, eye_bf], axis=1), jnp.concatenate([hi, mid, lo], axis=1))


def _softplus(x):
    return jnp.maximum(x, 0.0) + jnp.log(1.0 + jnp.exp(-jnp.abs(x)))


def _silu(x):
    return x * jax.nn.sigmoid(x)


def _iota(shape, dim):
    return lax.broadcasted_iota(jnp.int32, shape, dim)


def _fwd_chunk(i, n_lat, n_ctx):
    return jnp.where(i < n_ctx, n_lat + i, i - n_ctx)


def _chunk_of_step(i, n_lat, n_ctx, reverse):
    return (n_lat + n_ctx - 1 - i) if reverse else _fwd_chunk(i, n_lat, n_ctx)


def _segment_edges(c, n_lat, n_ctx):
    first = jnp.logical_or(c == 0, c == n_lat)
    last = jnp.logical_or(c == n_lat - 1, c == n_lat + n_ctx - 1)
    return first, last


def _normed_rows(x_ref, xp_ref, xn_ref, mod_ref, nw_ref):
    shift, scale = mod_ref[0:1, :], mod_ref[1:2, :]
    h_main = _mod_norm(x_ref[...], nw_ref[...], shift, scale)
    h_prev = _mod_norm(xp_ref[...], nw_ref[...], shift, scale)
    h_next = _mod_norm(xn_ref[...], nw_ref[...], shift, scale)
    return h_main.astype(BF16), jnp.concatenate([h_prev, h_main, h_next], axis=0).astype(BF16)


def _proj_conv(xe_ref, h_all, w_ref, cw_ref, col0, width, first, last, rows, col_step=512):
    base = SUBLANE - CONV_K // 2
    for c0 in range(0, width, col_step):
        c1 = min(c0 + col_step, width)
        y = _dot(h_all, w_ref[:, col0 + c0:col0 + c1])
        xe_ref[0:SUBLANE, c0:c1] = jnp.where(first, 0.0, y[0:SUBLANE])
        xe_ref[SUBLANE:SUBLANE + rows, c0:c1] = y[SUBLANE:SUBLANE + rows]
        xe_ref[SUBLANE + rows:2 * SUBLANE + rows, c0:c1] = jnp.where(last, 0.0, y[SUBLANE + rows:])
        u = cw_ref[0:1, c0:c1] * xe_ref[base:base + rows, c0:c1]
        for j in range(1, CONV_K):
            u = u + cw_ref[j:j + 1, c0:c1] * xe_ref[base + j:base + j + rows, c0:c1]
        yield c0, c1, u


def _halo_specs(width, rows, n_rows_total, chunk_fn, col_block=0):
    per = rows // SUBLANE
    last_tile = n_rows_total // SUBLANE - 1
    prev = pl.BlockSpec((None, SUBLANE, width),
                        lambda b, i: (b, jnp.maximum(chunk_fn(i) * per - 1, 0), col_block))
    nxt = pl.BlockSpec((None, SUBLANE, width),
                       lambda b, i: (b, jnp.minimum((chunk_fn(i) + 1) * per, last_tile), col_block))
    return prev, nxt


def _mod_kernel(a_ref, w_ref, b_ref, o_ref):
    o_ref[...] = _dot(a_ref[...].astype(BF16), w_ref[...].astype(BF16)) + b_ref[...]


def _modulation(act, w_mod, b_mod):
    depth, d, n = w_mod.shape
    tn = 1536
    return pl.pallas_call(
        _mod_kernel,
        grid=(depth, n // tn),
        in_specs=[pl.BlockSpec((SUBLANE, d), lambda l, j: (0, 0)),
                  pl.BlockSpec((None, d, tn), lambda l, j: (l, 0, j)),
                  pl.BlockSpec((None, 1, tn), lambda l, j: (l, 0, j))],
        out_specs=pl.BlockSpec((None, SUBLANE, tn), lambda l, j: (l, 0, j)),
        out_shape=jax.ShapeDtypeStruct((depth, SUBLANE, n), F32),
        compiler_params=_cparams("arbitrary", "arbitrary"),
        name="modulation",
    )(act, w_mod, b_mod.reshape(depth, 1, n))


def _mod_norm(x, nw, shift, scale):
    y = x * lax.rsqrt(jnp.mean(x * x, axis=-1, keepdims=True) + NORM_EPS)
    return (y * nw) * (1.0 + scale) + shift


def _norm_matmul_kernel(x_ref, mod_ref, nw_ref, w_ref, o_ref, *, shift_row, scale_row, col_step):
    h = _mod_norm(x_ref[...], nw_ref[...], mod_ref[shift_row:shift_row + 1, :],
                  mod_ref[scale_row:scale_row + 1, :]).astype(BF16)
    n = w_ref.shape[1]
    for c0 in range(0, n, col_step):
        c1 = min(c0 + col_step, n)
        o_ref[:, c0:c1] = _dot(h, w_ref[:, c0:c1]).astype(o_ref.dtype)


def _norm_matmul(x, modtab, norm_w, w, shift_row, scale_row, n_lat_tiles, name):
    bsz, l, d = x.shape
    n = w.shape[1]
    kern = functools.partial(_norm_matmul_kernel, shift_row=shift_row, scale_row=scale_row, col_step=512)
    return pl.pallas_call(
        kern,
        grid=(bsz, l // ROW_TILE),
        in_specs=[pl.BlockSpec((None, ROW_TILE, d), lambda b, j: (b, j, 0)),
                  pl.BlockSpec((None, None, SUBLANE, d), lambda b, j: (b, jnp.where(j >= n_lat_tiles, 1, 0), 0, 0)),
                  pl.BlockSpec((1, d), lambda b, j: (0, 0)),
                  pl.BlockSpec((d, n), lambda b, j: (0, 0))],
        out_specs=pl.BlockSpec((None, ROW_TILE, n), lambda b, j: (b, j, 0)),
        out_shape=jax.ShapeDtypeStruct((bsz, l, n), BF16),
        compiler_params=_cparams("arbitrary", "arbitrary"),
        name=name,
    )(x, modtab, norm_w, w)


def _row_tile_specs(d, l, n_lat_tiles):
    ident = lambda j: j
    main = pl.BlockSpec((None, ROW_TILE, d), lambda b, j: (b, j, 0))
    prev, nxt = _halo_specs(d, ROW_TILE, l, ident)
    mod = pl.BlockSpec((None, None, SUBLANE, d), lambda b, j: (b, jnp.where(j >= n_lat_tiles, 1, 0), 0, 0))
    return [main, prev, nxt, mod, pl.BlockSpec((1, d), lambda b, j: (0, 0))]


def _in_lru_kernel(x_ref, xp_ref, xn_ref, mod_ref, nw_ref, w_ref, cw_ref, cb_ref, u_ref, y_ref, xe_ref,
                   *, n_lat, n_ctx):
    first, last = _segment_edges(pl.program_id(1), n_lat, n_ctx)
    h_main, h_all = _normed_rows(x_ref, xp_ref, xn_ref, mod_ref, nw_ref)
    w = u_ref.shape[1]
    for c0, c1, u in _proj_conv(xe_ref, h_all, w_ref, cw_ref, 0, w, first, last, ROW_TILE):
        u_ref[:, c0:c1] = u + cb_ref[:, c0:c1]
    for c0 in range(0, w, 512):
        y_ref[:, c0:c0 + 512] = _dot(h_main, w_ref[:, w + c0:w + c0 + 512])


def _in_lru(x, modtab, norm_w, w, conv_w, conv_b, n_lat, n_ctx):
    bsz, l, d = x.shape
    wd = w.shape[1] // 2
    tok = pl.BlockSpec((None, ROW_TILE, wd), lambda b, j: (b, j, 0))
    return pl.pallas_call(
        functools.partial(_in_lru_kernel, n_lat=n_lat, n_ctx=n_ctx),
        grid=(bsz, l // ROW_TILE),
        in_specs=_row_tile_specs(d, l, n_lat) + [pl.BlockSpec((d, 2 * wd), lambda b, j: (0, 0)),
                                                 pl.BlockSpec((CONV_K, wd), lambda b, j: (0, 0)),
                                                 pl.BlockSpec((1, wd), lambda b, j: (0, 0))],
        out_specs=[tok, tok],
        out_shape=[jax.ShapeDtypeStruct((bsz, l, wd), F32)] * 2,
        scratch_shapes=[pltpu.VMEM((ROW_TILE + 2 * SUBLANE, wd), F32)],
        compiler_params=_cparams("arbitrary", "arbitrary"),
        name="in_lru",
    )(x, x, x, modtab, norm_w, w, conv_w, conv_b)


def _lru_kernel(*refs, reverse, final):
    if final:
        u_ref, wa_ref, ba_ref, wi_ref, bi_ref, lam_ref, hf_ref, y_ref, o_ref, carry_ref = refs
    else:
        u_ref, wa_ref, ba_ref, wi_ref, bi_ref, lam_ref, o_ref, carry_ref = refs
    rows = LRU_CHUNK
    i = pl.program_id(1)

    @pl.when(i == 0)
    def _():
        carry_ref[...] = jnp.zeros_like(carry_ref)

    u = u_ref[...]
    ub = u.astype(BF16)
    bw = u.shape[1] // LRU_BLOCKS
    pre_r = jnp.concatenate([_dot(ub[:, n * bw:(n + 1) * bw], wa_ref[n]) for n in range(LRU_BLOCKS)], axis=1)
    pre_i = jnp.concatenate([_dot(ub[:, n * bw:(n + 1) * bw], wi_ref[n]) for n in range(LRU_BLOCKS)], axis=1)
    r = jax.nn.sigmoid(pre_r + ba_ref[...])
    gi = jax.nn.sigmoid(pre_i + bi_ref[...])
    a = jnp.exp(-LRU_C * r * _softplus(-lam_ref[...]))
    h = jnp.sqrt(1.0 - a * a) * gi * u

    srow = _iota((SUBLANE, a.shape[1]), 0)
    n_tiles = rows // SUBLANE
    carry = carry_ref[...]
    done = {}
    for k in (range(n_tiles - 1, -1, -1) if reverse else range(n_tiles)):
        sl = slice(k * SUBLANE, (k + 1) * SUBLANE)
        at, ht = a[sl], h[sl]
        for s in (1, 2, 4):
            if reverse:
                keep = srow < SUBLANE - s
                sh = SUBLANE - s
            else:
                keep = srow >= s
                sh = s
            h_s = jnp.where(keep, pltpu.roll(ht, sh, axis=0), 0.0)
            a_s = jnp.where(keep, pltpu.roll(at, sh, axis=0), 1.0)
            ht = ht + at * h_s
            at = at * a_s
        hk = ht + at * carry
        carry = hk[0:1, :] if reverse else hk[SUBLANE - 1:SUBLANE, :]
        if final:
            done[k] = (hk + hf_ref[sl, :]) * jax.nn.gelu(y_ref[sl, :], approximate=True)
            if (k ^ 1) in done:
                lo = k & ~1
                pair = jnp.concatenate([done.pop(lo), done.pop(lo + 1)], axis=0)
                o_ref[lo * SUBLANE:(lo + 2) * SUBLANE, :] = pair.astype(o_ref.dtype)
        else:
            o_ref[sl, :] = hk
    carry_ref[...] = carry


def _lru_scan(u, y, wa, ba, wi, bi, lam, hf, n_lat, n_ctx, reverse):
    bsz, l, w = u.shape
    final = hf is not None
    chunk_fn = functools.partial(_chunk_of_step, n_lat=n_lat, n_ctx=n_ctx, reverse=reverse)
    main = pl.BlockSpec((None, LRU_CHUNK, w), lambda b, i: (b, chunk_fn(i), 0))
    vec = pl.BlockSpec((1, w), lambda b, i: (0, 0))
    blk = pl.BlockSpec((LRU_BLOCKS, w // LRU_BLOCKS, w // LRU_BLOCKS), lambda b, i: (0, 0, 0))
    in_specs = [main, blk, vec, blk, vec, vec]
    args = [u, wa, ba, wi, bi, lam]
    if final:
        in_specs += [main, main]
        args += [hf, y]
    return pl.pallas_call(
        functools.partial(_lru_kernel, reverse=reverse, final=final),
        grid=(bsz, l // LRU_CHUNK),
        in_specs=in_specs,
        out_specs=main,
        out_shape=jax.ShapeDtypeStruct((bsz, l, w), BF16 if final else F32),
        scratch_shapes=[pltpu.VMEM((1, w), F32)],
        compiler_params=_cparams("arbitrary", "arbitrary"),
        name="lru_rev" if reverse else "lru_fwd",
    )(*args)


def _in_gdn_kernel(x_ref, xp_ref, xn_ref, mod_ref, nw_ref, w_ref, cw_ref, alog_ref, dtb_ref,
                   q_ref, k_ref, v_ref, z_ref, gcol_ref, grow_ref, xe_ref, *, n_lat, n_ctx):
    rows = ROW_TILE
    first, last = _segment_edges(pl.program_id(1), n_lat, n_ctx)
    h_main, h_all = _normed_rows(x_ref, xp_ref, xn_ref, mod_ref, nw_ref)
    hw = GDN_HEADS * GDN_DK
    for c0, c1, u in _proj_conv(xe_ref, h_all, w_ref, cw_ref, 0, 3 * hw, first, last, rows):
        u = _silu(u)
        for h0 in range(c0, c1, GDN_DK):
            uh = u[:, h0 - c0:h0 - c0 + GDN_DK]
            if h0 < 2 * hw:
                uh = uh * lax.rsqrt(jnp.sum(uh * uh, axis=-1, keepdims=True) + NORM_EPS)
            if h0 < hw:
                q_ref[:, h0:h0 + GDN_DK] = uh * (GDN_DK ** -0.5)
            elif h0 < 2 * hw:
                k_ref[:, h0 - hw:h0 - hw + GDN_DK] = uh
            else:
                v_ref[:, h0 - 2 * hw:h0 - 2 * hw + GDN_DK] = uh
    for c0 in range(0, hw, 512):
        z_ref[:, c0:c0 + 512] = _dot(h_main, w_ref[:, 3 * hw + c0:3 * hw + c0 + 512])

    sm = _dot(h_main, w_ref[:, 4 * hw:4 * hw + LANE])
    g = -jnp.exp(alog_ref[...]) * _softplus(sm + dtb_ref[...])
    beta = jax.nn.sigmoid(sm)
    ri = _iota((rows, rows), 0)
    ci = _iota((rows, rows), 1)
    same = (ri // SCAN_CHUNK) == (ci // SCAN_CHUNK)
    tri_f = jnp.where(jnp.logical_and(same, ci <= ri), 1.0, 0.0).astype(BF16)
    tri_r = jnp.where(jnp.logical_and(same, ci >= ri), 1.0, 0.0).astype(BF16)
    gcs_f = _exact_left(tri_f, g)
    gcs_r = _exact_left(tri_r, g)
    lane = _iota((rows, LANE), 1)
    gcol = jnp.where(lane < GDN_HEADS, gcs_f,
                     jnp.where(lane < 2 * GDN_HEADS, gcs_r, jnp.where(lane < 4 * GDN_HEADS, beta, 0.0)))
    gcol_ref[...] = gcol
    eye = jnp.where(_iota((LANE, LANE), 0) == _iota((LANE, LANE), 1), 1.0, 0.0).astype(BF16)
    grow_ref[...] = _exact_transpose(gcol, eye)[0:4 * GDN_HEADS, :]


def _in_gdn(x, modtab, norm_w, w, conv_w, alog_vec, dtb_vec, n_lat, n_ctx):
    bsz, l, d = x.shape
    hw = GDN_HEADS * GDN_DK
    tok = pl.BlockSpec((None, ROW_TILE, hw), lambda b, i: (b, i, 0))
    return pl.pallas_call(
        functools.partial(_in_gdn_kernel, n_lat=n_lat, n_ctx=n_ctx),
        grid=(bsz, l // ROW_TILE),
        in_specs=_row_tile_specs(d, l, n_lat) + [pl.BlockSpec((d, w.shape[1]), lambda b, i: (0, 0)),
                                                 pl.BlockSpec((CONV_K, 3 * hw), lambda b, i: (0, 0)),
                                                 pl.BlockSpec((1, LANE), lambda b, i: (0, 0)),
                                                 pl.BlockSpec((1, LANE), lambda b, i: (0, 0))],
        out_specs=[tok, tok, tok, tok,
                   pl.BlockSpec((None, ROW_TILE, LANE), lambda b, i: (b, i, 0)),
                   pl.BlockSpec((None, 4 * GDN_HEADS, ROW_TILE), lambda b, i: (b, 0, i))],
        out_shape=[jax.ShapeDtypeStruct((bsz, l, hw), F32)] * 4
        + [jax.ShapeDtypeStruct((bsz, l, LANE), F32), jax.ShapeDtypeStruct((bsz, 4 * GDN_HEADS, l), F32)],
        scratch_shapes=[pltpu.VMEM((ROW_TILE + 2 * SUBLANE, 3 * hw), F32)],
        compiler_params=_cparams("arbitrary", "arbitrary"),
        name="in_gdn",
    )(x, x, x, modtab, norm_w, w, conv_w, alog_vec, dtb_vec)


def _pair_blockdiag(x):
    c = x.shape[0]
    z = jnp.zeros((c, c), x.dtype)
    return jnp.concatenate([jnp.concatenate([x[:, :c], z], axis=1),
                            jnp.concatenate([z, x[:, c:]], axis=1)], axis=0)


def _gdn_scan_kernel(*refs, reverse, final, n_lat, n_ctx):
    if final:
        q_ref, k_ref, v_ref, gcol_ref, grow_ref, of_ref, z_ref, nw_ref, o_ref, s_ref = refs
    else:
        q_ref, k_ref, v_ref, gcol_ref, grow_ref, o_ref, s_ref = refs
    rows = SCAN_CHUNK
    n_pairs = GDN_HEADS // 2
    pw = 2 * GDN_DK
    chains = [(bb, p) for bb in range(q_ref.shape[0]) for p in range(n_pairs)]
    n_chains = len(chains)
    i = pl.program_id(1)

    @pl.when(i == 0)
    def _():
        s_ref[...] = jnp.zeros_like(s_ref)

    ri = _iota((rows, rows), 0)
    ci = _iota((rows, rows), 1)
    incl = (ri <= ci) if reverse else (ri >= ci)
    ri2 = _iota((rows, pw), 0)
    ci2 = jnp.bitwise_and(_iota((rows, pw), 1), rows - 1)
    strict2 = (ri2 < ci2) if reverse else (ri2 > ci2)
    d = 1 if reverse else 0
    edge = 0 if reverse else rows - 1

    def pair_cols(gcol, lane0):
        return jnp.concatenate([jnp.broadcast_to(gcol[:, lane0 + j:lane0 + j + 1], (rows, GDN_DK))
                                for j in range(2)], axis=1)

    qs, ks, kbs, egs, gcs, decs, rhs, sts = [], [], [], [], [], [], [], []
    for c, (bb, p) in enumerate(chains):
        sl = slice(p * pw, (p + 1) * pw)
        li = d * GDN_HEADS + 2 * p
        gcol = gcol_ref[bb]
        grow = grow_ref[bb]
        q2, k2, v2 = q_ref[bb, :, sl], k_ref[bb, :, sl], v_ref[bb, :, sl]
        gc2 = pair_cols(gcol, li)
        beta2 = pair_cols(gcol, 2 * GDN_HEADS + li)
        eg2 = jnp.exp(gc2)
        kb2 = k2 * beta2
        dec2 = jnp.concatenate(
            [jnp.exp(jnp.where(incl, gcol[:, li + j:li + j + 1] - grow[li + j:li + j + 1, :], NEG_BIG))
             for j in range(2)], axis=1)
        qs.append(q2)
        ks.append(k2)
        kbs.append(kb2)
        egs.append(eg2)
        gcs.append(gc2)
        decs.append(dec2)
        rhs.append([jnp.concatenate([v2[:, j * GDN_DK:(j + 1) * GDN_DK] * beta2[:, j * GDN_DK:(j + 1) * GDN_DK],
                                     kb2[:, j * GDN_DK:(j + 1) * GDN_DK] * eg2[:, j * GDN_DK:(j + 1) * GDN_DK]],
                                    axis=1).astype(BF16) for j in range(2)])
        sts.append(s_ref[c])

    a_mats, attns = [], []
    for p in range(n_chains):
        kq = _dot_nt(jnp.concatenate([kbs[p], qs[p]], axis=0).astype(BF16), _pair_blockdiag(ks[p].astype(BF16)))
        a_mats.append(jnp.where(strict2, kq[:rows] * decs[p], 0.0))
        attns.append((kq[rows:] * decs[p]).astype(BF16))

    half = rows // 2
    rq = _iota((half, pw), 0)
    lq = _iota((half, pw), 1)
    cq = jnp.bitwise_and(lq, half - 1)
    low_half = jnp.bitwise_and(lq, rows - 1) < half
    blk_q = lq // half

    def quad_blockdiag(y):
        return jnp.concatenate([jnp.where(blk_q == b, y, jnp.zeros_like(y)) for b in range(pw // half)], axis=0)

    aqs = [jnp.where(low_half, a_mats[p][:half], a_mats[p][half:]) for p in range(n_chains)]
    tqs = [jnp.where(rq == cq, 1.0, 0.0) - jnp.where((rq // 2) == (cq // 2), aqs[p], 0.0) for p in range(n_chains)]
    s = 2
    while s < half:
        off = jnp.logical_and((rq // (2 * s)) == (cq // (2 * s)), (rq // s) != (cq // s))
        xs = [_dot(jnp.where(off, aqs[p], 0.0).astype(BF16), quad_blockdiag(tqs[p].astype(BF16)))
              for p in range(n_chains)]
        tqs = [tqs[p] - _dot(tqs[p].astype(BF16), quad_blockdiag(xs[p].astype(BF16))) for p in range(n_chains)]
        s *= 2
    ts = [jnp.concatenate([jnp.where(low_half, tqs[p], 0.0), jnp.where(low_half, 0.0, tqs[p])], axis=0)
          for p in range(n_chains)]
    off = (ri2 // half) != (ci2 // half)
    xs = [_dot(jnp.where(off, a_mats[p], 0.0).astype(BF16), _pair_blockdiag(ts[p].astype(BF16)))
          for p in range(n_chains)]
    ts = [ts[p] - _dot(ts[p].astype(BF16), _pair_blockdiag(xs[p].astype(BF16))) for p in range(n_chains)]

    us, ws = [], []
    for p in range(n_chains):
        tb = ts[p].astype(BF16)
        sol = [_dot(tb[:, j * GDN_DK:(j + 1) * GDN_DK], rhs[p][j]) for j in range(2)]
        us.append(jnp.concatenate([sol[0][:, :GDN_DK], sol[1][:, :GDN_DK]], axis=1))
        ws.append(jnp.concatenate([sol[0][:, GDN_DK:], sol[1][:, GDN_DK:]], axis=1))

    vns, outs = [], []
    for p in range(n_chains):
        wq = jnp.concatenate([ws[p], qs[p] * egs[p]], axis=0).astype(BF16)
        ws_qs = _dot(wq, _pair_blockdiag(sts[p].astype(BF16)))
        vns.append((us[p] - ws_qs[:rows]).astype(BF16))
        outs.append(ws_qs[rows:])
    for c, (bb, p) in enumerate(chains):
        o2 = outs[c] + _dot(attns[c], _pair_blockdiag(vns[c]))
        g_end = gcs[c][edge:edge + 1, :]
        k_dec = (ks[c] * jnp.exp(g_end - gcs[c])).astype(BF16)
        full = _dot_tn(k_dec, vns[c])
        upd = jnp.concatenate([full[:GDN_DK, :GDN_DK], full[GDN_DK:, GDN_DK:]], axis=1)
        s_ref[c] = sts[c] * jnp.exp(g_end) + upd
        sl = slice(p * pw, (p + 1) * pw)
        if final:
            o2 = o2 + of_ref[bb, :, sl]
            ys = []
            for j in range(2):
                oj = o2[:, j * GDN_DK:(j + 1) * GDN_DK]
                ys.append(oj * lax.rsqrt(jnp.mean(oj * oj, axis=-1, keepdims=True) + NORM_EPS) * nw_ref[...])
            o_ref[bb, :, sl] = (jnp.concatenate(ys, axis=1) * _silu(z_ref[bb, :, sl])).astype(o_ref.dtype)
        else:
            o_ref[bb, :, sl] = o2


def _gdn_scan(q, k, v, gcol, grow, o_fwd, z, norm_w, n_lat, n_ctx, reverse):
    bsz, l, hw = q.shape
    final = o_fwd is not None
    chunk_fn = functools.partial(_chunk_of_step, n_lat=n_lat, n_ctx=n_ctx, reverse=reverse)
    nb = 2 if bsz % 2 == 0 else 1
    tok = pl.BlockSpec((nb, SCAN_CHUNK, hw), lambda b, i: (b, chunk_fn(i), 0))
    in_specs = [tok, tok, tok,
                pl.BlockSpec((nb, SCAN_CHUNK, LANE), lambda b, i: (b, chunk_fn(i), 0)),
                pl.BlockSpec((nb, 4 * GDN_HEADS, SCAN_CHUNK), lambda b, i: (b, 0, chunk_fn(i)))]
    args = [q, k, v, gcol, grow]
    if final:
        in_specs += [tok, tok, pl.BlockSpec((1, GDN_DK), lambda b, i: (0, 0))]
        args += [o_fwd, z, norm_w]
    return pl.pallas_call(
        functools.partial(_gdn_scan_kernel, reverse=reverse, final=final, n_lat=n_lat, n_ctx=n_ctx),
        grid=(bsz // nb, l // SCAN_CHUNK),
        in_specs=in_specs,
        out_specs=tok,
        out_shape=jax.ShapeDtypeStruct((bsz, l, hw), BF16 if final else F32),
        scratch_shapes=[pltpu.VMEM((nb * GDN_HEADS // 2, GDN_DK, 2 * GDN_DK), F32)],
        compiler_params=_cparams("arbitrary", "arbitrary"),
        name="gdn_rev" if reverse else "gdn_fwd",
    )(*args)


def _in_ssd_kernel(x_ref, xp_ref, xn_ref, mod_ref, nw_ref, w_ref, cw_ref, cb_ref, z_ref, u_ref, dt_ref, xe_ref,
                   *, n_lat, n_ctx):
    first, last = _segment_edges(pl.program_id(1), n_lat, n_ctx)
    h_main, h_all = _normed_rows(x_ref, xp_ref, xn_ref, mod_ref, nw_ref)
    inner = z_ref.shape[1]
    xbc_w = u_ref.shape[1]
    for c0 in range(0, inner, 512):
        z_ref[:, c0:c0 + 512] = _dot(h_main, w_ref[:, c0:c0 + 512])
    for c0, c1, u in _proj_conv(xe_ref, h_all, w_ref, cw_ref, inner, xbc_w, first, last, ROW_TILE):
        u_ref[:, c0:c1] = _silu(u + cb_ref[:, c0:c1])
    dt_ref[...] = _dot(h_main, w_ref[:, inner + xbc_w:inner + xbc_w + LANE])


def _in_ssd(x, modtab, norm_w, w, conv_w, conv_b, n_lat, n_ctx):
    bsz, l, d = x.shape
    inner = SSD_HEADS * SSD_HEADDIM
    xbc_w = inner + 2 * SSD_GROUPS * SSD_STATE
    tok = lambda n: pl.BlockSpec((None, ROW_TILE, n), lambda b, j: (b, j, 0))
    return pl.pallas_call(
        functools.partial(_in_ssd_kernel, n_lat=n_lat, n_ctx=n_ctx),
        grid=(bsz, l // ROW_TILE),
        in_specs=_row_tile_specs(d, l, n_lat) + [pl.BlockSpec((d, w.shape[1]), lambda b, j: (0, 0)),
                                                 pl.BlockSpec((CONV_K, xbc_w), lambda b, j: (0, 0)),
                                                 pl.BlockSpec((1, xbc_w), lambda b, j: (0, 0))],
        out_specs=[tok(inner), tok(xbc_w), tok(LANE)],
        out_shape=[jax.ShapeDtypeStruct((bsz, l, inner), F32), jax.ShapeDtypeStruct((bsz, l, xbc_w), F32),
                   jax.ShapeDtypeStruct((bsz, l, LANE), F32)],
        scratch_shapes=[pltpu.VMEM((ROW_TILE + 2 * SUBLANE, xbc_w), F32)],
        compiler_params=_cparams("arbitrary", "arbitrary"),
        name="in_ssd",
    )(x, x, x, modtab, norm_w, w, conv_w, conv_b)


def _ssd_scan_kernel(*refs, reverse, final):
    if final:
        z_ref, u_ref, dt_ref, dtb_ref, a_ref, dsk_ref, yf_ref, nw_ref, o_ref, st_ref = refs
    else:
        u_ref, dt_ref, dtb_ref, a_ref, dsk_ref, o_ref, st_ref = refs
    rows = SCAN_CHUNK
    inner = SSD_HEADS * SSD_HEADDIM
    gw = inner // SSD_GROUPS
    i = pl.program_id(1)

    @pl.when(i == 0)
    def _():
        st_ref[...] = jnp.zeros_like(st_ref)

    u = u_ref[...]
    xs = u[:, :inner]
    dt = _softplus(dt_ref[...] + dtb_ref[...])
    adt = a_ref[...] * dt

    ri = _iota((rows, rows), 0)
    ci = _iota((rows, rows), 1)
    incl = (ri <= ci) if reverse else (ri >= ci)
    tri = jnp.where(incl, 1.0, 0.0).astype(BF16)
    eye = jnp.where(ri == ci, 1.0, 0.0).astype(BF16)
    acs = _exact_left(tri, adt)
    acs_t = _exact_transpose(acs, eye)
    d = 1 if reverse else 0
    edge = 0 if reverse else rows - 1
    er = _iota((LANE, inner), 0)
    ec = _iota((LANE, inner), 1)
    expand = jnp.where(er == d * SSD_HEADS + ec // SSD_HEADDIM, 1.0, 0.0).astype(BF16)
    dt_e = _exact_right(dt, expand)
    ea_e = _exact_right(jnp.exp(acs), expand)
    dc_e = _exact_right(jnp.exp(acs[edge:edge + 1, :] - acs), expand)
    xdt = xs * dt_e
    xdt_b = xdt.astype(BF16)
    xw_b = (xdt * dc_e).astype(BF16)
    lane = _iota((rows, LANE), 1)

    y_parts = []
    for g in range(SSD_GROUPS):
        bm = u[:, inner + g * SSD_STATE:inner + (g + 1) * SSD_STATE].astype(BF16)
        cm = u[:, inner + (SSD_GROUPS + g) * SSD_STATE:inner + (SSD_GROUPS + g + 1) * SSD_STATE].astype(BF16)
        cb = _dot_nt(cm, bm)
        st = st_ref[g]
        y_off = _dot(cm, st.astype(BF16)) * ea_e[:, g * gw:(g + 1) * gw]
        heads_per_group = SSD_HEADS // SSD_GROUPS
        diag = []
        for hp in range(heads_per_group // 2):
            h0 = g * heads_per_group + 2 * hp
            mats = []
            for hh in (h0, h0 + 1):
                li = d * SSD_HEADS + hh
                lm = jnp.exp(jnp.where(incl, acs[:, li:li + 1] - acs_t[li:li + 1, :], NEG_BIG))
                mats.append((cb * lm).astype(BF16))
            xp = xdt_b[:, h0 * SSD_HEADDIM:(h0 + 2) * SSD_HEADDIM]
            x_lo = jnp.where(lane < SSD_HEADDIM, xp, jnp.zeros_like(xp))
            x_hi = jnp.where(lane >= SSD_HEADDIM, xp, jnp.zeros_like(xp))
            diag.append(_dot(jnp.concatenate(mats, axis=1), jnp.concatenate([x_lo, x_hi], axis=0)))
        y_parts.append(jnp.concatenate(diag, axis=1) + y_off)
        st_ref[g] = (st * ea_e[edge:edge + 1, g * gw:(g + 1) * gw]
                     + _dot_tn(bm, xw_b[:, g * gw:(g + 1) * gw]))
    y = jnp.concatenate(y_parts, axis=1) + dsk_ref[...] * xs
    if final:
        y = y + yf_ref[...]
        yz = y * _silu(z_ref[...])
        outs = []
        for g in range(SSD_GROUPS):
            yg = yz[:, g * gw:(g + 1) * gw]
            outs.append(yg * lax.rsqrt(jnp.mean(yg * yg, axis=-1, keepdims=True) + NORM_EPS))
        o_ref[...] = (jnp.concatenate(outs, axis=1) * nw_ref[...]).astype(o_ref.dtype)
    else:
        o_ref[...] = y


def _ssd_scan(z, u, dt_raw, dtb_vec, a_vec, dskip, y_fwd, norm_w, n_lat, n_ctx, reverse):
    bsz, l, xbc_w = u.shape
    inner = SSD_HEADS * SSD_HEADDIM
    final = y_fwd is not None
    chunk_fn = functools.partial(_chunk_of_step, n_lat=n_lat, n_ctx=n_ctx, reverse=reverse)
    tok = lambda n: pl.BlockSpec((None, SCAN_CHUNK, n), lambda b, i: (b, chunk_fn(i), 0))
    row = lambda n: pl.BlockSpec((1, n), lambda b, i: (0, 0))
    in_specs = [tok(xbc_w), tok(LANE), row(LANE), row(LANE), row(inner)]
    args = [u, dt_raw, dtb_vec, a_vec, dskip]
    if final:
        in_specs = [tok(inner)] + in_specs + [tok(inner), row(inner)]
        args = [z] + args + [y_fwd, norm_w]
    return pl.pallas_call(
        functools.partial(_ssd_scan_kernel, reverse=reverse, final=final),
        grid=(bsz, l // SCAN_CHUNK),
        in_specs=in_specs,
        out_specs=tok(inner),
        out_shape=jax.ShapeDtypeStruct((bsz, l, inner), BF16 if final else F32),
        scratch_shapes=[pltpu.VMEM((SSD_GROUPS, SSD_STATE, inner // SSD_GROUPS), F32)],
        compiler_params=_cparams("arbitrary", "arbitrary"),
        name="ssd_rev" if reverse else "ssd_fwd",
    )(*args)


def _merge_kernel(x_ref, lru_ref, gdn_ref, ssd_ref, gate_ref, mod_ref, nw_ref, wb_ref, wo_ref, wr_ref, br_ref,
                  xo_ref, h_ref, lg_ref):
    d = x_ref.shape[1]
    acc = None
    for k, b_ref in enumerate((lru_ref, gdn_ref, ssd_ref)):
        t = jax.nn.sigmoid(gate_ref[:, k * d:(k + 1) * d].astype(F32)) * _dot(b_ref[...].astype(BF16), wb_ref[k])
        acc = t if acc is None else acc + t
    mix = _dot(acc.astype(BF16), wo_ref[...])
    x = x_ref[...] + mod_ref[2:3, :] * mix
    xo_ref[...] = x
    h = _mod_norm(x, nw_ref[...], mod_ref[3:4, :], mod_ref[4:5, :])
    hb = h.astype(BF16)
    h_ref[...] = hb
    lg_ref[...] = _dot(hb, wr_ref[...]) + br_ref[...]


def _merge(x, lru, gdn, ssd, gate, modtab, norm_w, wb, wo, wr, br, n_lat_tiles):
    bsz, l, d = x.shape
    tok = pl.BlockSpec((None, ROW_TILE, d), lambda b, j: (b, j, 0))
    return pl.pallas_call(
        _merge_kernel,
        grid=(bsz, l // ROW_TILE),
        in_specs=[tok, tok, tok, tok,
                  pl.BlockSpec((None, ROW_TILE, 3 * d), lambda b, j: (b, j, 0)),
                  pl.BlockSpec((None, None, SUBLANE, d), lambda b, j: (b, jnp.where(j >= n_lat_tiles, 1, 0), 0, 0)),
                  pl.BlockSpec((1, d), lambda b, j: (0, 0)),
                  pl.BlockSpec((3, d, d), lambda b, j: (0, 0, 0)),
                  pl.BlockSpec((d, d), lambda b, j: (0, 0)),
                  pl.BlockSpec((d, LANE), lambda b, j: (0, 0)),
                  pl.BlockSpec((1, LANE), lambda b, j: (0, 0))],
        out_specs=[tok, tok, pl.BlockSpec((None, ROW_TILE, LANE), lambda b, j: (b, j, 0))],
        out_shape=[jax.ShapeDtypeStruct((bsz, l, d), F32), jax.ShapeDtypeStruct((bsz, l, d), BF16),
                   jax.ShapeDtypeStruct((bsz, l, LANE), F32)],
        compiler_params=_cparams("arbitrary", "arbitrary"),
        name="merge",
    )(x, lru, gdn, ssd, gate, modtab, norm_w, wb, wo, wr, br)


def _moe_kernel(blk_ref, exp_ref, lo_ref, hi_ref, x_ref, w1_ref, w3_ref, w2_ref, o_ref, acc_ref):
    i = pl.program_id(0)
    lo = lo_ref[i]
    hi = hi_ref[i]
    base = blk_ref[i] * MOE_ROWS

    @pl.when(lo == base)
    def _():
        acc_ref[...] = jnp.zeros_like(acc_ref)

    @pl.when(hi > lo)
    def _():
        row = base + _iota((MOE_ROWS, 1), 0)
        x = jnp.where(jnp.logical_and(row >= lo, row < hi), x_ref[...], jnp.zeros_like(x_ref))
        mid = _silu(_dot(x, w1_ref[...])) * _dot(x, w3_ref[...])
        acc_ref[...] += _dot(mid.astype(BF16), w2_ref[...])

    o_ref[...] = acc_ref[...].astype(o_ref.dtype)


def _moe_experts(xb, item_block, item_expert, item_lo, item_hi, w1, w3, w2):
    n_rows, d = xb.shape
    ff = w1.shape[-1]
    rows_spec = pl.BlockSpec((MOE_ROWS, d), lambda i, blk, ex, lo, hi: (blk[i], 0))
    grid_spec = pltpu.PrefetchScalarGridSpec(
        num_scalar_prefetch=4,
        grid=(item_block.shape[0],),
        in_specs=[rows_spec,
                  pl.BlockSpec((None, d, ff), lambda i, blk, ex, lo, hi: (ex[i], 0, 0)),
                  pl.BlockSpec((None, d, ff), lambda i, blk, ex, lo, hi: (ex[i], 0, 0)),
                  pl.BlockSpec((None, ff, d), lambda i, blk, ex, lo, hi: (ex[i], 0, 0))],
        out_specs=rows_spec,
        scratch_shapes=[pltpu.VMEM((MOE_ROWS, d), F32)],
    )
    return pl.pallas_call(
        _moe_kernel,
        grid_spec=grid_spec,
        out_shape=jax.ShapeDtypeStruct((n_rows, d), BF16),
        compiler_params=_cparams("arbitrary"),
        name="moe_experts",
    )(item_block, item_expert, item_lo, item_hi, xb, w1, w3, w2)


def _route(logits):
    t = logits.shape[0]
    g_logits = logits[:, :N_GROUPS]
    e_logits = logits[:, N_GROUPS:N_GROUPS + N_EXPERTS].reshape(t, N_GROUPS, EXPERTS_PER_GROUP)
    g_prob = jax.nn.softmax(g_logits, axis=-1)
    g_idx = jnp.argmax(g_logits, axis=-1)
    p_group = jnp.take_along_axis(g_prob, g_idx[:, None], axis=1)[:, 0]
    e_in_group = jnp.take_along_axis(e_logits, g_idx[:, None, None], axis=1)[:, 0]
    top_v, top_i = lax.top_k(e_in_group, TOP_K)
    weights = jax.nn.softmax(top_v, axis=-1) * p_group[:, None]
    expert_id = (g_idx[:, None] * EXPERTS_PER_GROUP + top_i).reshape(-1).astype(jnp.int32)
    n_assign = t * TOP_K
    assert n_assign % MOE_ROWS == 0
    n_blocks = n_assign // MOE_ROWS
    ar = jnp.arange(n_assign, dtype=jnp.int32)
    e_sorted, order = lax.sort((expert_id, ar), num_keys=1)
    _, slot = lax.sort((order, ar), num_keys=1)
    ends = jnp.sum(e_sorted[None, :] <= jnp.arange(N_EXPERTS, dtype=jnp.int32)[:, None], axis=1).astype(jnp.int32)
    starts = jnp.concatenate([jnp.zeros((1,), jnp.int32), ends[:-1]])
    cuts = jnp.sort(jnp.concatenate([jnp.arange(n_blocks, dtype=jnp.int32) * MOE_ROWS, starts]))
    item_lo = cuts
    item_hi = jnp.concatenate([cuts[1:], jnp.full((1,), n_assign, jnp.int32)])
    item_block = jnp.minimum(item_lo // MOE_ROWS, n_blocks - 1).astype(jnp.int32)
    item_expert = jnp.minimum(jnp.sum(item_lo[:, None] >= ends[None, :], axis=1), N_EXPERTS - 1).astype(jnp.int32)
    return order // TOP_K, (item_block, item_expert, item_lo, item_hi), slot.reshape(t, TOP_K), weights


def _combine_kernel(x_ref, ya_ref, yb_ref, wt_ref, mod_ref, nw_ref, o_ref, *, final):
    wt = wt_ref[...]
    y = wt[:, 0:1] * ya_ref[...].astype(F32) + wt[:, 1:2] * yb_ref[...].astype(F32)
    x = x_ref[...] + mod_ref[5:6, :] * y
    if final:
        x = x * lax.rsqrt(jnp.mean(x * x, axis=-1, keepdims=True) + NORM_EPS) * nw_ref[...]
    o_ref[...] = x


def _combine(x, ya, yb, wts, modtab, norm_w, n_lat_tiles, final, out_len):
    bsz, l, d = x.shape
    tok = pl.BlockSpec((None, ROW_TILE, d), lambda b, j: (b, j, 0))
    return pl.pallas_call(
        functools.partial(_combine_kernel, final=final),
        grid=(bsz, out_len // ROW_TILE),
        in_specs=[tok, tok, tok,
                  pl.BlockSpec((None, ROW_TILE, LANE), lambda b, j: (b, j, 0)),
                  pl.BlockSpec((None, None, SUBLANE, d), lambda b, j: (b, jnp.where(j >= n_lat_tiles, 1, 0), 0, 0)),
                  pl.BlockSpec((1, d), lambda b, j: (0, 0))],
        out_specs=tok,
        out_shape=jax.ShapeDtypeStruct((bsz, out_len, d), F32),
        compiler_params=_cparams("arbitrary", "arbitrary"),
        name="combine",
    )(x, ya, yb, wts, modtab, norm_w)


def _pad_lanes(v, n=LANE):
    v = v.reshape(1, -1).astype(F32)
    return jnp.pad(v, ((0, 0), (0, n - v.shape[1])))


def kernel(x, c, ctx, c_ctx, w_mod, b_mod, norm1_w, norm2_w, w_in, lru_conv_w, lru_conv_b, lru_wa, lru_ba,
           lru_wi, lru_bi, lru_lambda, gdn_conv_w, gdn_a_log, gdn_dt_bias, gdn_norm_w, ssd_conv_w, ssd_conv_b,
           ssd_a_log, ssd_dt_bias, ssd_d, ssd_norm_w, w_branch, w_out, router_group_w, router_group_b,
           router_expert_w, router_expert_b, expert_w1, expert_w3, expert_w2, final_norm_w):
    bsz, seq, d = x.shape
    n_ctx_tok = ctx.shape[1]
    depth = w_mod.shape[0]
    l = seq + n_ctx_tok
    rows = seq // GRID_W
    assert seq % LRU_CHUNK == 0 and n_ctx_tok % LRU_CHUNK == 0 and ROW_TILE == LRU_CHUNK
    assert bsz + 1 <= SUBLANE
    hw = GDN_HEADS * GDN_DK
    inner = SSD_HEADS * SSD_HEADDIM
    xbc_w = inner + 2 * SSD_GROUPS * SSD_STATE

    act = jnp.concatenate([_silu(c), _silu(c_ctx)[None, :],
                           jnp.zeros((SUBLANE - bsz - 1, d), F32)], axis=0)
    mod_all = _modulation(act, w_mod, b_mod).reshape(depth, SUBLANE, 6, d)
    pad2 = jnp.zeros((bsz, 2, d), F32)

    xs = jnp.concatenate([x, ctx], axis=1)
    nl_t, nc_t = seq // ROW_TILE, n_ctx_tok // ROW_TILE
    nl_s, nc_s = seq // SCAN_CHUNK, n_ctx_tok // SCAN_CHUNK

    o_lx, o_qkv, o_gz, o_gb, o_ga = 0, 2 * d, 2 * d + 3 * hw, 2 * d + 4 * hw, 2 * d + 4 * hw + 2 * GDN_HEADS
    o_sz = o_ga + 2 * GDN_HEADS
    o_xbc = o_sz + inner
    o_sdt = o_xbc + xbc_w
    o_gate = o_sdt + 2 * SSD_HEADS

    for i in range(depth):
        lat = mod_all[i, :bsz]
        cx = jnp.broadcast_to(mod_all[i, bsz][None], (bsz, 6, d))
        modtab = jnp.stack([jnp.concatenate([lat, pad2], axis=1), jnp.concatenate([cx, pad2], axis=1)], axis=1)

        wi_ = w_in[i]
        w_lru = wi_[:, o_lx:o_qkv].astype(BF16)
        zpad = jnp.zeros((d, LANE - 4 * GDN_HEADS), F32)
        w_gdn = jnp.concatenate([wi_[:, o_qkv:o_gb], wi_[:, o_ga:o_sz], wi_[:, o_gb:o_ga], zpad], axis=1).astype(BF16)
        zpad2 = jnp.zeros((d, LANE - 2 * SSD_HEADS), F32)
        w_ssd = jnp.concatenate([wi_[:, o_sz:o_gate], zpad2], axis=1).astype(BF16)
        w_gate = wi_[:, o_gate:].astype(BF16)
        n1 = norm1_w[i].reshape(1, d)

        gate = _norm_matmul(xs, modtab, n1, w_gate, 0, 1, nl_t, "in_gate")

        lru_u, lru_y = _in_lru(xs, modtab, n1, w_lru, lru_conv_w[i], lru_conv_b[i].reshape(1, d), nl_t, nc_t)
        h_f = _lru_scan(lru_u, lru_y, lru_wa[i, 0].astype(BF16), lru_ba[i, 0].reshape(1, d),
                        lru_wi[i, 0].astype(BF16), lru_bi[i, 0].reshape(1, d), lru_lambda[i, 0].reshape(1, d),
                        None, nl_t, nc_t, False)
        lru_out = _lru_scan(lru_u, lru_y, lru_wa[i, 1].astype(BF16), lru_ba[i, 1].reshape(1, d),
                            lru_wi[i, 1].astype(BF16), lru_bi[i, 1].reshape(1, d), lru_lambda[i, 1].reshape(1, d),
                            h_f, nl_t, nc_t, True)

        q, k, v, gz, gcol, grow = _in_gdn(xs, modtab, n1, w_gdn, gdn_conv_w[i], _pad_lanes(gdn_a_log[i]),
                                          _pad_lanes(gdn_dt_bias[i]), nl_t, nc_t)
        o_f = _gdn_scan(q, k, v, gcol, grow, None, None, None, nl_s, nc_s, False)
        gdn_out = _gdn_scan(q, k, v, gcol, grow, o_f, gz, gdn_norm_w[i].reshape(1, GDN_DK), nl_s, nc_s, True)

        x_cm = xs[:, :seq].reshape(bsz, rows, GRID_W, d).swapaxes(1, 2).reshape(bsz, seq, d)
        xs_scan = jnp.concatenate([x_cm, xs[:, seq:]], axis=1)
        sz, su, sdt = _in_ssd(xs_scan, modtab, n1, w_ssd, ssd_conv_w[i], ssd_conv_b[i].reshape(1, xbc_w),
                              nl_t, nc_t)
        dtb_vec = _pad_lanes(ssd_dt_bias[i])
        a_vec = _pad_lanes(-jnp.exp(ssd_a_log[i].astype(F32)))
        dsk = [jnp.repeat(ssd_d[i, dd], SSD_HEADDIM).reshape(1, inner) for dd in range(2)]
        y_f = _ssd_scan(None, su, sdt, dtb_vec, a_vec, dsk[0], None, None, nl_s, nc_s, False)
        ssd_scan_out = _ssd_scan(sz, su, sdt, dtb_vec, a_vec, dsk[1], y_f,
                                 ssd_norm_w[i].reshape(1, inner), nl_s, nc_s, True)
        ssd_lat = ssd_scan_out[:, :seq].reshape(bsz, GRID_W, rows, inner).swapaxes(1, 2).reshape(bsz, seq, inner)
        ssd_out = jnp.concatenate([ssd_lat, ssd_scan_out[:, seq:]], axis=1)

        w_r = jnp.concatenate([router_group_w[i], router_expert_w[i],
                               jnp.zeros((d, LANE - N_GROUPS - N_EXPERTS), F32)], axis=1).astype(BF16)
        b_r = _pad_lanes(jnp.concatenate([router_group_b[i], router_expert_b[i]]))
        x_mid, h2, logits = _merge(xs, lru_out, gdn_out, ssd_out, gate, modtab, norm2_w[i].reshape(1, d),
                                   w_branch[i].astype(BF16), w_out[i].astype(BF16), w_r, b_r, nl_t)

        t = bsz * l
        tok_sorted, items, slot, weights = _route(logits.reshape(t, LANE))
        xb = h2.reshape(t, d)[tok_sorted]
        yb = _moe_experts(xb, *items, expert_w1[i].astype(BF16), expert_w3[i].astype(BF16),
                          expert_w2[i].astype(BF16))
        y0 = yb[slot[:, 0]].reshape(bsz, l, d)
        y1 = yb[slot[:, 1]].reshape(bsz, l, d)
        wts = jnp.pad(weights, ((0, 0), (0, LANE - TOP_K))).reshape(bsz, l, LANE)
        last = i == depth - 1
        xs = _combine(x_mid, y0, y1, wts, modtab, final_norm_w.reshape(1, d), nl_t, last, seq if last else l)

    return xs
```

```python
import functools

import jax
import jax.numpy as jnp
from jax import lax
from jax.experimental import pallas as pl
from jax.experimental.pallas import tpu as pltpu

GRID_W = 64
CONV_K = 4
NORM_EPS = 1e-6
LRU_BLOCKS = 8
LRU_C = 8.0
GDN_HEADS = 8
GDN_DK = 128
SSD_HEADS = 16
SSD_HEADDIM = 64
SSD_GROUPS = 2
SSD_STATE = 128
N_GROUPS = 4
EXPERTS_PER_GROUP = 8
N_EXPERTS = N_GROUPS * EXPERTS_PER_GROUP
TOP_K = 2

LANE = 128
SUBLANE = 8
ROW_TILE = 256
LRU_CHUNK = 256
SCAN_CHUNK = 128
MOE_ROWS = 256
VMEM_LIMIT = 56 * 1024 * 1024

F32 = jnp.float32
BF16 = jnp.bfloat16
NEG_BIG = -1e30


def _cparams(*sem):
    return pltpu.CompilerParams(dimension_semantics=sem, vmem_limit_bytes=VMEM_LIMIT)


def _dot(a, b):
    return jnp.dot(a, b, preferred_element_type=F32)


def _dot_nt(a, b):
    return lax.dot_general(a, b, (((1,), (1,)), ((), ())), preferred_element_type=F32)


def _dot_tn(a, b):
    return lax.dot_general(a, b, (((0,), (0,)), ((), ())), preferred_element_type=F32)


def _split3(x):
    hi = x.astype(BF16)
    r1 = x - hi.astype(F32)
    mid = r1.astype(BF16)
    lo = (r1 - mid.astype(F32)).astype(BF16)
    return hi, mid, lo


def _exact_left(m_bf, x):
    hi, mid, lo = _split3(x)
    return _dot(jnp.concatenate([m_bf, m_bf, m_bf], axis=1), jnp.concatenate([hi, mid, lo], axis=0))


def _exact_right(x, m_bf):
    hi, mid, lo = _split3(x)
    return _dot(jnp.concatenate([hi, mid, lo], axis=1), jnp.concatenate([m_bf, m_bf, m_bf], axis=0))


def _exact_transpose(x, eye_bf):
    hi, mid, lo = _split3(x)
    return _dot_nt(jnp.concatenate([eye_bf, eye_bf, eye_bf], axis=1), jnp.concatenate([hi, mid, lo], axis=1))


def _softplus(x):
    return jnp.maximum(x, 0.0) + jnp.log(1.0 + jnp.exp(-jnp.abs(x)))


def _silu(x):
    return x * jax.nn.sigmoid(x)


def _iota(shape, dim):
    return lax.broadcasted_iota(jnp.int32, shape, dim)


def _fwd_chunk(i, n_lat, n_ctx):
    return jnp.where(i < n_ctx, n_lat + i, i - n_ctx)


def _chunk_of_step(i, n_lat, n_ctx, reverse):
    return (n_lat + n_ctx - 1 - i) if reverse else _fwd_chunk(i, n_lat, n_ctx)


def _segment_edges(c, n_lat, n_ctx):
    first = jnp.logical_or(c == 0, c == n_lat)
    last = jnp.logical_or(c == n_lat - 1, c == n_lat + n_ctx - 1)
    return first, last


def _normed_rows(x, x_prev, x_next, mod_ref, nw_ref):
    shift, scale = mod_ref[0:1, :], mod_ref[1:2, :]
    h_main = _mod_norm(x, nw_ref[...], shift, scale)
    h_prev = _mod_norm(x_prev, nw_ref[...], shift, scale)
    h_next = _mod_norm(x_next, nw_ref[...], shift, scale)
    return h_main.astype(BF16), jnp.concatenate([h_prev, h_main, h_next], axis=0).astype(BF16)


def _proj_conv(xe_ref, h_all, w_ref, cw_ref, col0, width, first, last, rows, col_step=512):
    base = SUBLANE - CONV_K // 2
    for c0 in range(0, width, col_step):
        c1 = min(c0 + col_step, width)
        y = _dot(h_all, w_ref[:, col0 + c0:col0 + c1])
        xe_ref[0:SUBLANE, c0:c1] = jnp.where(first, 0.0, y[0:SUBLANE])
        xe_ref[SUBLANE:SUBLANE + rows, c0:c1] = y[SUBLANE:SUBLANE + rows]
        xe_ref[SUBLANE + rows:2 * SUBLANE + rows, c0:c1] = jnp.where(last, 0.0, y[SUBLANE + rows:])
        u = cw_ref[0:1, c0:c1] * xe_ref[base:base + rows, c0:c1]
        for j in range(1, CONV_K):
            u = u + cw_ref[j:j + 1, c0:c1] * xe_ref[base + j:base + j + rows, c0:c1]
        yield c0, c1, u


def _halo_specs(width, rows, n_rows_total, chunk_fn, col_block=0):
    per = rows // SUBLANE
    last_tile = n_rows_total // SUBLANE - 1
    prev = pl.BlockSpec((None, SUBLANE, width),
                        lambda b, i: (b, jnp.maximum(chunk_fn(i) * per - 1, 0), col_block))
    nxt = pl.BlockSpec((None, SUBLANE, width),
                       lambda b, i: (b, jnp.minimum((chunk_fn(i) + 1) * per, last_tile), col_block))
    return prev, nxt


def _mod_kernel(a_ref, w_ref, b_ref, o_ref):
    o_ref[...] = _dot(a_ref[...].astype(BF16), w_ref[...].astype(BF16)) + b_ref[...]


def _modulation(act, w_mod, b_mod):
    depth, d, n = w_mod.shape
    tn = 1536
    return pl.pallas_call(
        _mod_kernel,
        grid=(depth, n // tn),
        in_specs=[pl.BlockSpec((SUBLANE, d), lambda l, j: (0, 0)),
                  pl.BlockSpec((None, d, tn), lambda l, j: (l, 0, j)),
                  pl.BlockSpec((None, 1, tn), lambda l, j: (l, 0, j))],
        out_specs=pl.BlockSpec((None, SUBLANE, tn), lambda l, j: (l, 0, j)),
        out_shape=jax.ShapeDtypeStruct((depth, SUBLANE, n), F32),
        compiler_params=_cparams("arbitrary", "arbitrary"),
        name="modulation",
    )(act, w_mod, b_mod.reshape(depth, 1, n))


def _mod_norm(x, nw, shift, scale):
    y = x * lax.rsqrt(jnp.mean(x * x, axis=-1, keepdims=True) + NORM_EPS)
    return (y * nw) * (1.0 + scale) + shift


def _norm_matmul_kernel(x_ref, mod_ref, nw_ref, w_ref, o_ref, *, shift_row, scale_row, col_step):
    h = _mod_norm(x_ref[...], nw_ref[...], mod_ref[shift_row:shift_row + 1, :],
                  mod_ref[scale_row:scale_row + 1, :]).astype(BF16)
    n = w_ref.shape[1]
    for c0 in range(0, n, col_step):
        c1 = min(c0 + col_step, n)
        o_ref[:, c0:c1] = _dot(h, w_ref[:, c0:c1]).astype(o_ref.dtype)


def _norm_matmul(x, modtab, norm_w, w, shift_row, scale_row, n_lat_tiles, name):
    bsz, l, d = x.shape
    n = w.shape[1]
    kern = functools.partial(_norm_matmul_kernel, shift_row=shift_row, scale_row=scale_row, col_step=512)
    return pl.pallas_call(
        kern,
        grid=(bsz, l // ROW_TILE),
        in_specs=[pl.BlockSpec((None, ROW_TILE, d), lambda b, j: (b, j, 0)),
                  pl.BlockSpec((None, None, SUBLANE, d), lambda b, j: (b, jnp.where(j >= n_lat_tiles, 1, 0), 0, 0)),
                  pl.BlockSpec((1, d), lambda b, j: (0, 0)),
                  pl.BlockSpec((d, n), lambda b, j: (0, 0))],
        out_specs=pl.BlockSpec((None, ROW_TILE, n), lambda b, j: (b, j, 0)),
        out_shape=jax.ShapeDtypeStruct((bsz, l, n), BF16),
        compiler_params=_cparams("arbitrary", "arbitrary"),
        name=name,
    )(x, modtab, norm_w, w)


def _row_tile_specs(d, l, n_lat_tiles):
    ident = lambda j: j
    main = pl.BlockSpec((None, ROW_TILE, d), lambda b, j: (b, j, 0))
    prev, nxt = _halo_specs(d, ROW_TILE, l, ident)
    mod = pl.BlockSpec((None, None, SUBLANE, d), lambda b, j: (b, jnp.where(j >= n_lat_tiles, 1, 0), 0, 0))
    return [main, prev, nxt, mod, pl.BlockSpec((1, d), lambda b, j: (0, 0))]


def _in_lru_kernel(x_ref, xp_ref, xn_ref, mod_ref, nw_ref, w_ref, cw_ref, cb_ref, u_ref, y_ref, xe_ref,
                   *, n_lat, n_ctx):
    first, last = _segment_edges(pl.program_id(1), n_lat, n_ctx)
    h_main, h_all = _normed_rows(x_ref[...], xp_ref[...], xn_ref[...], mod_ref, nw_ref)
    w = u_ref.shape[1]
    for c0, c1, u in _proj_conv(xe_ref, h_all, w_ref, cw_ref, 0, w, first, last, ROW_TILE):
        u_ref[:, c0:c1] = u + cb_ref[:, c0:c1]
    for c0 in range(0, w, 512):
        y_ref[:, c0:c0 + 512] = _dot(h_main, w_ref[:, w + c0:w + c0 + 512])


def _in_lru(x, modtab, norm_w, w, conv_w, conv_b, n_lat, n_ctx):
    bsz, l, d = x.shape
    wd = w.shape[1] // 2
    tok = pl.BlockSpec((None, ROW_TILE, wd), lambda b, j: (b, j, 0))
    return pl.pallas_call(
        functools.partial(_in_lru_kernel, n_lat=n_lat, n_ctx=n_ctx),
        grid=(bsz, l // ROW_TILE),
        in_specs=_row_tile_specs(d, l, n_lat) + [pl.BlockSpec((d, 2 * wd), lambda b, j: (0, 0)),
                                                 pl.BlockSpec((CONV_K, wd), lambda b, j: (0, 0)),
                                                 pl.BlockSpec((1, wd), lambda b, j: (0, 0))],
        out_specs=[tok, tok],
        out_shape=[jax.ShapeDtypeStruct((bsz, l, wd), F32)] * 2,
        scratch_shapes=[pltpu.VMEM((ROW_TILE + 2 * SUBLANE, wd), F32)],
        compiler_params=_cparams("arbitrary", "arbitrary"),
        name="in_lru",
    )(x, x, x, modtab, norm_w, w, conv_w, conv_b)


def _lru_kernel(*refs, reverse, final):
    if final:
        u_ref, wa_ref, ba_ref, wi_ref, bi_ref, lam_ref, hf_ref, y_ref, o_ref, carry_ref = refs
    else:
        u_ref, wa_ref, ba_ref, wi_ref, bi_ref, lam_ref, o_ref, carry_ref = refs
    rows = LRU_CHUNK
    i = pl.program_id(1)

    @pl.when(i == 0)
    def _():
        carry_ref[...] = jnp.zeros_like(carry_ref)

    u = u_ref[...]
    ub = u.astype(BF16)
    bw = u.shape[1] // LRU_BLOCKS
    pre_r = jnp.concatenate([_dot(ub[:, n * bw:(n + 1) * bw], wa_ref[n]) for n in range(LRU_BLOCKS)], axis=1)
    pre_i = jnp.concatenate([_dot(ub[:, n * bw:(n + 1) * bw], wi_ref[n]) for n in range(LRU_BLOCKS)], axis=1)
    r = jax.nn.sigmoid(pre_r + ba_ref[...])
    gi = jax.nn.sigmoid(pre_i + bi_ref[...])
    a = jnp.exp(-LRU_C * r * _softplus(-lam_ref[...]))
    h = jnp.sqrt(1.0 - a * a) * gi * u

    srow = _iota((SUBLANE, a.shape[1]), 0)
    n_tiles = rows // SUBLANE
    carry = carry_ref[...]
    done = {}
    for k in (range(n_tiles - 1, -1, -1) if reverse else range(n_tiles)):
        sl = slice(k * SUBLANE, (k + 1) * SUBLANE)
        at, ht = a[sl], h[sl]
        for s in (1, 2, 4):
            if reverse:
                keep = srow < SUBLANE - s
                sh = SUBLANE - s
            else:
                keep = srow >= s
                sh = s
            h_s = jnp.where(keep, pltpu.roll(ht, sh, axis=0), 0.0)
            a_s = jnp.where(keep, pltpu.roll(at, sh, axis=0), 1.0)
            ht = ht + at * h_s
            at = at * a_s
        hk = ht + at * carry
        carry = hk[0:1, :] if reverse else hk[SUBLANE - 1:SUBLANE, :]
        if final:
            done[k] = (hk + hf_ref[sl, :]) * jax.nn.gelu(y_ref[sl, :], approximate=True)
            if (k ^ 1) in done:
                lo = k & ~1
                pair = jnp.concatenate([done.pop(lo), done.pop(lo + 1)], axis=0)
                o_ref[lo * SUBLANE:(lo + 2) * SUBLANE, :] = pair.astype(o_ref.dtype)
        else:
            o_ref[sl, :] = hk
    carry_ref[...] = carry


def _lru_scan(u, y, wa, ba, wi, bi, lam, hf, n_lat, n_ctx, reverse):
    bsz, l, w = u.shape
    final = hf is not None
    chunk_fn = functools.partial(_chunk_of_step, n_lat=n_lat, n_ctx=n_ctx, reverse=reverse)
    main = pl.BlockSpec((None, LRU_CHUNK, w), lambda b, i: (b, chunk_fn(i), 0))
    vec = pl.BlockSpec((1, w), lambda b, i: (0, 0))
    blk = pl.BlockSpec((LRU_BLOCKS, w // LRU_BLOCKS, w // LRU_BLOCKS), lambda b, i: (0, 0, 0))
    in_specs = [main, blk, vec, blk, vec, vec]
    args = [u, wa, ba, wi, bi, lam]
    if final:
        in_specs += [main, main]
        args += [hf, y]
    return pl.pallas_call(
        functools.partial(_lru_kernel, reverse=reverse, final=final),
        grid=(bsz, l // LRU_CHUNK),
        in_specs=in_specs,
        out_specs=main,
        out_shape=jax.ShapeDtypeStruct((bsz, l, w), BF16 if final else F32),
        scratch_shapes=[pltpu.VMEM((1, w), F32)],
        compiler_params=_cparams("arbitrary", "arbitrary"),
        name="lru_rev" if reverse else "lru_fwd",
    )(*args)


def _in_gdn_kernel(x_ref, xp_ref, xn_ref, mod_ref, nw_ref, w_ref, cw_ref, alog_ref, dtb_ref,
                   q_ref, k_ref, v_ref, z_ref, gcol_ref, grow_ref, xe_ref, *, n_lat, n_ctx):
    rows = ROW_TILE
    first, last = _segment_edges(pl.program_id(1), n_lat, n_ctx)
    h_main, h_all = _normed_rows(x_ref[...], xp_ref[...], xn_ref[...], mod_ref, nw_ref)
    hw = GDN_HEADS * GDN_DK
    for c0, c1, u in _proj_conv(xe_ref, h_all, w_ref, cw_ref, 0, 3 * hw, first, last, rows):
        u = _silu(u)
        for h0 in range(c0, c1, GDN_DK):
            uh = u[:, h0 - c0:h0 - c0 + GDN_DK]
            if h0 < 2 * hw:
                uh = uh * lax.rsqrt(jnp.sum(uh * uh, axis=-1, keepdims=True) + NORM_EPS)
            if h0 < hw:
                q_ref[:, h0:h0 + GDN_DK] = uh * (GDN_DK ** -0.5)
            elif h0 < 2 * hw:
                k_ref[:, h0 - hw:h0 - hw + GDN_DK] = uh
            else:
                v_ref[:, h0 - 2 * hw:h0 - 2 * hw + GDN_DK] = uh
    for c0 in range(0, hw, 512):
        z_ref[:, c0:c0 + 512] = _dot(h_main, w_ref[:, 3 * hw + c0:3 * hw + c0 + 512])

    sm = _dot(h_main, w_ref[:, 4 * hw:4 * hw + LANE])
    g = -jnp.exp(alog_ref[...]) * _softplus(sm + dtb_ref[...])
    beta = jax.nn.sigmoid(sm)
    ri = _iota((rows, rows), 0)
    ci = _iota((rows, rows), 1)
    same = (ri // SCAN_CHUNK) == (ci // SCAN_CHUNK)
    tri_f = jnp.where(jnp.logical_and(same, ci <= ri), 1.0, 0.0).astype(BF16)
    tri_r = jnp.where(jnp.logical_and(same, ci >= ri), 1.0, 0.0).astype(BF16)
    gcs_f = _exact_left(tri_f, g)
    gcs_r = _exact_left(tri_r, g)
    lane = _iota((rows, LANE), 1)
    gcol = jnp.where(lane < GDN_HEADS, gcs_f,
                     jnp.where(lane < 2 * GDN_HEADS, gcs_r, jnp.where(lane < 4 * GDN_HEADS, beta, 0.0)))
    gcol_ref[...] = gcol
    eye = jnp.where(_iota((LANE, LANE), 0) == _iota((LANE, LANE), 1), 1.0, 0.0).astype(BF16)
    grow_ref[...] = _exact_transpose(gcol, eye)[0:4 * GDN_HEADS, :]


def _in_gdn(x, modtab, norm_w, w, conv_w, alog_vec, dtb_vec, n_lat, n_ctx):
    bsz, l, d = x.shape
    hw = GDN_HEADS * GDN_DK
    tok = pl.BlockSpec((None, ROW_TILE, hw), lambda b, i: (b, i, 0))
    return pl.pallas_call(
        functools.partial(_in_gdn_kernel, n_lat=n_lat, n_ctx=n_ctx),
        grid=(bsz, l // ROW_TILE),
        in_specs=_row_tile_specs(d, l, n_lat) + [pl.BlockSpec((d, w.shape[1]), lambda b, i: (0, 0)),
                                                 pl.BlockSpec((CONV_K, 3 * hw), lambda b, i: (0, 0)),
                                                 pl.BlockSpec((1, LANE), lambda b, i: (0, 0)),
                                                 pl.BlockSpec((1, LANE), lambda b, i: (0, 0))],
        out_specs=[tok, tok, tok, tok,
                   pl.BlockSpec((None, ROW_TILE, LANE), lambda b, i: (b, i, 0)),
                   pl.BlockSpec((None, 4 * GDN_HEADS, ROW_TILE), lambda b, i: (b, 0, i))],
        out_shape=[jax.ShapeDtypeStruct((bsz, l, hw), F32)] * 4
        + [jax.ShapeDtypeStruct((bsz, l, LANE), F32), jax.ShapeDtypeStruct((bsz, 4 * GDN_HEADS, l), F32)],
        scratch_shapes=[pltpu.VMEM((ROW_TILE + 2 * SUBLANE, 3 * hw), F32)],
        compiler_params=_cparams("arbitrary", "arbitrary"),
        name="in_gdn",
    )(x, x, x, modtab, norm_w, w, conv_w, alog_vec, dtb_vec)


def _pair_blockdiag(x):
    c = x.shape[0]
    z = jnp.zeros((c, c), x.dtype)
    return jnp.concatenate([jnp.concatenate([x[:, :c], z], axis=1),
                            jnp.concatenate([z, x[:, c:]], axis=1)], axis=0)


def _gdn_scan_kernel(*refs, reverse, final, n_lat, n_ctx):
    if final:
        q_ref, k_ref, v_ref, gcol_ref, grow_ref, of_ref, z_ref, nw_ref, o_ref, s_ref = refs
    else:
        q_ref, k_ref, v_ref, gcol_ref, grow_ref, o_ref, s_ref = refs
    rows = SCAN_CHUNK
    n_pairs = GDN_HEADS // 2
    pw = 2 * GDN_DK
    chains = [(bb, p) for bb in range(q_ref.shape[0]) for p in range(n_pairs)]
    n_chains = len(chains)
    i = pl.program_id(1)

    @pl.when(i == 0)
    def _():
        s_ref[...] = jnp.zeros_like(s_ref)

    ri = _iota((rows, rows), 0)
    ci = _iota((rows, rows), 1)
    incl = (ri <= ci) if reverse else (ri >= ci)
    ri2 = _iota((rows, pw), 0)
    ci2 = jnp.bitwise_and(_iota((rows, pw), 1), rows - 1)
    strict2 = (ri2 < ci2) if reverse else (ri2 > ci2)
    d = 1 if reverse else 0
    edge = 0 if reverse else rows - 1

    def pair_cols(gcol, lane0):
        return jnp.concatenate([jnp.broadcast_to(gcol[:, lane0 + j:lane0 + j + 1], (rows, GDN_DK))
                                for j in range(2)], axis=1)

    qs, ks, kbs, egs, gcs, decs, rhs, sts = [], [], [], [], [], [], [], []
    for c, (bb, p) in enumerate(chains):
        sl = slice(p * pw, (p + 1) * pw)
        li = d * GDN_HEADS + 2 * p
        gcol = gcol_ref[bb]
        grow = grow_ref[bb]
        q2, k2, v2 = q_ref[bb, :, sl], k_ref[bb, :, sl], v_ref[bb, :, sl]
        gc2 = pair_cols(gcol, li)
        beta2 = pair_cols(gcol, 2 * GDN_HEADS + li)
        eg2 = jnp.exp(gc2)
        kb2 = k2 * beta2
        dec2 = jnp.concatenate(
            [jnp.exp(jnp.where(incl, gcol[:, li + j:li + j + 1] - grow[li + j:li + j + 1, :], NEG_BIG))
             for j in range(2)], axis=1)
        qs.append(q2)
        ks.append(k2)
        kbs.append(kb2)
        egs.append(eg2)
        gcs.append(gc2)
        decs.append(dec2)
        rhs.append([jnp.concatenate([v2[:, j * GDN_DK:(j + 1) * GDN_DK] * beta2[:, j * GDN_DK:(j + 1) * GDN_DK],
                                     kb2[:, j * GDN_DK:(j + 1) * GDN_DK] * eg2[:, j * GDN_DK:(j + 1) * GDN_DK]],
                                    axis=1).astype(BF16) for j in range(2)])
        sts.append(s_ref[c])

    a_mats, attns = [], []
    for p in range(n_chains):
        kq = _dot_nt(jnp.concatenate([kbs[p], qs[p]], axis=0).astype(BF16), _pair_blockdiag(ks[p].astype(BF16)))
        a_mats.append(jnp.where(strict2, kq[:rows] * decs[p], 0.0))
        attns.append((kq[rows:] * decs[p]).astype(BF16))

    half = rows // 2
    rq = _iota((half, pw), 0)
    lq = _iota((half, pw), 1)
    cq = jnp.bitwise_and(lq, half - 1)
    low_half = jnp.bitwise_and(lq, rows - 1) < half
    blk_q = lq // half

    def quad_blockdiag(y):
        return jnp.concatenate([jnp.where(blk_q == b, y, jnp.zeros_like(y)) for b in range(pw // half)], axis=0)

    aqs = [jnp.where(low_half, a_mats[p][:half], a_mats[p][half:]) for p in range(n_chains)]
    tqs = [jnp.where(rq == cq, 1.0, 0.0) - jnp.where((rq // 2) == (cq // 2), aqs[p], 0.0) for p in range(n_chains)]
    s = 2
    while s < half:
        off = jnp.logical_and((rq // (2 * s)) == (cq // (2 * s)), (rq // s) != (cq // s))
        xs = [_dot(jnp.where(off, aqs[p], 0.0).astype(BF16), quad_blockdiag(tqs[p].astype(BF16)))
              for p in range(n_chains)]
        tqs = [tqs[p] - _dot(tqs[p].astype(BF16), quad_blockdiag(xs[p].astype(BF16))) for p in range(n_chains)]
        s *= 2
    ts = [jnp.concatenate([jnp.where(low_half, tqs[p], 0.0), jnp.where(low_half, 0.0, tqs[p])], axis=0)
          for p in range(n_chains)]
    off = (ri2 // half) != (ci2 // half)
    xs = [_dot(jnp.where(off, a_mats[p], 0.0).astype(BF16), _pair_blockdiag(ts[p].astype(BF16)))
          for p in range(n_chains)]
    ts = [ts[p] - _dot(ts[p].astype(BF16), _pair_blockdiag(xs[p].astype(BF16))) for p in range(n_chains)]

    us, ws = [], []
    for p in range(n_chains):
        tb = ts[p].astype(BF16)
        sol = [_dot(tb[:, j * GDN_DK:(j + 1) * GDN_DK], rhs[p][j]) for j in range(2)]
        us.append(jnp.concatenate([sol[0][:, :GDN_DK], sol[1][:, :GDN_DK]], axis=1))
        ws.append(jnp.concatenate([sol[0][:, GDN_DK:], sol[1][:, GDN_DK:]], axis=1))

    vns, outs = [], []
    for p in range(n_chains):
        wq = jnp.concatenate([ws[p], qs[p] * egs[p]], axis=0).astype(BF16)
        ws_qs = _dot(wq, _pair_blockdiag(sts[p].astype(BF16)))
        vns.append((us[p] - ws_qs[:rows]).astype(BF16))
        outs.append(ws_qs[rows:])
    for c, (bb, p) in enumerate(chains):
        o2 = outs[c] + _dot(attns[c], _pair_blockdiag(vns[c]))
        g_end = gcs[c][edge:edge + 1, :]
        k_dec = (ks[c] * jnp.exp(g_end - gcs[c])).astype(BF16)
        full = _dot_tn(k_dec, vns[c])
        upd = jnp.concatenate([full[:GDN_DK, :GDN_DK], full[GDN_DK:, GDN_DK:]], axis=1)
        s_ref[c] = sts[c] * jnp.exp(g_end) + upd
        sl = slice(p * pw, (p + 1) * pw)
        if final:
            o2 = o2 + of_ref[bb, :, sl]
            ys = []
            for j in range(2):
                oj = o2[:, j * GDN_DK:(j + 1) * GDN_DK]
                ys.append(oj * lax.rsqrt(jnp.mean(oj * oj, axis=-1, keepdims=True) + NORM_EPS) * nw_ref[...])
            o_ref[bb, :, sl] = (jnp.concatenate(ys, axis=1) * _silu(z_ref[bb, :, sl])).astype(o_ref.dtype)
        else:
            o_ref[bb, :, sl] = o2


def _gdn_scan(q, k, v, gcol, grow, o_fwd, z, norm_w, n_lat, n_ctx, reverse):
    bsz, l, hw = q.shape
    final = o_fwd is not None
    chunk_fn = functools.partial(_chunk_of_step, n_lat=n_lat, n_ctx=n_ctx, reverse=reverse)
    nb = 2 if bsz % 2 == 0 else 1
    tok = pl.BlockSpec((nb, SCAN_CHUNK, hw), lambda b, i: (b, chunk_fn(i), 0))
    in_specs = [tok, tok, tok,
                pl.BlockSpec((nb, SCAN_CHUNK, LANE), lambda b, i: (b, chunk_fn(i), 0)),
                pl.BlockSpec((nb, 4 * GDN_HEADS, SCAN_CHUNK), lambda b, i: (b, 0, chunk_fn(i)))]
    args = [q, k, v, gcol, grow]
    if final:
        in_specs += [tok, tok, pl.BlockSpec((1, GDN_DK), lambda b, i: (0, 0))]
        args += [o_fwd, z, norm_w]
    return pl.pallas_call(
        functools.partial(_gdn_scan_kernel, reverse=reverse, final=final, n_lat=n_lat, n_ctx=n_ctx),
        grid=(bsz // nb, l // SCAN_CHUNK),
        in_specs=in_specs,
        out_specs=tok,
        out_shape=jax.ShapeDtypeStruct((bsz, l, hw), BF16 if final else F32),
        scratch_shapes=[pltpu.VMEM((nb * GDN_HEADS // 2, GDN_DK, 2 * GDN_DK), F32)],
        compiler_params=_cparams("arbitrary", "arbitrary"),
        name="gdn_rev" if reverse else "gdn_fwd",
    )(*args)


def _in_ssd_kernel(xl_ref, xlp_ref, xln_ref, xc_ref, xcp_ref, xcn_ref, mod_ref, nw_ref, w_ref, cw_ref, cb_ref,
                   z_ref, u_ref, dt_ref, xe_ref, *, n_lat, n_ctx):
    j = pl.program_id(1)
    first, last = _segment_edges(j, n_lat, n_ctx)
    is_ctx = j >= n_lat
    pick = lambda c_ref, l_ref: jnp.where(is_ctx, c_ref[...], l_ref[...])
    h_main, h_all = _normed_rows(pick(xc_ref, xl_ref), pick(xcp_ref, xlp_ref), pick(xcn_ref, xln_ref),
                                 mod_ref, nw_ref)
    inner = z_ref.shape[1]
    xbc_w = u_ref.shape[1]
    for c0 in range(0, inner, 512):
        z_ref[:, c0:c0 + 512] = _dot(h_main, w_ref[:, c0:c0 + 512])
    for c0, c1, u in _proj_conv(xe_ref, h_all, w_ref, cw_ref, inner, xbc_w, first, last, ROW_TILE):
        u_ref[:, c0:c1] = _silu(u + cb_ref[:, c0:c1])
    dt_ref[...] = _dot(h_main, w_ref[:, inner + xbc_w:inner + xbc_w + LANE])


def _in_ssd(x_lat, x, modtab, norm_w, w, conv_w, conv_b, n_lat, n_ctx):
    bsz, l, d = x.shape
    inner = SSD_HEADS * SSD_HEADDIM
    xbc_w = inner + 2 * SSD_GROUPS * SSD_STATE
    tok = lambda n: pl.BlockSpec((None, ROW_TILE, n), lambda b, j: (b, j, 0))
    lat_tile = lambda j: jnp.minimum(j, n_lat - 1)
    ctx_tile = lambda j: jnp.maximum(j, n_lat)
    lat_specs = [pl.BlockSpec((None, ROW_TILE, d), lambda b, j: (b, lat_tile(j), 0)),
                 *_halo_specs(d, ROW_TILE, x_lat.shape[1], lat_tile)]
    ctx_specs = [pl.BlockSpec((None, ROW_TILE, d), lambda b, j: (b, ctx_tile(j), 0)),
                 *_halo_specs(d, ROW_TILE, l, ctx_tile)]
    return pl.pallas_call(
        functools.partial(_in_ssd_kernel, n_lat=n_lat, n_ctx=n_ctx),
        grid=(bsz, l // ROW_TILE),
        in_specs=lat_specs + ctx_specs + _row_tile_specs(d, l, n_lat)[3:] + [
            pl.BlockSpec((d, w.shape[1]), lambda b, j: (0, 0)),
            pl.BlockSpec((CONV_K, xbc_w), lambda b, j: (0, 0)),
            pl.BlockSpec((1, xbc_w), lambda b, j: (0, 0))],
        out_specs=[tok(inner), tok(xbc_w), tok(LANE)],
        out_shape=[jax.ShapeDtypeStruct((bsz, l, inner), F32), jax.ShapeDtypeStruct((bsz, l, xbc_w), F32),
                   jax.ShapeDtypeStruct((bsz, l, LANE), F32)],
        scratch_shapes=[pltpu.VMEM((ROW_TILE + 2 * SUBLANE, xbc_w), F32)],
        compiler_params=_cparams("arbitrary", "arbitrary"),
        name="in_ssd",
    )(x_lat, x_lat, x_lat, x, x, x, modtab, norm_w, w, conv_w, conv_b)


def _ssd_scan_kernel(*refs, reverse, final):
    if final:
        z_ref, u_ref, dt_ref, dtb_ref, a_ref, dsk_ref, yf_ref, nw_ref, o_ref, st_ref = refs
    else:
        u_ref, dt_ref, dtb_ref, a_ref, dsk_ref, o_ref, st_ref = refs
    rows = SCAN_CHUNK
    inner = SSD_HEADS * SSD_HEADDIM
    gw = inner // SSD_GROUPS
    i = pl.program_id(1)

    @pl.when(i == 0)
    def _():
        st_ref[...] = jnp.zeros_like(st_ref)

    u = u_ref[...]
    xs = u[:, :inner]
    dt = _softplus(dt_ref[...] + dtb_ref[...])
    adt = a_ref[...] * dt

    ri = _iota((rows, rows), 0)
    ci = _iota((rows, rows), 1)
    incl = (ri <= ci) if reverse else (ri >= ci)
    tri = jnp.where(incl, 1.0, 0.0).astype(BF16)
    eye = jnp.where(ri == ci, 1.0, 0.0).astype(BF16)
    acs = _exact_left(tri, adt)
    acs_t = _exact_transpose(acs, eye)
    d = 1 if reverse else 0
    edge = 0 if reverse else rows - 1
    er = _iota((LANE, inner), 0)
    ec = _iota((LANE, inner), 1)
    expand = jnp.where(er == d * SSD_HEADS + ec // SSD_HEADDIM, 1.0, 0.0).astype(BF16)
    dt_e = _exact_right(dt, expand)
    ea_e = _exact_right(jnp.exp(acs), expand)
    dc_e = _exact_right(jnp.exp(acs[edge:edge + 1, :] - acs), expand)
    xdt = xs * dt_e
    xdt_b = xdt.astype(BF16)
    xw_b = (xdt * dc_e).astype(BF16)
    lane = _iota((rows, LANE), 1)

    y_parts = []
    for g in range(SSD_GROUPS):
        bm = u[:, inner + g * SSD_STATE:inner + (g + 1) * SSD_STATE].astype(BF16)
        cm = u[:, inner + (SSD_GROUPS + g) * SSD_STATE:inner + (SSD_GROUPS + g + 1) * SSD_STATE].astype(BF16)
        cb = _dot_nt(cm, bm)
        st = st_ref[g]
        y_off = _dot(cm, st.astype(BF16)) * ea_e[:, g * gw:(g + 1) * gw]
        heads_per_group = SSD_HEADS // SSD_GROUPS
        diag = []
        for hp in range(heads_per_group // 2):
            h0 = g * heads_per_group + 2 * hp
            mats = []
            for hh in (h0, h0 + 1):
                li = d * SSD_HEADS + hh
                lm = jnp.exp(jnp.where(incl, acs[:, li:li + 1] - acs_t[li:li + 1, :], NEG_BIG))
                mats.append((cb * lm).astype(BF16))
            xp = xdt_b[:, h0 * SSD_HEADDIM:(h0 + 2) * SSD_HEADDIM]
            x_lo = jnp.where(lane < SSD_HEADDIM, xp, jnp.zeros_like(xp))
            x_hi = jnp.where(lane >= SSD_HEADDIM, xp, jnp.zeros_like(xp))
            diag.append(_dot(jnp.concatenate(mats, axis=1), jnp.concatenate([x_lo, x_hi], axis=0)))
        y_parts.append(jnp.concatenate(diag, axis=1) + y_off)
        st_ref[g] = (st * ea_e[edge:edge + 1, g * gw:(g + 1) * gw]
                     + _dot_tn(bm, xw_b[:, g * gw:(g + 1) * gw]))
    y = jnp.concatenate(y_parts, axis=1) + dsk_ref[...] * xs
    if final:
        y = y + yf_ref[...]
        yz = y * _silu(z_ref[...])
        outs = []
        for g in range(SSD_GROUPS):
            yg = yz[:, g * gw:(g + 1) * gw]
            outs.append(yg * lax.rsqrt(jnp.mean(yg * yg, axis=-1, keepdims=True) + NORM_EPS))
        o_ref[...] = (jnp.concatenate(outs, axis=1) * nw_ref[...]).astype(o_ref.dtype)
    else:
        o_ref[...] = y


def _ssd_scan(z, u, dt_raw, dtb_vec, a_vec, dskip, y_fwd, norm_w, n_lat, n_ctx, reverse):
    bsz, l, xbc_w = u.shape
    inner = SSD_HEADS * SSD_HEADDIM
    final = y_fwd is not None
    chunk_fn = functools.partial(_chunk_of_step, n_lat=n_lat, n_ctx=n_ctx, reverse=reverse)
    tok = lambda n: pl.BlockSpec((None, SCAN_CHUNK, n), lambda b, i: (b, chunk_fn(i), 0))
    row = lambda n: pl.BlockSpec((1, n), lambda b, i: (0, 0))
    in_specs = [tok(xbc_w), tok(LANE), row(LANE), row(LANE), row(inner)]
    args = [u, dt_raw, dtb_vec, a_vec, dskip]
    if final:
        in_specs = [tok(inner)] + in_specs + [tok(inner), row(inner)]
        args = [z] + args + [y_fwd, norm_w]
    return pl.pallas_call(
        functools.partial(_ssd_scan_kernel, reverse=reverse, final=final),
        grid=(bsz, l // SCAN_CHUNK),
        in_specs=in_specs,
        out_specs=tok(inner),
        out_shape=jax.ShapeDtypeStruct((bsz, l, inner), BF16 if final else F32),
        scratch_shapes=[pltpu.VMEM((SSD_GROUPS, SSD_STATE, inner // SSD_GROUPS), F32)],
        compiler_params=_cparams("arbitrary", "arbitrary"),
        name="ssd_rev" if reverse else "ssd_fwd",
    )(*args)


def _merge_kernel(x_ref, lru_ref, gdn_ref, ssdl_ref, ssdc_ref, gate_ref, mod_ref, nw_ref, wb_ref, wo_ref, wr_ref,
                  br_ref, xo_ref, h_ref, lg_ref, *, n_lat):
    d = x_ref.shape[1]
    ssd = jnp.where(pl.program_id(1) >= n_lat, ssdc_ref[...], ssdl_ref[...])
    acc = None
    for k, b in enumerate((lru_ref[...], gdn_ref[...], ssd)):
        t = jax.nn.sigmoid(gate_ref[:, k * d:(k + 1) * d].astype(F32)) * _dot(b.astype(BF16), wb_ref[k])
        acc = t if acc is None else acc + t
    mix = _dot(acc.astype(BF16), wo_ref[...])
    x = x_ref[...] + mod_ref[2:3, :] * mix
    xo_ref[...] = x
    h = _mod_norm(x, nw_ref[...], mod_ref[3:4, :], mod_ref[4:5, :])
    hb = h.astype(BF16)
    h_ref[...] = hb
    lg_ref[...] = _dot(hb, wr_ref[...]) + br_ref[...]


def _merge(x, lru, gdn, ssd_lat, ssd_scan, gate, modtab, norm_w, wb, wo, wr, br, n_lat_tiles):
    bsz, l, d = x.shape
    tok = pl.BlockSpec((None, ROW_TILE, d), lambda b, j: (b, j, 0))
    return pl.pallas_call(
        functools.partial(_merge_kernel, n_lat=n_lat_tiles),
        grid=(bsz, l // ROW_TILE),
        in_specs=[tok, tok, tok,
                  pl.BlockSpec((None, ROW_TILE, d), lambda b, j: (b, jnp.minimum(j, n_lat_tiles - 1), 0)),
                  pl.BlockSpec((None, ROW_TILE, d), lambda b, j: (b, jnp.maximum(j, n_lat_tiles), 0)),
                  pl.BlockSpec((None, ROW_TILE, 3 * d), lambda b, j: (b, j, 0)),
                  pl.BlockSpec((None, None, SUBLANE, d), lambda b, j: (b, jnp.where(j >= n_lat_tiles, 1, 0), 0, 0)),
                  pl.BlockSpec((1, d), lambda b, j: (0, 0)),
                  pl.BlockSpec((3, d, d), lambda b, j: (0, 0, 0)),
                  pl.BlockSpec((d, d), lambda b, j: (0, 0)),
                  pl.BlockSpec((d, LANE), lambda b, j: (0, 0)),
                  pl.BlockSpec((1, LANE), lambda b, j: (0, 0))],
        out_specs=[tok, tok, pl.BlockSpec((None, ROW_TILE, LANE), lambda b, j: (b, j, 0))],
        out_shape=[jax.ShapeDtypeStruct((bsz, l, d), F32), jax.ShapeDtypeStruct((bsz, l, d), BF16),
                   jax.ShapeDtypeStruct((bsz, l, LANE), F32)],
        compiler_params=_cparams("arbitrary", "arbitrary"),
        name="merge",
    )(x, lru, gdn, ssd_lat, ssd_scan, gate, modtab, norm_w, wb, wo, wr, br)


def _moe_kernel(blk_ref, exp_ref, lo_ref, hi_ref, x_ref, w1_ref, w3_ref, w2_ref, o_ref,
                acc_ref, w1b_ref, w3b_ref, w2b_ref):
    i = pl.program_id(0)
    lo = lo_ref[i]
    hi = hi_ref[i]
    base = blk_ref[i] * MOE_ROWS

    @pl.when(jnp.logical_or(i == 0, exp_ref[i] != exp_ref[jnp.maximum(i - 1, 0)]))
    def _():
        w1b_ref[...] = w1_ref[...].astype(BF16)
        w3b_ref[...] = w3_ref[...].astype(BF16)
        w2b_ref[...] = w2_ref[...].astype(BF16)

    @pl.when(lo == base)
    def _():
        acc_ref[...] = jnp.zeros_like(acc_ref)

    @pl.when(hi > lo)
    def _():
        row = base + _iota((MOE_ROWS, 1), 0)
        x = jnp.where(jnp.logical_and(row >= lo, row < hi), x_ref[...], jnp.zeros_like(x_ref))
        mid = _silu(_dot(x, w1b_ref[...])) * _dot(x, w3b_ref[...])
        acc_ref[...] += _dot(mid.astype(BF16), w2b_ref[...])

    o_ref[...] = acc_ref[...].astype(o_ref.dtype)


def _moe_experts(xb, item_block, item_expert, item_lo, item_hi, w1, w3, w2, layer):
    n_rows, d = xb.shape
    ff = w1.shape[-1]
    rows_spec = pl.BlockSpec((MOE_ROWS, d), lambda i, blk, ex, lo, hi: (blk[i], 0))
    grid_spec = pltpu.PrefetchScalarGridSpec(
        num_scalar_prefetch=4,
        grid=(item_block.shape[0],),
        in_specs=[rows_spec,
                  pl.BlockSpec((None, None, d, ff), lambda i, blk, ex, lo, hi: (layer, ex[i], 0, 0)),
                  pl.BlockSpec((None, None, d, ff), lambda i, blk, ex, lo, hi: (layer, ex[i], 0, 0)),
                  pl.BlockSpec((None, None, ff, d), lambda i, blk, ex, lo, hi: (layer, ex[i], 0, 0))],
        out_specs=rows_spec,
        scratch_shapes=[pltpu.VMEM((MOE_ROWS, d), F32), pltpu.VMEM((d, ff), BF16), pltpu.VMEM((d, ff), BF16),
                        pltpu.VMEM((ff, d), BF16)],
    )
    return pl.pallas_call(
        _moe_kernel,
        grid_spec=grid_spec,
        out_shape=jax.ShapeDtypeStruct((n_rows, d), BF16),
        compiler_params=_cparams("arbitrary"),
        name="moe_experts",
    )(item_block, item_expert, item_lo, item_hi, xb, w1, w3, w2)


def _route(logits):
    t = logits.shape[0]
    g_logits = logits[:, :N_GROUPS]
    e_logits = logits[:, N_GROUPS:N_GROUPS + N_EXPERTS].reshape(t, N_GROUPS, EXPERTS_PER_GROUP)
    g_prob = jax.nn.softmax(g_logits, axis=-1)
    g_idx = jnp.argmax(g_logits, axis=-1)
    p_group = jnp.take_along_axis(g_prob, g_idx[:, None], axis=1)[:, 0]
    e_in_group = jnp.take_along_axis(e_logits, g_idx[:, None, None], axis=1)[:, 0]
    top_v, top_i = lax.top_k(e_in_group, TOP_K)
    weights = jax.nn.softmax(top_v, axis=-1) * p_group[:, None]
    expert_id = (g_idx[:, None] * EXPERTS_PER_GROUP + top_i).reshape(-1).astype(jnp.int32)
    n_assign = t * TOP_K
    assert n_assign % MOE_ROWS == 0
    n_blocks = n_assign // MOE_ROWS
    ar = jnp.arange(n_assign, dtype=jnp.int32)
    e_sorted, order = lax.sort((expert_id, ar), num_keys=1)
    _, slot = lax.sort((order, ar), num_keys=1)
    ends = jnp.sum(e_sorted[None, :] <= jnp.arange(N_EXPERTS, dtype=jnp.int32)[:, None], axis=1).astype(jnp.int32)
    starts = jnp.concatenate([jnp.zeros((1,), jnp.int32), ends[:-1]])
    cuts = jnp.sort(jnp.concatenate([jnp.arange(n_blocks, dtype=jnp.int32) * MOE_ROWS, starts]))
    item_lo = cuts
    item_hi = jnp.concatenate([cuts[1:], jnp.full((1,), n_assign, jnp.int32)])
    item_block = jnp.minimum(item_lo // MOE_ROWS, n_blocks - 1).astype(jnp.int32)
    item_expert = jnp.minimum(jnp.sum(item_lo[:, None] >= ends[None, :], axis=1), N_EXPERTS - 1).astype(jnp.int32)
    return order // TOP_K, (item_block, item_expert, item_lo, item_hi), slot.reshape(t, TOP_K), weights


def _combine_kernel(x_ref, ya_ref, yb_ref, wt_ref, mod_ref, nw_ref, o_ref, *, final):
    wt = wt_ref[...]
    y = wt[:, 0:1] * ya_ref[...].astype(F32) + wt[:, 1:2] * yb_ref[...].astype(F32)
    x = x_ref[...] + mod_ref[5:6, :] * y
    if final:
        x = x * lax.rsqrt(jnp.mean(x * x, axis=-1, keepdims=True) + NORM_EPS) * nw_ref[...]
    o_ref[...] = x


def _combine(x, ya, yb, wts, modtab, norm_w, n_lat_tiles, final, out_len):
    bsz, l, d = x.shape
    tok = pl.BlockSpec((None, ROW_TILE, d), lambda b, j: (b, j, 0))
    return pl.pallas_call(
        functools.partial(_combine_kernel, final=final),
        grid=(bsz, out_len // ROW_TILE),
        in_specs=[tok, tok, tok,
                  pl.BlockSpec((None, ROW_TILE, LANE), lambda b, j: (b, j, 0)),
                  pl.BlockSpec((None, None, SUBLANE, d), lambda b, j: (b, jnp.where(j >= n_lat_tiles, 1, 0), 0, 0)),
                  pl.BlockSpec((1, d), lambda b, j: (0, 0))],
        out_specs=tok,
        out_shape=jax.ShapeDtypeStruct((bsz, out_len, d), F32),
        compiler_params=_cparams("arbitrary", "arbitrary"),
        name="combine",
    )(x, ya, yb, wts, modtab, norm_w)


def _pad_lanes(v, n=LANE):
    v = v.reshape(1, -1).astype(F32)
    return jnp.pad(v, ((0, 0), (0, n - v.shape[1])))


def kernel(x, c, ctx, c_ctx, w_mod, b_mod, norm1_w, norm2_w, w_in, lru_conv_w, lru_conv_b, lru_wa, lru_ba,
           lru_wi, lru_bi, lru_lambda, gdn_conv_w, gdn_a_log, gdn_dt_bias, gdn_norm_w, ssd_conv_w, ssd_conv_b,
           ssd_a_log, ssd_dt_bias, ssd_d, ssd_norm_w, w_branch, w_out, router_group_w, router_group_b,
           router_expert_w, router_expert_b, expert_w1, expert_w3, expert_w2, final_norm_w):
    bsz, seq, d = x.shape
    n_ctx_tok = ctx.shape[1]
    depth = w_mod.shape[0]
    l = seq + n_ctx_tok
    rows = seq // GRID_W
    assert seq % LRU_CHUNK == 0 and n_ctx_tok % LRU_CHUNK == 0 and ROW_TILE == LRU_CHUNK
    assert bsz + 1 <= SUBLANE
    hw = GDN_HEADS * GDN_DK
    inner = SSD_HEADS * SSD_HEADDIM
    xbc_w = inner + 2 * SSD_GROUPS * SSD_STATE

    act = jnp.concatenate([_silu(c), _silu(c_ctx)[None, :],
                           jnp.zeros((SUBLANE - bsz - 1, d), F32)], axis=0)
    mod_all = _modulation(act, w_mod, b_mod).reshape(depth, SUBLANE, 6, d)
    pad2 = jnp.zeros((bsz, 2, d), F32)

    xs = jnp.concatenate([x, ctx], axis=1)
    nl_t, nc_t = seq // ROW_TILE, n_ctx_tok // ROW_TILE
    nl_s, nc_s = seq // SCAN_CHUNK, n_ctx_tok // SCAN_CHUNK

    o_lx, o_qkv, o_gz, o_gb, o_ga = 0, 2 * d, 2 * d + 3 * hw, 2 * d + 4 * hw, 2 * d + 4 * hw + 2 * GDN_HEADS
    o_sz = o_ga + 2 * GDN_HEADS
    o_xbc = o_sz + inner
    o_sdt = o_xbc + xbc_w
    o_gate = o_sdt + 2 * SSD_HEADS

    for i in range(depth):
        lat = mod_all[i, :bsz]
        cx = jnp.broadcast_to(mod_all[i, bsz][None], (bsz, 6, d))
        modtab = jnp.stack([jnp.concatenate([lat, pad2], axis=1), jnp.concatenate([cx, pad2], axis=1)], axis=1)

        wi_ = w_in[i]
        w_lru = wi_[:, o_lx:o_qkv].astype(BF16)
        zpad = jnp.zeros((d, LANE - 4 * GDN_HEADS), F32)
        w_gdn = jnp.concatenate([wi_[:, o_qkv:o_gb], wi_[:, o_ga:o_sz], wi_[:, o_gb:o_ga], zpad], axis=1).astype(BF16)
        zpad2 = jnp.zeros((d, LANE - 2 * SSD_HEADS), F32)
        w_ssd = jnp.concatenate([wi_[:, o_sz:o_gate], zpad2], axis=1).astype(BF16)
        w_gate = wi_[:, o_gate:].astype(BF16)
        n1 = norm1_w[i].reshape(1, d)

        gate = _norm_matmul(xs, modtab, n1, w_gate, 0, 1, nl_t, "in_gate")

        lru_u, lru_y = _in_lru(xs, modtab, n1, w_lru, lru_conv_w[i], lru_conv_b[i].reshape(1, d), nl_t, nc_t)
        h_f = _lru_scan(lru_u, lru_y, lru_wa[i, 0].astype(BF16), lru_ba[i, 0].reshape(1, d),
                        lru_wi[i, 0].astype(BF16), lru_bi[i, 0].reshape(1, d), lru_lambda[i, 0].reshape(1, d),
                        None, nl_t, nc_t, False)
        lru_out = _lru_scan(lru_u, lru_y, lru_wa[i, 1].astype(BF16), lru_ba[i, 1].reshape(1, d),
                            lru_wi[i, 1].astype(BF16), lru_bi[i, 1].reshape(1, d), lru_lambda[i, 1].reshape(1, d),
                            h_f, nl_t, nc_t, True)

        q, k, v, gz, gcol, grow = _in_gdn(xs, modtab, n1, w_gdn, gdn_conv_w[i], _pad_lanes(gdn_a_log[i]),
                                          _pad_lanes(gdn_dt_bias[i]), nl_t, nc_t)
        o_f = _gdn_scan(q, k, v, gcol, grow, None, None, None, nl_s, nc_s, False)
        gdn_out = _gdn_scan(q, k, v, gcol, grow, o_f, gz, gdn_norm_w[i].reshape(1, GDN_DK), nl_s, nc_s, True)

        x_cm = xs[:, :seq].reshape(bsz, rows, GRID_W, d).swapaxes(1, 2).reshape(bsz, seq, d)
        sz, su, sdt = _in_ssd(x_cm, xs, modtab, n1, w_ssd, ssd_conv_w[i], ssd_conv_b[i].reshape(1, xbc_w),
                              nl_t, nc_t)
        dtb_vec = _pad_lanes(ssd_dt_bias[i])
        a_vec = _pad_lanes(-jnp.exp(ssd_a_log[i].astype(F32)))
        dsk = [jnp.repeat(ssd_d[i, dd], SSD_HEADDIM).reshape(1, inner) for dd in range(2)]
        y_f = _ssd_scan(None, su, sdt, dtb_vec, a_vec, dsk[0], None, None, nl_s, nc_s, False)
        ssd_scan_out = _ssd_scan(sz, su, sdt, dtb_vec, a_vec, dsk[1], y_f,
                                 ssd_norm_w[i].reshape(1, inner), nl_s, nc_s, True)
        ssd_lat = ssd_scan_out[:, :seq].reshape(bsz, GRID_W, rows, inner).swapaxes(1, 2).reshape(bsz, seq, inner)

        w_r = jnp.concatenate([router_group_w[i], router_expert_w[i],
                               jnp.zeros((d, LANE - N_GROUPS - N_EXPERTS), F32)], axis=1).astype(BF16)
        b_r = _pad_lanes(jnp.concatenate([router_group_b[i], router_expert_b[i]]))
        x_mid, h2, logits = _merge(xs, lru_out, gdn_out, ssd_lat, ssd_scan_out, gate, modtab, norm2_w[i].reshape(1, d),
                                   w_branch[i].astype(BF16), w_out[i].astype(BF16), w_r, b_r, nl_t)

        t = bsz * l
        tok_sorted, items, slot, weights = _route(logits.reshape(t, LANE))
        xb = h2.reshape(t, d)[tok_sorted]
        yb = _moe_experts(xb, *items, expert_w1, expert_w3, expert_w2, i)
        y0 = yb[slot[:, 0]].reshape(bsz, l, d)
        y1 = yb[slot[:, 1]].reshape(bsz, l, d)
        wts = jnp.pad(weights, ((0, 0), (0, LANE - TOP_K))).reshape(bsz, l, LANE)
        last = i == depth - 1
        xs = _combine(x_mid, y0, y1, wts, modtab, final_norm_w.reshape(1, d), nl_t, last, seq if last else l)

    return xs
```

```python
import functools

import jax
import jax.numpy as jnp
from jax import lax
from jax.experimental import pallas as pl
from jax.experimental.pallas import tpu as pltpu

GRID_W = 64
CONV_K = 4
NORM_EPS = 1e-6
LRU_BLOCKS = 8
LRU_C = 8.0
GDN_HEADS = 8
GDN_DK = 128
SSD_HEADS = 16
SSD_HEADDIM = 64
SSD_GROUPS = 2
SSD_STATE = 128
N_GROUPS = 4
EXPERTS_PER_GROUP = 8
N_EXPERTS = N_GROUPS * EXPERTS_PER_GROUP
TOP_K = 2

LANE = 128
SUBLANE = 8
ROW_TILE = 256
LRU_CHUNK = 256
SCAN_CHUNK = 128
MOE_ROWS = 512
VMEM_LIMIT = 56 * 1024 * 1024

F32 = jnp.float32
BF16 = jnp.bfloat16
NEG_BIG = -1e30


def _cparams(*sem):
    return pltpu.CompilerParams(dimension_semantics=sem, vmem_limit_bytes=VMEM_LIMIT)


def _dot(a, b):
    return jnp.dot(a, b, preferred_element_type=F32)


def _dot_nt(a, b):
    return lax.dot_general(a, b, (((1,), (1,)), ((), ())), preferred_element_type=F32)


def _dot_tn(a, b):
    return lax.dot_general(a, b, (((0,), (0,)), ((), ())), preferred_element_type=F32)


def _split3(x):
    hi = x.astype(BF16)
    r1 = x - hi.astype(F32)
    mid = r1.astype(BF16)
    lo = (r1 - mid.astype(F32)).astype(BF16)
    return hi, mid, lo


def _exact_left(m_bf, x):
    hi, mid, lo = _split3(x)
    return _dot(jnp.concatenate([m_bf, m_bf, m_bf], axis=1), jnp.concatenate([hi, mid, lo], axis=0))


def _exact_right(x, m_bf):
    hi, mid, lo = _split3(x)
    return _dot(jnp.concatenate([hi, mid, lo], axis=1), jnp.concatenate([m_bf, m_bf, m_bf], axis=0))


def _exact_transpose(x, eye_bf):
    hi, mid, lo = _split3(x)
    return _dot_nt(jnp.concatenate([eye_bf, eye_bf, eye_bf], axis=1), jnp.concatenate([hi, mid, lo], axis=1))


def _softplus(x):
    return jnp.maximum(x, 0.0) + jnp.log(1.0 + jnp.exp(-jnp.abs(x)))


def _silu(x):
    return x * jax.nn.sigmoid(x)


def _iota(shape, dim):
    return lax.broadcasted_iota(jnp.int32, shape, dim)


def _fwd_chunk(i, n_lat, n_ctx):
    return jnp.where(i < n_ctx, n_lat + i, i - n_ctx)


def _chunk_of_step(i, n_lat, n_ctx, reverse):
    return (n_lat + n_ctx - 1 - i) if reverse else _fwd_chunk(i, n_lat, n_ctx)


def _segment_edges(c, n_lat, n_ctx):
    first = jnp.logical_or(c == 0, c == n_lat)
    last = jnp.logical_or(c == n_lat - 1, c == n_lat + n_ctx - 1)
    return first, last


def _normed_rows(x, x_prev, x_next, mod_ref, nw_ref):
    shift, scale = mod_ref[0:1, :], mod_ref[1:2, :]
    h_main = _mod_norm(x, nw_ref[...], shift, scale)
    h_prev = _mod_norm(x_prev, nw_ref[...], shift, scale)
    h_next = _mod_norm(x_next, nw_ref[...], shift, scale)
    return h_main.astype(BF16), jnp.concatenate([h_prev, h_main, h_next], axis=0).astype(BF16)


def _proj_conv(xe_ref, h_all, w_ref, cw_ref, col0, width, first, last, rows, col_step=512):
    base = SUBLANE - CONV_K // 2
    for c0 in range(0, width, col_step):
        c1 = min(c0 + col_step, width)
        y = _dot(h_all, w_ref[:, col0 + c0:col0 + c1])
        xe_ref[0:SUBLANE, c0:c1] = jnp.where(first, 0.0, y[0:SUBLANE])
        xe_ref[SUBLANE:SUBLANE + rows, c0:c1] = y[SUBLANE:SUBLANE + rows]
        xe_ref[SUBLANE + rows:2 * SUBLANE + rows, c0:c1] = jnp.where(last, 0.0, y[SUBLANE + rows:])
        u = cw_ref[0:1, c0:c1] * xe_ref[base:base + rows, c0:c1]
        for j in range(1, CONV_K):
            u = u + cw_ref[j:j + 1, c0:c1] * xe_ref[base + j:base + j + rows, c0:c1]
        yield c0, c1, u


def _halo_specs(width, rows, n_rows_total, chunk_fn, col_block=0):
    per = rows // SUBLANE
    last_tile = n_rows_total // SUBLANE - 1
    prev = pl.BlockSpec((None, SUBLANE, width),
                        lambda b, i: (b, jnp.maximum(chunk_fn(i) * per - 1, 0), col_block))
    nxt = pl.BlockSpec((None, SUBLANE, width),
                       lambda b, i: (b, jnp.minimum((chunk_fn(i) + 1) * per, last_tile), col_block))
    return prev, nxt


def _mod_kernel(a_ref, w_ref, b_ref, o_ref):
    o_ref[...] = _dot(a_ref[...].astype(BF16), w_ref[...].astype(BF16)) + b_ref[...]


def _modulation(act, w_mod, b_mod):
    depth, d, n = w_mod.shape
    tn = 1536
    return pl.pallas_call(
        _mod_kernel,
        grid=(depth, n // tn),
        in_specs=[pl.BlockSpec((SUBLANE, d), lambda l, j: (0, 0)),
                  pl.BlockSpec((None, d, tn), lambda l, j: (l, 0, j)),
                  pl.BlockSpec((None, 1, tn), lambda l, j: (l, 0, j))],
        out_specs=pl.BlockSpec((None, SUBLANE, tn), lambda l, j: (l, 0, j)),
        out_shape=jax.ShapeDtypeStruct((depth, SUBLANE, n), F32),
        compiler_params=_cparams("arbitrary", "arbitrary"),
        name="modulation",
    )(act, w_mod, b_mod.reshape(depth, 1, n))


def _mod_norm(x, nw, shift, scale):
    y = x * lax.rsqrt(jnp.mean(x * x, axis=-1, keepdims=True) + NORM_EPS)
    return (y * nw) * (1.0 + scale) + shift


def _norm_matmul_kernel(x_ref, mod_ref, nw_ref, w_ref, o_ref, *, shift_row, scale_row, col_step):
    h = _mod_norm(x_ref[...], nw_ref[...], mod_ref[shift_row:shift_row + 1, :],
                  mod_ref[scale_row:scale_row + 1, :]).astype(BF16)
    n = w_ref.shape[1]
    for c0 in range(0, n, col_step):
        c1 = min(c0 + col_step, n)
        o_ref[:, c0:c1] = _dot(h, w_ref[:, c0:c1]).astype(o_ref.dtype)


def _norm_matmul(x, modtab, norm_w, w, shift_row, scale_row, n_lat_tiles, name):
    bsz, l, d = x.shape
    n = w.shape[1]
    kern = functools.partial(_norm_matmul_kernel, shift_row=shift_row, scale_row=scale_row, col_step=512)
    return pl.pallas_call(
        kern,
        grid=(bsz, l // ROW_TILE),
        in_specs=[pl.BlockSpec((None, ROW_TILE, d), lambda b, j: (b, j, 0)),
                  pl.BlockSpec((None, None, SUBLANE, d), lambda b, j: (b, jnp.where(j >= n_lat_tiles, 1, 0), 0, 0)),
                  pl.BlockSpec((1, d), lambda b, j: (0, 0)),
                  pl.BlockSpec((d, n), lambda b, j: (0, 0))],
        out_specs=pl.BlockSpec((None, ROW_TILE, n), lambda b, j: (b, j, 0)),
        out_shape=jax.ShapeDtypeStruct((bsz, l, n), BF16),
        compiler_params=_cparams("arbitrary", "arbitrary"),
        name=name,
    )(x, modtab, norm_w, w)


def _row_tile_specs(d, l, n_lat_tiles):
    ident = lambda j: j
    main = pl.BlockSpec((None, ROW_TILE, d), lambda b, j: (b, j, 0))
    prev, nxt = _halo_specs(d, ROW_TILE, l, ident)
    mod = pl.BlockSpec((None, None, SUBLANE, d), lambda b, j: (b, jnp.where(j >= n_lat_tiles, 1, 0), 0, 0))
    return [main, prev, nxt, mod, pl.BlockSpec((1, d), lambda b, j: (0, 0))]


def _in_lru_kernel(x_ref, xp_ref, xn_ref, mod_ref, nw_ref, w_ref, cw_ref, cb_ref, u_ref, y_ref, xe_ref,
                   *, n_lat, n_ctx):
    first, last = _segment_edges(pl.program_id(1), n_lat, n_ctx)
    h_main, h_all = _normed_rows(x_ref[...], xp_ref[...], xn_ref[...], mod_ref, nw_ref)
    w = u_ref.shape[1]
    for c0, c1, u in _proj_conv(xe_ref, h_all, w_ref, cw_ref, 0, w, first, last, ROW_TILE):
        u_ref[:, c0:c1] = u + cb_ref[:, c0:c1]
    for c0 in range(0, w, 512):
        y_ref[:, c0:c0 + 512] = _dot(h_main, w_ref[:, w + c0:w + c0 + 512])


def _in_lru(x, modtab, norm_w, w, conv_w, conv_b, n_lat, n_ctx):
    bsz, l, d = x.shape
    wd = w.shape[1] // 2
    tok = pl.BlockSpec((None, ROW_TILE, wd), lambda b, j: (b, j, 0))
    return pl.pallas_call(
        functools.partial(_in_lru_kernel, n_lat=n_lat, n_ctx=n_ctx),
        grid=(bsz, l // ROW_TILE),
        in_specs=_row_tile_specs(d, l, n_lat) + [pl.BlockSpec((d, 2 * wd), lambda b, j: (0, 0)),
                                                 pl.BlockSpec((CONV_K, wd), lambda b, j: (0, 0)),
                                                 pl.BlockSpec((1, wd), lambda b, j: (0, 0))],
        out_specs=[tok, tok],
        out_shape=[jax.ShapeDtypeStruct((bsz, l, wd), F32)] * 2,
        scratch_shapes=[pltpu.VMEM((ROW_TILE + 2 * SUBLANE, wd), F32)],
        compiler_params=_cparams("arbitrary", "arbitrary"),
        name="in_lru",
    )(x, x, x, modtab, norm_w, w, conv_w, conv_b)


def _lru_kernel(*refs, reverse, final):
    if final:
        u_ref, wa_ref, ba_ref, wi_ref, bi_ref, lam_ref, hf_ref, y_ref, o_ref, carry_ref = refs
    else:
        u_ref, wa_ref, ba_ref, wi_ref, bi_ref, lam_ref, o_ref, carry_ref = refs
    rows = LRU_CHUNK
    i = pl.program_id(1)

    @pl.when(i == 0)
    def _():
        carry_ref[...] = jnp.zeros_like(carry_ref)

    u = u_ref[...]
    ub = u.astype(BF16)
    bw = u.shape[1] // LRU_BLOCKS
    pre_r = jnp.concatenate([_dot(ub[:, n * bw:(n + 1) * bw], wa_ref[n]) for n in range(LRU_BLOCKS)], axis=1)
    pre_i = jnp.concatenate([_dot(ub[:, n * bw:(n + 1) * bw], wi_ref[n]) for n in range(LRU_BLOCKS)], axis=1)
    r = jax.nn.sigmoid(pre_r + ba_ref[...])
    gi = jax.nn.sigmoid(pre_i + bi_ref[...])
    a = jnp.exp(-LRU_C * r * _softplus(-lam_ref[...]))
    h = jnp.sqrt(1.0 - a * a) * gi * u

    srow = _iota((SUBLANE, a.shape[1]), 0)
    n_tiles = rows // SUBLANE
    carry = carry_ref[...]
    done = {}
    for k in (range(n_tiles - 1, -1, -1) if reverse else range(n_tiles)):
        sl = slice(k * SUBLANE, (k + 1) * SUBLANE)
        at, ht = a[sl], h[sl]
        for s in (1, 2, 4):
            if reverse:
                keep = srow < SUBLANE - s
                sh = SUBLANE - s
            else:
                keep = srow >= s
                sh = s
            h_s = jnp.where(keep, pltpu.roll(ht, sh, axis=0), 0.0)
            a_s = jnp.where(keep, pltpu.roll(at, sh, axis=0), 1.0)
            ht = ht + at * h_s
            at = at * a_s
        hk = ht + at * carry
        carry = hk[0:1, :] if reverse else hk[SUBLANE - 1:SUBLANE, :]
        if final:
            done[k] = (hk + hf_ref[sl, :]) * jax.nn.gelu(y_ref[sl, :], approximate=True)
            if (k ^ 1) in done:
                lo = k & ~1
                pair = jnp.concatenate([done.pop(lo), done.pop(lo + 1)], axis=0)
                o_ref[lo * SUBLANE:(lo + 2) * SUBLANE, :] = pair.astype(o_ref.dtype)
        else:
            o_ref[sl, :] = hk
    carry_ref[...] = carry


def _lru_scan(u, y, wa, ba, wi, bi, lam, hf, n_lat, n_ctx, reverse):
    bsz, l, w = u.shape
    final = hf is not None
    chunk_fn = functools.partial(_chunk_of_step, n_lat=n_lat, n_ctx=n_ctx, reverse=reverse)
    main = pl.BlockSpec((None, LRU_CHUNK, w), lambda b, i: (b, chunk_fn(i), 0))
    vec = pl.BlockSpec((1, w), lambda b, i: (0, 0))
    blk = pl.BlockSpec((LRU_BLOCKS, w // LRU_BLOCKS, w // LRU_BLOCKS), lambda b, i: (0, 0, 0))
    in_specs = [main, blk, vec, blk, vec, vec]
    args = [u, wa, ba, wi, bi, lam]
    if final:
        in_specs += [main, main]
        args += [hf, y]
    return pl.pallas_call(
        functools.partial(_lru_kernel, reverse=reverse, final=final),
        grid=(bsz, l // LRU_CHUNK),
        in_specs=in_specs,
        out_specs=main,
        out_shape=jax.ShapeDtypeStruct((bsz, l, w), BF16 if final else F32),
        scratch_shapes=[pltpu.VMEM((1, w), F32)],
        compiler_params=_cparams("arbitrary", "arbitrary"),
        name="lru_rev" if reverse else "lru_fwd",
    )(*args)


def _in_gdn_kernel(x_ref, xp_ref, xn_ref, mod_ref, nw_ref, w_ref, cw_ref, alog_ref, dtb_ref,
                   q_ref, k_ref, v_ref, z_ref, gcol_ref, grow_ref, xe_ref, *, n_lat, n_ctx):
    rows = ROW_TILE
    first, last = _segment_edges(pl.program_id(1), n_lat, n_ctx)
    h_main, h_all = _normed_rows(x_ref[...], xp_ref[...], xn_ref[...], mod_ref, nw_ref)
    hw = GDN_HEADS * GDN_DK
    for c0, c1, u in _proj_conv(xe_ref, h_all, w_ref, cw_ref, 0, 3 * hw, first, last, rows):
        u = _silu(u)
        for h0 in range(c0, c1, GDN_DK):
            uh = u[:, h0 - c0:h0 - c0 + GDN_DK]
            if h0 < 2 * hw:
                uh = uh * lax.rsqrt(jnp.sum(uh * uh, axis=-1, keepdims=True) + NORM_EPS)
            if h0 < hw:
                q_ref[:, h0:h0 + GDN_DK] = uh * (GDN_DK ** -0.5)
            elif h0 < 2 * hw:
                k_ref[:, h0 - hw:h0 - hw + GDN_DK] = uh
            else:
                v_ref[:, h0 - 2 * hw:h0 - 2 * hw + GDN_DK] = uh
    for c0 in range(0, hw, 512):
        z_ref[:, c0:c0 + 512] = _dot(h_main, w_ref[:, 3 * hw + c0:3 * hw + c0 + 512])

    sm = _dot(h_main, w_ref[:, 4 * hw:4 * hw + LANE])
    g = -jnp.exp(alog_ref[...]) * _softplus(sm + dtb_ref[...])
    beta = jax.nn.sigmoid(sm)
    ri = _iota((rows, rows), 0)
    ci = _iota((rows, rows), 1)
    same = (ri // SCAN_CHUNK) == (ci // SCAN_CHUNK)
    tri_f = jnp.where(jnp.logical_and(same, ci <= ri), 1.0, 0.0).astype(BF16)
    tri_r = jnp.where(jnp.logical_and(same, ci >= ri), 1.0, 0.0).astype(BF16)
    gcs_f = _exact_left(tri_f, g)
    gcs_r = _exact_left(tri_r, g)
    lane = _iota((rows, LANE), 1)
    gcol = jnp.where(lane < GDN_HEADS, gcs_f,
                     jnp.where(lane < 2 * GDN_HEADS, gcs_r, jnp.where(lane < 4 * GDN_HEADS, beta, 0.0)))
    gcol_ref[...] = gcol
    eye = jnp.where(_iota((LANE, LANE), 0) == _iota((LANE, LANE), 1), 1.0, 0.0).astype(BF16)
    grow_ref[...] = _exact_transpose(gcol, eye)[0:4 * GDN_HEADS, :]


def _in_gdn(x, modtab, norm_w, w, conv_w, alog_vec, dtb_vec, n_lat, n_ctx):
    bsz, l, d = x.shape
    hw = GDN_HEADS * GDN_DK
    tok = pl.BlockSpec((None, ROW_TILE, hw), lambda b, i: (b, i, 0))
    return pl.pallas_call(
        functools.partial(_in_gdn_kernel, n_lat=n_lat, n_ctx=n_ctx),
        grid=(bsz, l // ROW_TILE),
        in_specs=_row_tile_specs(d, l, n_lat) + [pl.BlockSpec((d, w.shape[1]), lambda b, i: (0, 0)),
                                                 pl.BlockSpec((CONV_K, 3 * hw), lambda b, i: (0, 0)),
                                                 pl.BlockSpec((1, LANE), lambda b, i: (0, 0)),
                                                 pl.BlockSpec((1, LANE), lambda b, i: (0, 0))],
        out_specs=[tok, tok, tok, tok,
                   pl.BlockSpec((None, ROW_TILE, LANE), lambda b, i: (b, i, 0)),
                   pl.BlockSpec((None, 4 * GDN_HEADS, ROW_TILE), lambda b, i: (b, 0, i))],
        out_shape=[jax.ShapeDtypeStruct((bsz, l, hw), F32)] * 4
        + [jax.ShapeDtypeStruct((bsz, l, LANE), F32), jax.ShapeDtypeStruct((bsz, 4 * GDN_HEADS, l), F32)],
        scratch_shapes=[pltpu.VMEM((ROW_TILE + 2 * SUBLANE, 3 * hw), F32)],
        compiler_params=_cparams("arbitrary", "arbitrary"),
        name="in_gdn",
    )(x, x, x, modtab, norm_w, w, conv_w, alog_vec, dtb_vec)


def _pair_blockdiag(x):
    c = x.shape[0]
    z = jnp.zeros((c, c), x.dtype)
    return jnp.concatenate([jnp.concatenate([x[:, :c], z], axis=1),
                            jnp.concatenate([z, x[:, c:]], axis=1)], axis=0)


def _gdn_scan_kernel(*refs, reverse, final, n_lat, n_ctx):
    if final:
        q_ref, k_ref, v_ref, gcol_ref, grow_ref, of_ref, z_ref, nw_ref, o_ref, s_ref = refs
    else:
        q_ref, k_ref, v_ref, gcol_ref, grow_ref, o_ref, s_ref = refs
    rows = SCAN_CHUNK
    n_pairs = GDN_HEADS // 2
    pw = 2 * GDN_DK
    chains = [(bb, p) for bb in range(q_ref.shape[0]) for p in range(n_pairs)]
    n_chains = len(chains)
    i = pl.program_id(1)

    @pl.when(i == 0)
    def _():
        s_ref[...] = jnp.zeros_like(s_ref)

    ri = _iota((rows, rows), 0)
    ci = _iota((rows, rows), 1)
    incl = (ri <= ci) if reverse else (ri >= ci)
    ri2 = _iota((rows, pw), 0)
    ci2 = jnp.bitwise_and(_iota((rows, pw), 1), rows - 1)
    strict2 = (ri2 < ci2) if reverse else (ri2 > ci2)
    d = 1 if reverse else 0
    edge = 0 if reverse else rows - 1

    def pair_cols(gcol, lane0):
        return jnp.concatenate([jnp.broadcast_to(gcol[:, lane0 + j:lane0 + j + 1], (rows, GDN_DK))
                                for j in range(2)], axis=1)

    qs, ks, kbs, egs, gcs, decs, rhs, sts = [], [], [], [], [], [], [], []
    for c, (bb, p) in enumerate(chains):
        sl = slice(p * pw, (p + 1) * pw)
        li = d * GDN_HEADS + 2 * p
        gcol = gcol_ref[bb]
        grow = grow_ref[bb]
        q2, k2, v2 = q_ref[bb, :, sl], k_ref[bb, :, sl], v_ref[bb, :, sl]
        gc2 = pair_cols(gcol, li)
        beta2 = pair_cols(gcol, 2 * GDN_HEADS + li)
        eg2 = jnp.exp(gc2)
        kb2 = k2 * beta2
        dec2 = jnp.concatenate(
            [jnp.exp(jnp.where(incl, gcol[:, li + j:li + j + 1] - grow[li + j:li + j + 1, :], NEG_BIG))
             for j in range(2)], axis=1)
        qs.append(q2)
        ks.append(k2)
        kbs.append(kb2)
        egs.append(eg2)
        gcs.append(gc2)
        decs.append(dec2)
        rhs.append([jnp.concatenate([v2[:, j * GDN_DK:(j + 1) * GDN_DK] * beta2[:, j * GDN_DK:(j + 1) * GDN_DK],
                                     kb2[:, j * GDN_DK:(j + 1) * GDN_DK] * eg2[:, j * GDN_DK:(j + 1) * GDN_DK]],
                                    axis=1).astype(BF16) for j in range(2)])
        sts.append(s_ref[c])

    a_mats, attns = [], []
    for p in range(n_chains):
        kq = _dot_nt(jnp.concatenate([kbs[p], qs[p]], axis=0).astype(BF16), _pair_blockdiag(ks[p].astype(BF16)))
        a_mats.append(jnp.where(strict2, kq[:rows] * decs[p], 0.0))
        attns.append((kq[rows:] * decs[p]).astype(BF16))

    half = rows // 2
    rq = _iota((half, pw), 0)
    lq = _iota((half, pw), 1)
    cq = jnp.bitwise_and(lq, half - 1)
    low_half = jnp.bitwise_and(lq, rows - 1) < half
    blk_q = lq // half

    def quad_blockdiag(y):
        return jnp.concatenate([jnp.where(blk_q == b, y, jnp.zeros_like(y)) for b in range(pw // half)], axis=0)

    aqs = [jnp.where(low_half, a_mats[p][:half], a_mats[p][half:]) for p in range(n_chains)]
    tqs = [jnp.where(rq == cq, 1.0, 0.0) - jnp.where((rq // 2) == (cq // 2), aqs[p], 0.0) for p in range(n_chains)]
    s = 2
    while s < half:
        off = jnp.logical_and((rq // (2 * s)) == (cq // (2 * s)), (rq // s) != (cq // s))
        xs = [_dot(jnp.where(off, aqs[p], 0.0).astype(BF16), quad_blockdiag(tqs[p].astype(BF16)))
              for p in range(n_chains)]
        tqs = [tqs[p] - _dot(tqs[p].astype(BF16), quad_blockdiag(xs[p].astype(BF16))) for p in range(n_chains)]
        s *= 2
    ts = [jnp.concatenate([jnp.where(low_half, tqs[p], 0.0), jnp.where(low_half, 0.0, tqs[p])], axis=0)
          for p in range(n_chains)]
    off = (ri2 // half) != (ci2 // half)
    xs = [_dot(jnp.where(off, a_mats[p], 0.0).astype(BF16), _pair_blockdiag(ts[p].astype(BF16)))
          for p in range(n_chains)]
    ts = [ts[p] - _dot(ts[p].astype(BF16), _pair_blockdiag(xs[p].astype(BF16))) for p in range(n_chains)]

    us, ws = [], []
    for p in range(n_chains):
        tb = ts[p].astype(BF16)
        sol = [_dot(tb[:, j * GDN_DK:(j + 1) * GDN_DK], rhs[p][j]) for j in range(2)]
        us.append(jnp.concatenate([sol[0][:, :GDN_DK], sol[1][:, :GDN_DK]], axis=1))
        ws.append(jnp.concatenate([sol[0][:, GDN_DK:], sol[1][:, GDN_DK:]], axis=1))

    vns, outs = [], []
    for p in range(n_chains):
        wq = jnp.concatenate([ws[p], qs[p] * egs[p]], axis=0).astype(BF16)
        ws_qs = _dot(wq, _pair_blockdiag(sts[p].astype(BF16)))
        vns.append((us[p] - ws_qs[:rows]).astype(BF16))
        outs.append(ws_qs[rows:])
    for c, (bb, p) in enumerate(chains):
        o2 = outs[c] + _dot(attns[c], _pair_blockdiag(vns[c]))
        g_end = gcs[c][edge:edge + 1, :]
        k_dec = (ks[c] * jnp.exp(g_end - gcs[c])).astype(BF16)
        full = _dot_tn(k_dec, vns[c])
        upd = jnp.concatenate([full[:GDN_DK, :GDN_DK], full[GDN_DK:, GDN_DK:]], axis=1)
        s_ref[c] = sts[c] * jnp.exp(g_end) + upd
        sl = slice(p * pw, (p + 1) * pw)
        if final:
            o2 = o2 + of_ref[bb, :, sl]
            ys = []
            for j in range(2):
                oj = o2[:, j * GDN_DK:(j + 1) * GDN_DK]
                ys.append(oj * lax.rsqrt(jnp.mean(oj * oj, axis=-1, keepdims=True) + NORM_EPS) * nw_ref[...])
            o_ref[bb, :, sl] = (jnp.concatenate(ys, axis=1) * _silu(z_ref[bb, :, sl])).astype(o_ref.dtype)
        else:
            o_ref[bb, :, sl] = o2


def _gdn_scan(q, k, v, gcol, grow, o_fwd, z, norm_w, n_lat, n_ctx, reverse):
    bsz, l, hw = q.shape
    final = o_fwd is not None
    chunk_fn = functools.partial(_chunk_of_step, n_lat=n_lat, n_ctx=n_ctx, reverse=reverse)
    nb = next(n for n in (4, 2, 1) if bsz % n == 0)
    tok = pl.BlockSpec((nb, SCAN_CHUNK, hw), lambda b, i: (b, chunk_fn(i), 0))
    in_specs = [tok, tok, tok,
                pl.BlockSpec((nb, SCAN_CHUNK, LANE), lambda b, i: (b, chunk_fn(i), 0)),
                pl.BlockSpec((nb, 4 * GDN_HEADS, SCAN_CHUNK), lambda b, i: (b, 0, chunk_fn(i)))]
    args = [q, k, v, gcol, grow]
    if final:
        in_specs += [tok, tok, pl.BlockSpec((1, GDN_DK), lambda b, i: (0, 0))]
        args += [o_fwd, z, norm_w]
    return pl.pallas_call(
        functools.partial(_gdn_scan_kernel, reverse=reverse, final=final, n_lat=n_lat, n_ctx=n_ctx),
        grid=(bsz // nb, l // SCAN_CHUNK),
        in_specs=in_specs,
        out_specs=tok,
        out_shape=jax.ShapeDtypeStruct((bsz, l, hw), BF16 if final else F32),
        scratch_shapes=[pltpu.VMEM((nb * GDN_HEADS // 2, GDN_DK, 2 * GDN_DK), F32)],
        compiler_params=_cparams("arbitrary", "arbitrary"),
        name="gdn_rev" if reverse else "gdn_fwd",
    )(*args)


def _in_ssd_kernel(xl_ref, xlp_ref, xln_ref, xc_ref, xcp_ref, xcn_ref, mod_ref, nw_ref, w_ref, cw_ref, cb_ref,
                   z_ref, u_ref, dt_ref, xe_ref, *, n_lat, n_ctx):
    j = pl.program_id(1)
    first, last = _segment_edges(j, n_lat, n_ctx)
    is_ctx = j >= n_lat
    pick = lambda c_ref, l_ref: jnp.where(is_ctx, c_ref[...], l_ref[...])
    h_main, h_all = _normed_rows(pick(xc_ref, xl_ref), pick(xcp_ref, xlp_ref), pick(xcn_ref, xln_ref),
                                 mod_ref, nw_ref)
    inner = z_ref.shape[1]
    xbc_w = u_ref.shape[1]
    for c0 in range(0, inner, 512):
        z_ref[:, c0:c0 + 512] = _dot(h_main, w_ref[:, c0:c0 + 512])
    for c0, c1, u in _proj_conv(xe_ref, h_all, w_ref, cw_ref, inner, xbc_w, first, last, ROW_TILE):
        u_ref[:, c0:c1] = _silu(u + cb_ref[:, c0:c1])
    dt_ref[...] = _dot(h_main, w_ref[:, inner + xbc_w:inner + xbc_w + LANE])


def _in_ssd(x_lat, x, modtab, norm_w, w, conv_w, conv_b, n_lat, n_ctx):
    bsz, l, d = x.shape
    inner = SSD_HEADS * SSD_HEADDIM
    xbc_w = inner + 2 * SSD_GROUPS * SSD_STATE
    tok = lambda n: pl.BlockSpec((None, ROW_TILE, n), lambda b, j: (b, j, 0))
    lat_tile = lambda j: jnp.minimum(j, n_lat - 1)
    ctx_tile = lambda j: jnp.maximum(j, n_lat)
    lat_specs = [pl.BlockSpec((None, ROW_TILE, d), lambda b, j: (b, lat_tile(j), 0)),
                 *_halo_specs(d, ROW_TILE, x_lat.shape[1], lat_tile)]
    ctx_specs = [pl.BlockSpec((None, ROW_TILE, d), lambda b, j: (b, ctx_tile(j), 0)),
                 *_halo_specs(d, ROW_TILE, l, ctx_tile)]
    return pl.pallas_call(
        functools.partial(_in_ssd_kernel, n_lat=n_lat, n_ctx=n_ctx),
        grid=(bsz, l // ROW_TILE),
        in_specs=lat_specs + ctx_specs + _row_tile_specs(d, l, n_lat)[3:] + [
            pl.BlockSpec((d, w.shape[1]), lambda b, j: (0, 0)),
            pl.BlockSpec((CONV_K, xbc_w), lambda b, j: (0, 0)),
            pl.BlockSpec((1, xbc_w), lambda b, j: (0, 0))],
        out_specs=[tok(inner), tok(xbc_w), tok(LANE)],
        out_shape=[jax.ShapeDtypeStruct((bsz, l, inner), F32), jax.ShapeDtypeStruct((bsz, l, xbc_w), F32),
                   jax.ShapeDtypeStruct((bsz, l, LANE), F32)],
        scratch_shapes=[pltpu.VMEM((ROW_TILE + 2 * SUBLANE, xbc_w), F32)],
        compiler_params=_cparams("arbitrary", "arbitrary"),
        name="in_ssd",
    )(x_lat, x_lat, x_lat, x, x, x, modtab, norm_w, w, conv_w, conv_b)


def _ssd_scan_kernel(*refs, reverse, final):
    if final:
        z_ref, u_ref, dt_ref, dtb_ref, a_ref, dsk_ref, yf_ref, nw_ref, o_ref, st_ref = refs
    else:
        u_ref, dt_ref, dtb_ref, a_ref, dsk_ref, o_ref, st_ref = refs
    rows = SCAN_CHUNK
    inner = SSD_HEADS * SSD_HEADDIM
    gw = inner // SSD_GROUPS
    i = pl.program_id(1)

    @pl.when(i == 0)
    def _():
        st_ref[...] = jnp.zeros_like(st_ref)

    u = u_ref[...]
    xs = u[:, :inner]
    dt = _softplus(dt_ref[...] + dtb_ref[...])
    adt = a_ref[...] * dt

    ri = _iota((rows, rows), 0)
    ci = _iota((rows, rows), 1)
    incl = (ri <= ci) if reverse else (ri >= ci)
    tri = jnp.where(incl, 1.0, 0.0).astype(BF16)
    eye = jnp.where(ri == ci, 1.0, 0.0).astype(BF16)
    acs = _exact_left(tri, adt)
    acs_t = _exact_transpose(acs, eye)
    d = 1 if reverse else 0
    edge = 0 if reverse else rows - 1
    er = _iota((LANE, inner), 0)
    ec = _iota((LANE, inner), 1)
    expand = jnp.where(er == d * SSD_HEADS + ec // SSD_HEADDIM, 1.0, 0.0).astype(BF16)
    dt_e = _exact_right(dt, expand)
    ea_e = _exact_right(jnp.exp(acs), expand)
    dc_e = _exact_right(jnp.exp(acs[edge:edge + 1, :] - acs), expand)
    xdt = xs * dt_e
    xdt_b = xdt.astype(BF16)
    xw_b = (xdt * dc_e).astype(BF16)
    lane = _iota((rows, LANE), 1)

    y_parts = []
    for g in range(SSD_GROUPS):
        bm = u[:, inner + g * SSD_STATE:inner + (g + 1) * SSD_STATE].astype(BF16)
        cm = u[:, inner + (SSD_GROUPS + g) * SSD_STATE:inner + (SSD_GROUPS + g + 1) * SSD_STATE].astype(BF16)
        cb = _dot_nt(cm, bm)
        st = st_ref[g]
        y_off = _dot(cm, st.astype(BF16)) * ea_e[:, g * gw:(g + 1) * gw]
        heads_per_group = SSD_HEADS // SSD_GROUPS
        diag = []
        for hp in range(heads_per_group // 2):
            h0 = g * heads_per_group + 2 * hp
            mats = []
            for hh in (h0, h0 + 1):
                li = d * SSD_HEADS + hh
                lm = jnp.exp(jnp.where(incl, acs[:, li:li + 1] - acs_t[li:li + 1, :], NEG_BIG))
                mats.append((cb * lm).astype(BF16))
            xp = xdt_b[:, h0 * SSD_HEADDIM:(h0 + 2) * SSD_HEADDIM]
            x_lo = jnp.where(lane < SSD_HEADDIM, xp, jnp.zeros_like(xp))
            x_hi = jnp.where(lane >= SSD_HEADDIM, xp, jnp.zeros_like(xp))
            diag.append(_dot(jnp.concatenate(mats, axis=1), jnp.concatenate([x_lo, x_hi], axis=0)))
        y_parts.append(jnp.concatenate(diag, axis=1) + y_off)
        st_ref[g] = (st * ea_e[edge:edge + 1, g * gw:(g + 1) * gw]
                     + _dot_tn(bm, xw_b[:, g * gw:(g + 1) * gw]))
    y = jnp.concatenate(y_parts, axis=1) + dsk_ref[...] * xs
    if final:
        y = y + yf_ref[...]
        yz = y * _silu(z_ref[...])
        outs = []
        for g in range(SSD_GROUPS):
            yg = yz[:, g * gw:(g + 1) * gw]
            outs.append(yg * lax.rsqrt(jnp.mean(yg * yg, axis=-1, keepdims=True) + NORM_EPS))
        o_ref[...] = (jnp.concatenate(outs, axis=1) * nw_ref[...]).astype(o_ref.dtype)
    else:
        o_ref[...] = y


def _ssd_scan(z, u, dt_raw, dtb_vec, a_vec, dskip, y_fwd, norm_w, n_lat, n_ctx, reverse):
    bsz, l, xbc_w = u.shape
    inner = SSD_HEADS * SSD_HEADDIM
    final = y_fwd is not None
    chunk_fn = functools.partial(_chunk_of_step, n_lat=n_lat, n_ctx=n_ctx, reverse=reverse)
    tok = lambda n: pl.BlockSpec((None, SCAN_CHUNK, n), lambda b, i: (b, chunk_fn(i), 0))
    row = lambda n: pl.BlockSpec((1, n), lambda b, i: (0, 0))
    in_specs = [tok(xbc_w), tok(LANE), row(LANE), row(LANE), row(inner)]
    args = [u, dt_raw, dtb_vec, a_vec, dskip]
    if final:
        in_specs = [tok(inner)] + in_specs + [tok(inner), row(inner)]
        args = [z] + args + [y_fwd, norm_w]
    return pl.pallas_call(
        functools.partial(_ssd_scan_kernel, reverse=reverse, final=final),
        grid=(bsz, l // SCAN_CHUNK),
        in_specs=in_specs,
        out_specs=tok(inner),
        out_shape=jax.ShapeDtypeStruct((bsz, l, inner), BF16 if final else F32),
        scratch_shapes=[pltpu.VMEM((SSD_GROUPS, SSD_STATE, inner // SSD_GROUPS), F32)],
        compiler_params=_cparams("arbitrary", "arbitrary"),
        name="ssd_rev" if reverse else "ssd_fwd",
    )(*args)


def _merge_kernel(x_ref, lru_ref, gdn_ref, ssdl_ref, ssdc_ref, gate_ref, mod_ref, nw_ref, wb_ref, wo_ref, wr_ref,
                  br_ref, xo_ref, h_ref, lg_ref, *, n_lat):
    d = x_ref.shape[1]
    ssd = jnp.where(pl.program_id(1) >= n_lat, ssdc_ref[...], ssdl_ref[...])
    acc = None
    for k, b in enumerate((lru_ref[...], gdn_ref[...], ssd)):
        t = jax.nn.sigmoid(gate_ref[:, k * d:(k + 1) * d].astype(F32)) * _dot(b.astype(BF16), wb_ref[k])
        acc = t if acc is None else acc + t
    mix = _dot(acc.astype(BF16), wo_ref[...])
    x = x_ref[...] + mod_ref[2:3, :] * mix
    xo_ref[...] = x
    h = _mod_norm(x, nw_ref[...], mod_ref[3:4, :], mod_ref[4:5, :])
    hb = h.astype(BF16)
    h_ref[...] = hb
    lg_ref[...] = _dot(hb, wr_ref[...]) + br_ref[...]


def _merge(x, lru, gdn, ssd_lat, ssd_scan, gate, modtab, norm_w, wb, wo, wr, br, n_lat_tiles):
    bsz, l, d = x.shape
    tok = pl.BlockSpec((None, ROW_TILE, d), lambda b, j: (b, j, 0))
    return pl.pallas_call(
        functools.partial(_merge_kernel, n_lat=n_lat_tiles),
        grid=(bsz, l // ROW_TILE),
        in_specs=[tok, tok, tok,
                  pl.BlockSpec((None, ROW_TILE, d), lambda b, j: (b, jnp.minimum(j, n_lat_tiles - 1), 0)),
                  pl.BlockSpec((None, ROW_TILE, d), lambda b, j: (b, jnp.maximum(j, n_lat_tiles), 0)),
                  pl.BlockSpec((None, ROW_TILE, 3 * d), lambda b, j: (b, j, 0)),
                  pl.BlockSpec((None, None, SUBLANE, d), lambda b, j: (b, jnp.where(j >= n_lat_tiles, 1, 0), 0, 0)),
                  pl.BlockSpec((1, d), lambda b, j: (0, 0)),
                  pl.BlockSpec((3, d, d), lambda b, j: (0, 0, 0)),
                  pl.BlockSpec((d, d), lambda b, j: (0, 0)),
                  pl.BlockSpec((d, LANE), lambda b, j: (0, 0)),
                  pl.BlockSpec((1, LANE), lambda b, j: (0, 0))],
        out_specs=[tok, tok, pl.BlockSpec((None, ROW_TILE, LANE), lambda b, j: (b, j, 0))],
        out_shape=[jax.ShapeDtypeStruct((bsz, l, d), F32), jax.ShapeDtypeStruct((bsz, l, d), BF16),
                   jax.ShapeDtypeStruct((bsz, l, LANE), F32)],
        compiler_params=_cparams("arbitrary", "arbitrary"),
        name="merge",
    )(x, lru, gdn, ssd_lat, ssd_scan, gate, modtab, norm_w, wb, wo, wr, br)


def _moe_kernel(blk_ref, exp_ref, lo_ref, hi_ref, x_ref, w1_ref, w3_ref, w2_ref, o_ref,
                acc_ref, w1b_ref, w3b_ref, w2b_ref):
    i = pl.program_id(0)
    lo = lo_ref[i]
    hi = hi_ref[i]
    base = blk_ref[i] * MOE_ROWS

    @pl.when(jnp.logical_or(i == 0, exp_ref[i] != exp_ref[jnp.maximum(i - 1, 0)]))
    def _():
        w1b_ref[...] = w1_ref[...].astype(BF16)
        w3b_ref[...] = w3_ref[...].astype(BF16)
        w2b_ref[...] = w2_ref[...].astype(BF16)

    @pl.when(lo == base)
    def _():
        acc_ref[...] = jnp.zeros_like(acc_ref)

    @pl.when(hi > lo)
    def _():
        row = base + _iota((MOE_ROWS, 1), 0)
        x = jnp.where(jnp.logical_and(row >= lo, row < hi), x_ref[...], jnp.zeros_like(x_ref))
        mid = _silu(_dot(x, w1b_ref[...])) * _dot(x, w3b_ref[...])
        acc_ref[...] += _dot(mid.astype(BF16), w2b_ref[...])

    o_ref[...] = acc_ref[...].astype(o_ref.dtype)


def _moe_experts(xb, item_block, item_expert, item_lo, item_hi, w1, w3, w2, layer):
    n_rows, d = xb.shape
    ff = w1.shape[-1]
    rows_spec = pl.BlockSpec((MOE_ROWS, d), lambda i, blk, ex, lo, hi: (blk[i], 0))
    grid_spec = pltpu.PrefetchScalarGridSpec(
        num_scalar_prefetch=4,
        grid=(item_block.shape[0],),
        in_specs=[rows_spec,
                  pl.BlockSpec((None, None, d, ff), lambda i, blk, ex, lo, hi: (layer, ex[i], 0, 0)),
                  pl.BlockSpec((None, None, d, ff), lambda i, blk, ex, lo, hi: (layer, ex[i], 0, 0)),
                  pl.BlockSpec((None, None, ff, d), lambda i, blk, ex, lo, hi: (layer, ex[i], 0, 0))],
        out_specs=rows_spec,
        scratch_shapes=[pltpu.VMEM((MOE_ROWS, d), F32), pltpu.VMEM((d, ff), BF16), pltpu.VMEM((d, ff), BF16),
                        pltpu.VMEM((ff, d), BF16)],
    )
    return pl.pallas_call(
        _moe_kernel,
        grid_spec=grid_spec,
        out_shape=jax.ShapeDtypeStruct((n_rows, d), BF16),
        compiler_params=_cparams("arbitrary"),
        name="moe_experts",
    )(item_block, item_expert, item_lo, item_hi, xb, w1, w3, w2)


def _route(logits):
    t = logits.shape[0]
    g_logits = logits[:, :N_GROUPS]
    e_logits = logits[:, N_GROUPS:N_GROUPS + N_EXPERTS].reshape(t, N_GROUPS, EXPERTS_PER_GROUP)
    g_prob = jax.nn.softmax(g_logits, axis=-1)
    g_idx = jnp.argmax(g_logits, axis=-1)
    p_group = jnp.take_along_axis(g_prob, g_idx[:, None], axis=1)[:, 0]
    e_in_group = jnp.take_along_axis(e_logits, g_idx[:, None, None], axis=1)[:, 0]
    top_v, top_i = lax.top_k(e_in_group, TOP_K)
    weights = jax.nn.softmax(top_v, axis=-1) * p_group[:, None]
    expert_id = (g_idx[:, None] * EXPERTS_PER_GROUP + top_i).reshape(-1).astype(jnp.int32)
    n_assign = t * TOP_K
    assert n_assign % MOE_ROWS == 0
    n_blocks = n_assign // MOE_ROWS
    ar = jnp.arange(n_assign, dtype=jnp.int32)
    e_sorted, order = lax.sort((expert_id, ar), num_keys=1)
    _, slot = lax.sort((order, ar), num_keys=1)
    ends = jnp.sum(e_sorted[None, :] <= jnp.arange(N_EXPERTS, dtype=jnp.int32)[:, None], axis=1).astype(jnp.int32)
    starts = jnp.concatenate([jnp.zeros((1,), jnp.int32), ends[:-1]])
    cuts = jnp.sort(jnp.concatenate([jnp.arange(n_blocks, dtype=jnp.int32) * MOE_ROWS, starts]))
    item_lo = cuts
    item_hi = jnp.concatenate([cuts[1:], jnp.full((1,), n_assign, jnp.int32)])
    item_block = jnp.minimum(item_lo // MOE_ROWS, n_blocks - 1).astype(jnp.int32)
    item_expert = jnp.minimum(jnp.sum(item_lo[:, None] >= ends[None, :], axis=1), N_EXPERTS - 1).astype(jnp.int32)
    return order // TOP_K, (item_block, item_expert, item_lo, item_hi), slot.reshape(t, TOP_K), weights


def _combine_kernel(x_ref, ya_ref, yb_ref, wt_ref, mod_ref, nw_ref, o_ref, *, final):
    wt = wt_ref[...]
    y = wt[:, 0:1] * ya_ref[...].astype(F32) + wt[:, 1:2] * yb_ref[...].astype(F32)
    x = x_ref[...] + mod_ref[5:6, :] * y
    if final:
        x = x * lax.rsqrt(jnp.mean(x * x, axis=-1, keepdims=True) + NORM_EPS) * nw_ref[...]
    o_ref[...] = x


def _combine(x, ya, yb, wts, modtab, norm_w, n_lat_tiles, final, out_len):
    bsz, l, d = x.shape
    tok = pl.BlockSpec((None, ROW_TILE, d), lambda b, j: (b, j, 0))
    return pl.pallas_call(
        functools.partial(_combine_kernel, final=final),
        grid=(bsz, out_len // ROW_TILE),
        in_specs=[tok, tok, tok,
                  pl.BlockSpec((None, ROW_TILE, LANE), lambda b, j: (b, j, 0)),
                  pl.BlockSpec((None, None, SUBLANE, d), lambda b, j: (b, jnp.where(j >= n_lat_tiles, 1, 0), 0, 0)),
                  pl.BlockSpec((1, d), lambda b, j: (0, 0))],
        out_specs=tok,
        out_shape=jax.ShapeDtypeStruct((bsz, out_len, d), F32),
        compiler_params=_cparams("arbitrary", "arbitrary"),
        name="combine",
    )(x, ya, yb, wts, modtab, norm_w)


def _pad_lanes(v, n=LANE):
    v = v.reshape(1, -1).astype(F32)
    return jnp.pad(v, ((0, 0), (0, n - v.shape[1])))


def kernel(x, c, ctx, c_ctx, w_mod, b_mod, norm1_w, norm2_w, w_in, lru_conv_w, lru_conv_b, lru_wa, lru_ba,
           lru_wi, lru_bi, lru_lambda, gdn_conv_w, gdn_a_log, gdn_dt_bias, gdn_norm_w, ssd_conv_w, ssd_conv_b,
           ssd_a_log, ssd_dt_bias, ssd_d, ssd_norm_w, w_branch, w_out, router_group_w, router_group_b,
           router_expert_w, router_expert_b, expert_w1, expert_w3, expert_w2, final_norm_w):
    bsz, seq, d = x.shape
    n_ctx_tok = ctx.shape[1]
    depth = w_mod.shape[0]
    l = seq + n_ctx_tok
    rows = seq // GRID_W
    assert seq % LRU_CHUNK == 0 and n_ctx_tok % LRU_CHUNK == 0 and ROW_TILE == LRU_CHUNK
    assert bsz + 1 <= SUBLANE
    assert l % GRID_W == 0 and l % rows == 0
    hw = GDN_HEADS * GDN_DK
    inner = SSD_HEADS * SSD_HEADDIM
    xbc_w = inner + 2 * SSD_GROUPS * SSD_STATE

    act = jnp.concatenate([_silu(c), _silu(c_ctx)[None, :],
                           jnp.zeros((SUBLANE - bsz - 1, d), F32)], axis=0)
    mod_all = _modulation(act, w_mod, b_mod).reshape(depth, SUBLANE, 6, d)
    pad2 = jnp.zeros((bsz, 2, d), F32)

    xs = jnp.concatenate([x, ctx], axis=1)
    nl_t, nc_t = seq // ROW_TILE, n_ctx_tok // ROW_TILE
    nl_s, nc_s = seq // SCAN_CHUNK, n_ctx_tok // SCAN_CHUNK

    o_lx, o_qkv, o_gz, o_gb, o_ga = 0, 2 * d, 2 * d + 3 * hw, 2 * d + 4 * hw, 2 * d + 4 * hw + 2 * GDN_HEADS
    o_sz = o_ga + 2 * GDN_HEADS
    o_xbc = o_sz + inner
    o_sdt = o_xbc + xbc_w
    o_gate = o_sdt + 2 * SSD_HEADS

    for i in range(depth):
        lat = mod_all[i, :bsz]
        cx = jnp.broadcast_to(mod_all[i, bsz][None], (bsz, 6, d))
        modtab = jnp.stack([jnp.concatenate([lat, pad2], axis=1), jnp.concatenate([cx, pad2], axis=1)], axis=1)

        wi_ = w_in[i]
        w_lru = wi_[:, o_lx:o_qkv].astype(BF16)
        zpad = jnp.zeros((d, LANE - 4 * GDN_HEADS), F32)
        w_gdn = jnp.concatenate([wi_[:, o_qkv:o_gb], wi_[:, o_ga:o_sz], wi_[:, o_gb:o_ga], zpad], axis=1).astype(BF16)
        zpad2 = jnp.zeros((d, LANE - 2 * SSD_HEADS), F32)
        w_ssd = jnp.concatenate([wi_[:, o_sz:o_gate], zpad2], axis=1).astype(BF16)
        w_gate = wi_[:, o_gate:].astype(BF16)
        n1 = norm1_w[i].reshape(1, d)

        gate = _norm_matmul(xs, modtab, n1, w_gate, 0, 1, nl_t, "in_gate")

        lru_u, lru_y = _in_lru(xs, modtab, n1, w_lru, lru_conv_w[i], lru_conv_b[i].reshape(1, d), nl_t, nc_t)
        h_f = _lru_scan(lru_u, lru_y, lru_wa[i, 0].astype(BF16), lru_ba[i, 0].reshape(1, d),
                        lru_wi[i, 0].astype(BF16), lru_bi[i, 0].reshape(1, d), lru_lambda[i, 0].reshape(1, d),
                        None, nl_t, nc_t, False)
        lru_out = _lru_scan(lru_u, lru_y, lru_wa[i, 1].astype(BF16), lru_ba[i, 1].reshape(1, d),
                            lru_wi[i, 1].astype(BF16), lru_bi[i, 1].reshape(1, d), lru_lambda[i, 1].reshape(1, d),
                            h_f, nl_t, nc_t, True)

        q, k, v, gz, gcol, grow = _in_gdn(xs, modtab, n1, w_gdn, gdn_conv_w[i], _pad_lanes(gdn_a_log[i]),
                                          _pad_lanes(gdn_dt_bias[i]), nl_t, nc_t)
        o_f = _gdn_scan(q, k, v, gcol, grow, None, None, None, nl_s, nc_s, False)
        gdn_out = _gdn_scan(q, k, v, gcol, grow, o_f, gz, gdn_norm_w[i].reshape(1, GDN_DK), nl_s, nc_s, True)

        x_cm = xs.reshape(bsz, l // GRID_W, GRID_W, d)[:, :rows].swapaxes(1, 2).reshape(bsz, seq, d)
        sz, su, sdt = _in_ssd(x_cm, xs, modtab, n1, w_ssd, ssd_conv_w[i], ssd_conv_b[i].reshape(1, xbc_w),
                              nl_t, nc_t)
        dtb_vec = _pad_lanes(ssd_dt_bias[i])
        a_vec = _pad_lanes(-jnp.exp(ssd_a_log[i].astype(F32)))
        dsk = [jnp.repeat(ssd_d[i, dd], SSD_HEADDIM).reshape(1, inner) for dd in range(2)]
        y_f = _ssd_scan(None, su, sdt, dtb_vec, a_vec, dsk[0], None, None, nl_s, nc_s, False)
        ssd_scan_out = _ssd_scan(sz, su, sdt, dtb_vec, a_vec, dsk[1], y_f,
                                 ssd_norm_w[i].reshape(1, inner), nl_s, nc_s, True)
        ssd_lat = (ssd_scan_out.reshape(bsz, l // rows, rows, inner)[:, :GRID_W].swapaxes(1, 2)
                   .reshape(bsz, seq, inner))

        w_r = jnp.concatenate([router_group_w[i], router_expert_w[i],
                               jnp.zeros((d, LANE - N_GROUPS - N_EXPERTS), F32)], axis=1).astype(BF16)
        b_r = _pad_lanes(jnp.concatenate([router_group_b[i], router_expert_b[i]]))
        x_mid, h2, logits = _merge(xs, lru_out, gdn_out, ssd_lat, ssd_scan_out, gate, modtab, norm2_w[i].reshape(1, d),
                                   w_branch[i].astype(BF16), w_out[i].astype(BF16), w_r, b_r, nl_t)

        t = bsz * l
        tok_sorted, items, slot, weights = _route(logits.reshape(t, LANE))
        xb = h2.reshape(t, d)[tok_sorted]
        yb = _moe_experts(xb, *items, expert_w1, expert_w3, expert_w2, i)
        y0 = yb[slot[:, 0]].reshape(bsz, l, d)
        y1 = yb[slot[:, 1]].reshape(bsz, l, d)
        wts = jnp.pad(weights, ((0, 0), (0, LANE - TOP_K))).reshape(bsz, l, LANE)
        last = i == depth - 1
        xs = _combine(x_mid, y0, y1, wts, modtab, final_norm_w.reshape(1, d), nl_t, last, seq if last else l)

    return xs
```

```python
import functools

import jax
import jax.numpy as jnp
from jax import lax
from jax.experimental import pallas as pl
from jax.experimental.pallas import tpu as pltpu

GRID_W = 64
CONV_K = 4
NORM_EPS = 1e-6
LRU_BLOCKS = 8
LRU_C = 8.0
GDN_HEADS = 8
GDN_DK = 128
SSD_HEADS = 16
SSD_HEADDIM = 64
SSD_GROUPS = 2
SSD_STATE = 128
N_GROUPS = 4
EXPERTS_PER_GROUP = 8
N_EXPERTS = N_GROUPS * EXPERTS_PER_GROUP
TOP_K = 2

LANE = 128
SUBLANE = 8
ROW_TILE = 256
LRU_CHUNK = 256
SCAN_CHUNK = 128
MOE_ROWS = 512
VMEM_LIMIT = 56 * 1024 * 1024

F32 = jnp.float32
BF16 = jnp.bfloat16
NEG_BIG = -1e30


def _cparams(*sem):
    return pltpu.CompilerParams(dimension_semantics=sem, vmem_limit_bytes=VMEM_LIMIT)


def _dot(a, b):
    return jnp.dot(a, b, preferred_element_type=F32)


def _dot_nt(a, b):
    return lax.dot_general(a, b, (((1,), (1,)), ((), ())), preferred_element_type=F32)


def _dot_tn(a, b):
    return lax.dot_general(a, b, (((0,), (0,)), ((), ())), preferred_element_type=F32)


def _split3(x):
    hi = x.astype(BF16)
    r1 = x - hi.astype(F32)
    mid = r1.astype(BF16)
    lo = (r1 - mid.astype(F32)).astype(BF16)
    return hi, mid, lo


def _exact_left(m_bf, x):
    hi, mid, lo = _split3(x)
    return _dot(jnp.concatenate([m_bf, m_bf, m_bf], axis=1), jnp.concatenate([hi, mid, lo], axis=0))


def _exact_right(x, m_bf):
    hi, mid, lo = _split3(x)
    return _dot(jnp.concatenate([hi, mid, lo], axis=1), jnp.concatenate([m_bf, m_bf, m_bf], axis=0))


def _exact_transpose(x, eye_bf):
    hi, mid, lo = _split3(x)
    return _dot_nt(jnp.concatenate([eye_bf, eye_bf, eye_bf], axis=1), jnp.concatenate([hi, mid, lo], axis=1))


def _softplus(x):
    return jnp.maximum(x, 0.0) + jnp.log(1.0 + jnp.exp(-jnp.abs(x)))


def _silu(x):
    return x * jax.nn.sigmoid(x)


def _iota(shape, dim):
    return lax.broadcasted_iota(jnp.int32, shape, dim)


def _fwd_chunk(i, n_lat, n_ctx):
    return jnp.where(i < n_ctx, n_lat + i, i - n_ctx)


def _chunk_of_step(i, n_lat, n_ctx, reverse):
    return (n_lat + n_ctx - 1 - i) if reverse else _fwd_chunk(i, n_lat, n_ctx)


def _segment_edges(c, n_lat, n_ctx):
    first = jnp.logical_or(c == 0, c == n_lat)
    last = jnp.logical_or(c == n_lat - 1, c == n_lat + n_ctx - 1)
    return first, last


def _normed_rows(x, x_prev, x_next, mod_ref, nw_ref):
    shift, scale = mod_ref[0:1, :], mod_ref[1:2, :]
    h_main = _mod_norm(x, nw_ref[...], shift, scale)
    h_prev = _mod_norm(x_prev, nw_ref[...], shift, scale)
    h_next = _mod_norm(x_next, nw_ref[...], shift, scale)
    return h_main.astype(BF16), jnp.concatenate([h_prev, h_main, h_next], axis=0).astype(BF16)


def _proj_conv(xe_ref, h_all, w_ref, cw_ref, col0, width, first, last, rows, col_step=512):
    base = SUBLANE - CONV_K // 2
    for c0 in range(0, width, col_step):
        c1 = min(c0 + col_step, width)
        y = _dot(h_all, w_ref[:, col0 + c0:col0 + c1])
        xe_ref[0:SUBLANE, c0:c1] = jnp.where(first, 0.0, y[0:SUBLANE])
        xe_ref[SUBLANE:SUBLANE + rows, c0:c1] = y[SUBLANE:SUBLANE + rows]
        xe_ref[SUBLANE + rows:2 * SUBLANE + rows, c0:c1] = jnp.where(last, 0.0, y[SUBLANE + rows:])
        u = cw_ref[0:1, c0:c1] * xe_ref[base:base + rows, c0:c1]
        for j in range(1, CONV_K):
            u = u + cw_ref[j:j + 1, c0:c1] * xe_ref[base + j:base + j + rows, c0:c1]
        yield c0, c1, u


def _halo_specs(width, rows, n_rows_total, chunk_fn, col_block=0):
    per = rows // SUBLANE
    last_tile = n_rows_total // SUBLANE - 1
    prev = pl.BlockSpec((None, SUBLANE, width),
                        lambda b, i: (b, jnp.maximum(chunk_fn(i) * per - 1, 0), col_block))
    nxt = pl.BlockSpec((None, SUBLANE, width),
                       lambda b, i: (b, jnp.minimum((chunk_fn(i) + 1) * per, last_tile), col_block))
    return prev, nxt


def _mod_kernel(a_ref, w_ref, b_ref, o_ref):
    o_ref[...] = _dot(a_ref[...].astype(BF16), w_ref[...].astype(BF16)) + b_ref[...]


def _modulation(act, w_mod, b_mod):
    depth, d, n = w_mod.shape
    tn = 1536
    return pl.pallas_call(
        _mod_kernel,
        grid=(depth, n // tn),
        in_specs=[pl.BlockSpec((SUBLANE, d), lambda l, j: (0, 0)),
                  pl.BlockSpec((None, d, tn), lambda l, j: (l, 0, j)),
                  pl.BlockSpec((None, 1, tn), lambda l, j: (l, 0, j))],
        out_specs=pl.BlockSpec((None, SUBLANE, tn), lambda l, j: (l, 0, j)),
        out_shape=jax.ShapeDtypeStruct((depth, SUBLANE, n), F32),
        compiler_params=_cparams("arbitrary", "arbitrary"),
        name="modulation",
    )(act, w_mod, b_mod.reshape(depth, 1, n))


def _mod_norm(x, nw, shift, scale):
    y = x * lax.rsqrt(jnp.mean(x * x, axis=-1, keepdims=True) + NORM_EPS)
    return (y * nw) * (1.0 + scale) + shift


def _norm_matmul_kernel(x_ref, mod_ref, nw_ref, w_ref, o_ref, *, shift_row, scale_row, col_step):
    h = _mod_norm(x_ref[...], nw_ref[...], mod_ref[shift_row:shift_row + 1, :],
                  mod_ref[scale_row:scale_row + 1, :]).astype(BF16)
    n = w_ref.shape[1]
    for c0 in range(0, n, col_step):
        c1 = min(c0 + col_step, n)
        o_ref[:, c0:c1] = _dot(h, w_ref[:, c0:c1]).astype(o_ref.dtype)


def _norm_matmul(x, modtab, norm_w, w, shift_row, scale_row, n_lat_tiles, name):
    bsz, l, d = x.shape
    n = w.shape[1]
    kern = functools.partial(_norm_matmul_kernel, shift_row=shift_row, scale_row=scale_row, col_step=512)
    return pl.pallas_call(
        kern,
        grid=(bsz, l // ROW_TILE),
        in_specs=[pl.BlockSpec((None, ROW_TILE, d), lambda b, j: (b, j, 0)),
                  pl.BlockSpec((None, None, SUBLANE, d), lambda b, j: (b, jnp.where(j >= n_lat_tiles, 1, 0), 0, 0)),
                  pl.BlockSpec((1, d), lambda b, j: (0, 0)),
                  pl.BlockSpec((d, n), lambda b, j: (0, 0))],
        out_specs=pl.BlockSpec((None, ROW_TILE, n), lambda b, j: (b, j, 0)),
        out_shape=jax.ShapeDtypeStruct((bsz, l, n), BF16),
        compiler_params=_cparams("arbitrary", "arbitrary"),
        name=name,
    )(x, modtab, norm_w, w)


def _row_tile_specs(d, l, n_lat_tiles):
    ident = lambda j: j
    main = pl.BlockSpec((None, ROW_TILE, d), lambda b, j: (b, j, 0))
    prev, nxt = _halo_specs(d, ROW_TILE, l, ident)
    mod = pl.BlockSpec((None, None, SUBLANE, d), lambda b, j: (b, jnp.where(j >= n_lat_tiles, 1, 0), 0, 0))
    return [main, prev, nxt, mod, pl.BlockSpec((1, d), lambda b, j: (0, 0))]


def _in_lru_kernel(x_ref, xp_ref, xn_ref, mod_ref, nw_ref, w_ref, cw_ref, cb_ref, u_ref, y_ref, xe_ref,
                   *, n_lat, n_ctx):
    first, last = _segment_edges(pl.program_id(1), n_lat, n_ctx)
    h_main, h_all = _normed_rows(x_ref[...], xp_ref[...], xn_ref[...], mod_ref, nw_ref)
    w = u_ref.shape[1]
    for c0, c1, u in _proj_conv(xe_ref, h_all, w_ref, cw_ref, 0, w, first, last, ROW_TILE):
        u_ref[:, c0:c1] = u + cb_ref[:, c0:c1]
    for c0 in range(0, w, 512):
        y_ref[:, c0:c0 + 512] = _dot(h_main, w_ref[:, w + c0:w + c0 + 512])


def _in_lru(x, modtab, norm_w, w, conv_w, conv_b, n_lat, n_ctx):
    bsz, l, d = x.shape
    wd = w.shape[1] // 2
    tok = pl.BlockSpec((None, ROW_TILE, wd), lambda b, j: (b, j, 0))
    return pl.pallas_call(
        functools.partial(_in_lru_kernel, n_lat=n_lat, n_ctx=n_ctx),
        grid=(bsz, l // ROW_TILE),
        in_specs=_row_tile_specs(d, l, n_lat) + [pl.BlockSpec((d, 2 * wd), lambda b, j: (0, 0)),
                                                 pl.BlockSpec((CONV_K, wd), lambda b, j: (0, 0)),
                                                 pl.BlockSpec((1, wd), lambda b, j: (0, 0))],
        out_specs=[tok, tok],
        out_shape=[jax.ShapeDtypeStruct((bsz, l, wd), F32)] * 2,
        scratch_shapes=[pltpu.VMEM((ROW_TILE + 2 * SUBLANE, wd), F32)],
        compiler_params=_cparams("arbitrary", "arbitrary"),
        name="in_lru",
    )(x, x, x, modtab, norm_w, w, conv_w, conv_b)


def _lru_kernel(*refs, reverse, final):
    if final:
        u_ref, wa_ref, ba_ref, wi_ref, bi_ref, lam_ref, hf_ref, y_ref, o_ref, carry_ref = refs
    else:
        u_ref, wa_ref, ba_ref, wi_ref, bi_ref, lam_ref, o_ref, carry_ref = refs
    rows = LRU_CHUNK
    i = pl.program_id(1)

    @pl.when(i == 0)
    def _():
        carry_ref[...] = jnp.zeros_like(carry_ref)

    u = u_ref[...]
    ub = u.astype(BF16)
    bw = u.shape[1] // LRU_BLOCKS
    pre_r = jnp.concatenate([_dot(ub[:, n * bw:(n + 1) * bw], wa_ref[n]) for n in range(LRU_BLOCKS)], axis=1)
    pre_i = jnp.concatenate([_dot(ub[:, n * bw:(n + 1) * bw], wi_ref[n]) for n in range(LRU_BLOCKS)], axis=1)
    r = jax.nn.sigmoid(pre_r + ba_ref[...])
    gi = jax.nn.sigmoid(pre_i + bi_ref[...])
    a = jnp.exp(-LRU_C * r * _softplus(-lam_ref[...]))
    h = jnp.sqrt(1.0 - a * a) * gi * u

    srow = _iota((SUBLANE, a.shape[1]), 0)
    n_tiles = rows // SUBLANE
    carry = carry_ref[...]
    done = {}
    for k in (range(n_tiles - 1, -1, -1) if reverse else range(n_tiles)):
        sl = slice(k * SUBLANE, (k + 1) * SUBLANE)
        at, ht = a[sl], h[sl]
        for s in (1, 2, 4):
            if reverse:
                keep = srow < SUBLANE - s
                sh = SUBLANE - s
            else:
                keep = srow >= s
                sh = s
            h_s = jnp.where(keep, pltpu.roll(ht, sh, axis=0), 0.0)
            a_s = jnp.where(keep, pltpu.roll(at, sh, axis=0), 1.0)
            ht = ht + at * h_s
            at = at * a_s
        hk = ht + at * carry
        carry = hk[0:1, :] if reverse else hk[SUBLANE - 1:SUBLANE, :]
        if final:
            done[k] = (hk + hf_ref[sl, :]) * jax.nn.gelu(y_ref[sl, :], approximate=True)
            if (k ^ 1) in done:
                lo = k & ~1
                pair = jnp.concatenate([done.pop(lo), done.pop(lo + 1)], axis=0)
                o_ref[lo * SUBLANE:(lo + 2) * SUBLANE, :] = pair.astype(o_ref.dtype)
        else:
            o_ref[sl, :] = hk
    carry_ref[...] = carry


def _lru_scan(u, y, wa, ba, wi, bi, lam, hf, n_lat, n_ctx, reverse):
    bsz, l, w = u.shape
    final = hf is not None
    chunk_fn = functools.partial(_chunk_of_step, n_lat=n_lat, n_ctx=n_ctx, reverse=reverse)
    main = pl.BlockSpec((None, LRU_CHUNK, w), lambda b, i: (b, chunk_fn(i), 0))
    vec = pl.BlockSpec((1, w), lambda b, i: (0, 0))
    blk = pl.BlockSpec((LRU_BLOCKS, w // LRU_BLOCKS, w // LRU_BLOCKS), lambda b, i: (0, 0, 0))
    in_specs = [main, blk, vec, blk, vec, vec]
    args = [u, wa, ba, wi, bi, lam]
    if final:
        in_specs += [main, main]
        args += [hf, y]
    return pl.pallas_call(
        functools.partial(_lru_kernel, reverse=reverse, final=final),
        grid=(bsz, l // LRU_CHUNK),
        in_specs=in_specs,
        out_specs=main,
        out_shape=jax.ShapeDtypeStruct((bsz, l, w), BF16 if final else F32),
        scratch_shapes=[pltpu.VMEM((1, w), F32)],
        compiler_params=_cparams("arbitrary", "arbitrary"),
        name="lru_rev" if reverse else "lru_fwd",
    )(*args)


def _in_gdn_kernel(x_ref, xp_ref, xn_ref, mod_ref, nw_ref, w_ref, cw_ref, alog_ref, dtb_ref,
                   q_ref, k_ref, v_ref, z_ref, gcol_ref, grow_ref, xe_ref, *, n_lat, n_ctx):
    rows = ROW_TILE
    first, last = _segment_edges(pl.program_id(1), n_lat, n_ctx)
    h_main, h_all = _normed_rows(x_ref[...], xp_ref[...], xn_ref[...], mod_ref, nw_ref)
    hw = GDN_HEADS * GDN_DK
    for c0, c1, u in _proj_conv(xe_ref, h_all, w_ref, cw_ref, 0, 3 * hw, first, last, rows):
        u = _silu(u)
        for h0 in range(c0, c1, GDN_DK):
            uh = u[:, h0 - c0:h0 - c0 + GDN_DK]
            if h0 < 2 * hw:
                uh = uh * lax.rsqrt(jnp.sum(uh * uh, axis=-1, keepdims=True) + NORM_EPS)
            if h0 < hw:
                q_ref[:, h0:h0 + GDN_DK] = uh * (GDN_DK ** -0.5)
            elif h0 < 2 * hw:
                k_ref[:, h0 - hw:h0 - hw + GDN_DK] = uh
            else:
                v_ref[:, h0 - 2 * hw:h0 - 2 * hw + GDN_DK] = uh
    for c0 in range(0, hw, 512):
        z_ref[:, c0:c0 + 512] = _dot(h_main, w_ref[:, 3 * hw + c0:3 * hw + c0 + 512])

    sm = _dot(h_main, w_ref[:, 4 * hw:4 * hw + LANE])
    g = -jnp.exp(alog_ref[...]) * _softplus(sm + dtb_ref[...])
    beta = jax.nn.sigmoid(sm)
    ri = _iota((rows, rows), 0)
    ci = _iota((rows, rows), 1)
    same = (ri // SCAN_CHUNK) == (ci // SCAN_CHUNK)
    tri_f = jnp.where(jnp.logical_and(same, ci <= ri), 1.0, 0.0).astype(BF16)
    tri_r = jnp.where(jnp.logical_and(same, ci >= ri), 1.0, 0.0).astype(BF16)
    gcs_f = _exact_left(tri_f, g)
    gcs_r = _exact_left(tri_r, g)
    lane = _iota((rows, LANE), 1)
    gcol = jnp.where(lane < GDN_HEADS, gcs_f,
                     jnp.where(lane < 2 * GDN_HEADS, gcs_r, jnp.where(lane < 4 * GDN_HEADS, beta, 0.0)))
    gcol_ref[...] = gcol
    eye = jnp.where(_iota((LANE, LANE), 0) == _iota((LANE, LANE), 1), 1.0, 0.0).astype(BF16)
    grow_ref[...] = _exact_transpose(gcol, eye)[0:4 * GDN_HEADS, :]


def _in_gdn(x, modtab, norm_w, w, conv_w, alog_vec, dtb_vec, n_lat, n_ctx):
    bsz, l, d = x.shape
    hw = GDN_HEADS * GDN_DK
    tok = pl.BlockSpec((None, ROW_TILE, hw), lambda b, i: (b, i, 0))
    return pl.pallas_call(
        functools.partial(_in_gdn_kernel, n_lat=n_lat, n_ctx=n_ctx),
        grid=(bsz, l // ROW_TILE),
        in_specs=_row_tile_specs(d, l, n_lat) + [pl.BlockSpec((d, w.shape[1]), lambda b, i: (0, 0)),
                                                 pl.BlockSpec((CONV_K, 3 * hw), lambda b, i: (0, 0)),
                                                 pl.BlockSpec((1, LANE), lambda b, i: (0, 0)),
                                                 pl.BlockSpec((1, LANE), lambda b, i: (0, 0))],
        out_specs=[tok, tok, tok, tok,
                   pl.BlockSpec((None, ROW_TILE, LANE), lambda b, i: (b, i, 0)),
                   pl.BlockSpec((None, 4 * GDN_HEADS, ROW_TILE), lambda b, i: (b, 0, i))],
        out_shape=[jax.ShapeDtypeStruct((bsz, l, hw), F32)] * 4
        + [jax.ShapeDtypeStruct((bsz, l, LANE), F32), jax.ShapeDtypeStruct((bsz, 4 * GDN_HEADS, l), F32)],
        scratch_shapes=[pltpu.VMEM((ROW_TILE + 2 * SUBLANE, 3 * hw), F32)],
        compiler_params=_cparams("arbitrary", "arbitrary"),
        name="in_gdn",
    )(x, x, x, modtab, norm_w, w, conv_w, alog_vec, dtb_vec)


def _pair_blockdiag(x):
    c = x.shape[0]
    z = jnp.zeros((c, c), x.dtype)
    return jnp.concatenate([jnp.concatenate([x[:, :c], z], axis=1),
                            jnp.concatenate([z, x[:, c:]], axis=1)], axis=0)


def _gdn_scan_kernel(*refs, reverse, final, n_lat, n_ctx):
    if final:
        q_ref, k_ref, v_ref, gcol_ref, grow_ref, of_ref, z_ref, nw_ref, o_ref, s_ref = refs
    else:
        q_ref, k_ref, v_ref, gcol_ref, grow_ref, o_ref, s_ref = refs
    rows = SCAN_CHUNK
    n_pairs = GDN_HEADS // 2
    pw = 2 * GDN_DK
    chains = [(bb, p) for bb in range(q_ref.shape[0]) for p in range(n_pairs)]
    n_chains = len(chains)
    i = pl.program_id(1)

    @pl.when(i == 0)
    def _():
        s_ref[...] = jnp.zeros_like(s_ref)

    ri = _iota((rows, rows), 0)
    ci = _iota((rows, rows), 1)
    incl = (ri <= ci) if reverse else (ri >= ci)
    ri2 = _iota((rows, pw), 0)
    ci2 = jnp.bitwise_and(_iota((rows, pw), 1), rows - 1)
    strict2 = (ri2 < ci2) if reverse else (ri2 > ci2)
    d = 1 if reverse else 0
    edge = 0 if reverse else rows - 1

    def pair_cols(gcol, lane0):
        return jnp.concatenate([jnp.broadcast_to(gcol[:, lane0 + j:lane0 + j + 1], (rows, GDN_DK))
                                for j in range(2)], axis=1)

    qs, ks, kbs, egs, gcs, decs, rhs, sts = [], [], [], [], [], [], [], []
    for c, (bb, p) in enumerate(chains):
        sl = slice(p * pw, (p + 1) * pw)
        li = d * GDN_HEADS + 2 * p
        gcol = gcol_ref[bb]
        grow = grow_ref[bb]
        q2, k2, v2 = q_ref[bb, :, sl], k_ref[bb, :, sl], v_ref[bb, :, sl]
        gc2 = pair_cols(gcol, li)
        beta2 = pair_cols(gcol, 2 * GDN_HEADS + li)
        eg2 = jnp.exp(gc2)
        kb2 = k2 * beta2
        dec2 = jnp.concatenate(
            [jnp.exp(jnp.where(incl, gcol[:, li + j:li + j + 1] - grow[li + j:li + j + 1, :], NEG_BIG))
             for j in range(2)], axis=1)
        qs.append(q2)
        ks.append(k2)
        kbs.append(kb2)
        egs.append(eg2)
        gcs.append(gc2)
        decs.append(dec2)
        rhs.append([jnp.concatenate([v2[:, j * GDN_DK:(j + 1) * GDN_DK] * beta2[:, j * GDN_DK:(j + 1) * GDN_DK],
                                     kb2[:, j * GDN_DK:(j + 1) * GDN_DK] * eg2[:, j * GDN_DK:(j + 1) * GDN_DK]],
                                    axis=1).astype(BF16) for j in range(2)])
        sts.append(s_ref[c])

    a_mats, attns = [], []
    for p in range(n_chains):
        kq = _dot_nt(jnp.concatenate([kbs[p], qs[p]], axis=0).astype(BF16), _pair_blockdiag(ks[p].astype(BF16)))
        a_mats.append(jnp.where(strict2, kq[:rows] * decs[p], 0.0))
        attns.append((kq[rows:] * decs[p]).astype(BF16))

    half = rows // 2
    rq = _iota((half, pw), 0)
    lq = _iota((half, pw), 1)
    cq = jnp.bitwise_and(lq, half - 1)
    low_half = jnp.bitwise_and(lq, rows - 1) < half
    blk_q = lq // half

    def quad_blockdiag(y):
        return jnp.concatenate([jnp.where(blk_q == b, y, jnp.zeros_like(y)) for b in range(pw // half)], axis=0)

    aqs = [jnp.where(low_half, a_mats[p][:half], a_mats[p][half:]) for p in range(n_chains)]
    tqs = [jnp.where(rq == cq, 1.0, 0.0) - jnp.where((rq // 2) == (cq // 2), aqs[p], 0.0) for p in range(n_chains)]
    s = 2
    while s < half:
        off = jnp.logical_and((rq // (2 * s)) == (cq // (2 * s)), (rq // s) != (cq // s))
        xs = [_dot(jnp.where(off, aqs[p], 0.0).astype(BF16), quad_blockdiag(tqs[p].astype(BF16)))
              for p in range(n_chains)]
        tqs = [tqs[p] - _dot(tqs[p].astype(BF16), quad_blockdiag(xs[p].astype(BF16))) for p in range(n_chains)]
        s *= 2
    ts = [jnp.concatenate([jnp.where(low_half, tqs[p], 0.0), jnp.where(low_half, 0.0, tqs[p])], axis=0)
          for p in range(n_chains)]
    off = (ri2 // half) != (ci2 // half)
    xs = [_dot(jnp.where(off, a_mats[p], 0.0).astype(BF16), _pair_blockdiag(ts[p].astype(BF16)))
          for p in range(n_chains)]
    ts = [ts[p] - _dot(ts[p].astype(BF16), _pair_blockdiag(xs[p].astype(BF16))) for p in range(n_chains)]

    us, ws = [], []
    for p in range(n_chains):
        tb = ts[p].astype(BF16)
        sol = [_dot(tb[:, j * GDN_DK:(j + 1) * GDN_DK], rhs[p][j]) for j in range(2)]
        us.append(jnp.concatenate([sol[0][:, :GDN_DK], sol[1][:, :GDN_DK]], axis=1))
        ws.append(jnp.concatenate([sol[0][:, GDN_DK:], sol[1][:, GDN_DK:]], axis=1))

    vns, outs = [], []
    for p in range(n_chains):
        wq = jnp.concatenate([ws[p], qs[p] * egs[p]], axis=0).astype(BF16)
        ws_qs = _dot(wq, _pair_blockdiag(sts[p].astype(BF16)))
        vns.append((us[p] - ws_qs[:rows]).astype(BF16))
        outs.append(ws_qs[rows:])
    for c, (bb, p) in enumerate(chains):
        o2 = outs[c] + _dot(attns[c], _pair_blockdiag(vns[c]))
        g_end = gcs[c][edge:edge + 1, :]
        k_dec = (ks[c] * jnp.exp(g_end - gcs[c])).astype(BF16)
        full = _dot_tn(k_dec, vns[c])
        upd = jnp.concatenate([full[:GDN_DK, :GDN_DK], full[GDN_DK:, GDN_DK:]], axis=1)
        s_ref[c] = sts[c] * jnp.exp(g_end) + upd
        sl = slice(p * pw, (p + 1) * pw)
        if final:
            o2 = o2 + of_ref[bb, :, sl]
            ys = []
            for j in range(2):
                oj = o2[:, j * GDN_DK:(j + 1) * GDN_DK]
                ys.append(oj * lax.rsqrt(jnp.mean(oj * oj, axis=-1, keepdims=True) + NORM_EPS) * nw_ref[...])
            o_ref[bb, :, sl] = (jnp.concatenate(ys, axis=1) * _silu(z_ref[bb, :, sl])).astype(o_ref.dtype)
        else:
            o_ref[bb, :, sl] = o2


def _gdn_scan(q, k, v, gcol, grow, o_fwd, z, norm_w, n_lat, n_ctx, reverse):
    bsz, l, hw = q.shape
    final = o_fwd is not None
    chunk_fn = functools.partial(_chunk_of_step, n_lat=n_lat, n_ctx=n_ctx, reverse=reverse)
    nb = next(n for n in (4, 2, 1) if bsz % n == 0)
    tok = pl.BlockSpec((nb, SCAN_CHUNK, hw), lambda b, i: (b, chunk_fn(i), 0))
    in_specs = [tok, tok, tok,
                pl.BlockSpec((nb, SCAN_CHUNK, LANE), lambda b, i: (b, chunk_fn(i), 0)),
                pl.BlockSpec((nb, 4 * GDN_HEADS, SCAN_CHUNK), lambda b, i: (b, 0, chunk_fn(i)))]
    args = [q, k, v, gcol, grow]
    if final:
        in_specs += [tok, tok, pl.BlockSpec((1, GDN_DK), lambda b, i: (0, 0))]
        args += [o_fwd, z, norm_w]
    return pl.pallas_call(
        functools.partial(_gdn_scan_kernel, reverse=reverse, final=final, n_lat=n_lat, n_ctx=n_ctx),
        grid=(bsz // nb, l // SCAN_CHUNK),
        in_specs=in_specs,
        out_specs=tok,
        out_shape=jax.ShapeDtypeStruct((bsz, l, hw), BF16 if final else F32),
        scratch_shapes=[pltpu.VMEM((nb * GDN_HEADS // 2, GDN_DK, 2 * GDN_DK), F32)],
        compiler_params=_cparams("arbitrary", "arbitrary"),
        name="gdn_rev" if reverse else "gdn_fwd",
    )(*args)


def _in_ssd_kernel(xl_ref, xlp_ref, xln_ref, xc_ref, xcp_ref, xcn_ref, mod_ref, nw_ref, w_ref, cw_ref, cb_ref,
                   z_ref, u_ref, dt_ref, xe_ref, *, n_lat, n_ctx):
    j = pl.program_id(1)
    first, last = _segment_edges(j, n_lat, n_ctx)
    is_ctx = j >= n_lat
    pick = lambda c_ref, l_ref: jnp.where(is_ctx, c_ref[...], l_ref[...])
    h_main, h_all = _normed_rows(pick(xc_ref, xl_ref), pick(xcp_ref, xlp_ref), pick(xcn_ref, xln_ref),
                                 mod_ref, nw_ref)
    inner = z_ref.shape[1]
    xbc_w = u_ref.shape[1]
    for c0 in range(0, inner, 512):
        z_ref[:, c0:c0 + 512] = _dot(h_main, w_ref[:, c0:c0 + 512])
    for c0, c1, u in _proj_conv(xe_ref, h_all, w_ref, cw_ref, inner, xbc_w, first, last, ROW_TILE):
        u_ref[:, c0:c1] = _silu(u + cb_ref[:, c0:c1])
    dt_ref[...] = _dot(h_main, w_ref[:, inner + xbc_w:inner + xbc_w + LANE])


def _in_ssd(x_lat, x, modtab, norm_w, w, conv_w, conv_b, n_lat, n_ctx):
    bsz, l, d = x.shape
    inner = SSD_HEADS * SSD_HEADDIM
    xbc_w = inner + 2 * SSD_GROUPS * SSD_STATE
    tok = lambda n: pl.BlockSpec((None, ROW_TILE, n), lambda b, j: (b, j, 0))
    lat_tile = lambda j: jnp.minimum(j, n_lat - 1)
    ctx_tile = lambda j: jnp.maximum(j, n_lat)
    lat_specs = [pl.BlockSpec((None, ROW_TILE, d), lambda b, j: (b, lat_tile(j), 0)),
                 *_halo_specs(d, ROW_TILE, x_lat.shape[1], lat_tile)]
    ctx_specs = [pl.BlockSpec((None, ROW_TILE, d), lambda b, j: (b, ctx_tile(j), 0)),
                 *_halo_specs(d, ROW_TILE, l, ctx_tile)]
    return pl.pallas_call(
        functools.partial(_in_ssd_kernel, n_lat=n_lat, n_ctx=n_ctx),
        grid=(bsz, l // ROW_TILE),
        in_specs=lat_specs + ctx_specs + _row_tile_specs(d, l, n_lat)[3:] + [
            pl.BlockSpec((d, w.shape[1]), lambda b, j: (0, 0)),
            pl.BlockSpec((CONV_K, xbc_w), lambda b, j: (0, 0)),
            pl.BlockSpec((1, xbc_w), lambda b, j: (0, 0))],
        out_specs=[tok(inner), tok(xbc_w), tok(LANE)],
        out_shape=[jax.ShapeDtypeStruct((bsz, l, inner), F32), jax.ShapeDtypeStruct((bsz, l, xbc_w), F32),
                   jax.ShapeDtypeStruct((bsz, l, LANE), F32)],
        scratch_shapes=[pltpu.VMEM((ROW_TILE + 2 * SUBLANE, xbc_w), F32)],
        compiler_params=_cparams("arbitrary", "arbitrary"),
        name="in_ssd",
    )(x_lat, x_lat, x_lat, x, x, x, modtab, norm_w, w, conv_w, conv_b)


def _ssd_scan_kernel(*refs, reverse, final):
    if final:
        z_ref, u_ref, dt_ref, dtb_ref, a_ref, dsk_ref, yf_ref, nw_ref, o_ref, st_ref = refs
    else:
        u_ref, dt_ref, dtb_ref, a_ref, dsk_ref, o_ref, st_ref = refs
    rows = SCAN_CHUNK
    inner = SSD_HEADS * SSD_HEADDIM
    gw = inner // SSD_GROUPS
    i = pl.program_id(1)

    @pl.when(i == 0)
    def _():
        st_ref[...] = jnp.zeros_like(st_ref)

    ri = _iota((rows, rows), 0)
    ci = _iota((rows, rows), 1)
    incl = (ri <= ci) if reverse else (ri >= ci)
    tri = jnp.where(incl, 1.0, 0.0).astype(BF16)
    eye = jnp.where(ri == ci, 1.0, 0.0).astype(BF16)
    d = 1 if reverse else 0
    edge = 0 if reverse else rows - 1
    er = _iota((LANE, inner), 0)
    ec = _iota((LANE, inner), 1)
    expand = jnp.where(er == d * SSD_HEADS + ec // SSD_HEADDIM, 1.0, 0.0).astype(BF16)
    lane = _iota((rows, LANE), 1)
    heads_per_group = SSD_HEADS // SSD_GROUPS
    samples = range(u_ref.shape[0])
    units = [(bb, g) for bb in samples for g in range(SSD_GROUPS)]

    us, xss, acss, acs_ts, ea_es, xdt_bs, xw_bs = [], [], [], [], [], [], []
    for bb in samples:
        u = u_ref[bb]
        xs = u[:, :inner]
        dt = _softplus(dt_ref[bb] + dtb_ref[...])
        acs = _exact_left(tri, a_ref[...] * dt)
        xdt = xs * _exact_right(dt, expand)
        us.append(u)
        xss.append(xs)
        acss.append(acs)
        acs_ts.append(_exact_transpose(acs, eye))
        ea_es.append(_exact_right(jnp.exp(acs), expand))
        xdt_bs.append(xdt.astype(BF16))
        xw_bs.append((xdt * _exact_right(jnp.exp(acs[edge:edge + 1, :] - acs), expand)).astype(BF16))

    bms, cbs, sts, y_offs = [], [], [], []
    for c, (bb, g) in enumerate(units):
        u = us[bb]
        bm = u[:, inner + g * SSD_STATE:inner + (g + 1) * SSD_STATE].astype(BF16)
        cm = u[:, inner + (SSD_GROUPS + g) * SSD_STATE:inner + (SSD_GROUPS + g + 1) * SSD_STATE].astype(BF16)
        st = st_ref[c]
        bms.append(bm)
        cbs.append(_dot_nt(cm, bm))
        sts.append(st)
        y_offs.append(_dot(cm, st.astype(BF16)) * ea_es[bb][:, g * gw:(g + 1) * gw])

    diags = [[] for _ in units]
    for hp in range(heads_per_group // 2):
        for c, (bb, g) in enumerate(units):
            h0 = g * heads_per_group + 2 * hp
            mats = []
            for hh in (h0, h0 + 1):
                li = d * SSD_HEADS + hh
                lm = jnp.exp(jnp.where(incl, acss[bb][:, li:li + 1] - acs_ts[bb][li:li + 1, :], NEG_BIG))
                mats.append((cbs[c] * lm).astype(BF16))
            xp = xdt_bs[bb][:, h0 * SSD_HEADDIM:(h0 + 2) * SSD_HEADDIM]
            x_lo = jnp.where(lane < SSD_HEADDIM, xp, jnp.zeros_like(xp))
            x_hi = jnp.where(lane >= SSD_HEADDIM, xp, jnp.zeros_like(xp))
            diags[c].append(_dot(jnp.concatenate(mats, axis=1), jnp.concatenate([x_lo, x_hi], axis=0)))

    for c, (bb, g) in enumerate(units):
        st_ref[c] = (sts[c] * ea_es[bb][edge:edge + 1, g * gw:(g + 1) * gw]
                     + _dot_tn(bms[c], xw_bs[bb][:, g * gw:(g + 1) * gw]))
    for bb in samples:
        y_parts = [jnp.concatenate(diags[bb * SSD_GROUPS + g], axis=1) + y_offs[bb * SSD_GROUPS + g]
                   for g in range(SSD_GROUPS)]
        y = jnp.concatenate(y_parts, axis=1) + dsk_ref[...] * xss[bb]
        if final:
            y = y + yf_ref[bb]
            yz = y * _silu(z_ref[bb])
            outs = []
            for g in range(SSD_GROUPS):
                yg = yz[:, g * gw:(g + 1) * gw]
                outs.append(yg * lax.rsqrt(jnp.mean(yg * yg, axis=-1, keepdims=True) + NORM_EPS))
            o_ref[bb] = (jnp.concatenate(outs, axis=1) * nw_ref[...]).astype(o_ref.dtype)
        else:
            o_ref[bb] = y


def _ssd_scan(z, u, dt_raw, dtb_vec, a_vec, dskip, y_fwd, norm_w, n_lat, n_ctx, reverse):
    bsz, l, xbc_w = u.shape
    inner = SSD_HEADS * SSD_HEADDIM
    final = y_fwd is not None
    chunk_fn = functools.partial(_chunk_of_step, n_lat=n_lat, n_ctx=n_ctx, reverse=reverse)
    nb = 2 if bsz % 2 == 0 else 1
    tok = lambda n: pl.BlockSpec((nb, SCAN_CHUNK, n), lambda b, i: (b, chunk_fn(i), 0))
    row = lambda n: pl.BlockSpec((1, n), lambda b, i: (0, 0))
    in_specs = [tok(xbc_w), tok(LANE), row(LANE), row(LANE), row(inner)]
    args = [u, dt_raw, dtb_vec, a_vec, dskip]
    if final:
        in_specs = [tok(inner)] + in_specs + [tok(inner), row(inner)]
        args = [z] + args + [y_fwd, norm_w]
    return pl.pallas_call(
        functools.partial(_ssd_scan_kernel, reverse=reverse, final=final),
        grid=(bsz // nb, l // SCAN_CHUNK),
        in_specs=in_specs,
        out_specs=tok(inner),
        out_shape=jax.ShapeDtypeStruct((bsz, l, inner), BF16 if final else F32),
        scratch_shapes=[pltpu.VMEM((nb * SSD_GROUPS, SSD_STATE, inner // SSD_GROUPS), F32)],
        compiler_params=_cparams("arbitrary", "arbitrary"),
        name="ssd_rev" if reverse else "ssd_fwd",
    )(*args)


def _merge_kernel(x_ref, lru_ref, gdn_ref, ssdl_ref, ssdc_ref, gate_ref, mod_ref, nw_ref, wb_ref, wo_ref, wr_ref,
                  br_ref, xo_ref, h_ref, lg_ref, *, n_lat):
    d = x_ref.shape[1]
    ssd = jnp.where(pl.program_id(1) >= n_lat, ssdc_ref[...], ssdl_ref[...])
    acc = None
    for k, b in enumerate((lru_ref[...], gdn_ref[...], ssd)):
        t = jax.nn.sigmoid(gate_ref[:, k * d:(k + 1) * d].astype(F32)) * _dot(b.astype(BF16), wb_ref[k])
        acc = t if acc is None else acc + t
    mix = _dot(acc.astype(BF16), wo_ref[...])
    x = x_ref[...] + mod_ref[2:3, :] * mix
    xo_ref[...] = x
    h = _mod_norm(x, nw_ref[...], mod_ref[3:4, :], mod_ref[4:5, :])
    hb = h.astype(BF16)
    h_ref[...] = hb
    lg_ref[...] = _dot(hb, wr_ref[...]) + br_ref[...]


def _merge(x, lru, gdn, ssd_lat, ssd_scan, gate, modtab, norm_w, wb, wo, wr, br, n_lat_tiles):
    bsz, l, d = x.shape
    tok = pl.BlockSpec((None, ROW_TILE, d), lambda b, j: (b, j, 0))
    return pl.pallas_call(
        functools.partial(_merge_kernel, n_lat=n_lat_tiles),
        grid=(bsz, l // ROW_TILE),
        in_specs=[tok, tok, tok,
                  pl.BlockSpec((None, ROW_TILE, d), lambda b, j: (b, jnp.minimum(j, n_lat_tiles - 1), 0)),
                  pl.BlockSpec((None, ROW_TILE, d), lambda b, j: (b, jnp.maximum(j, n_lat_tiles), 0)),
                  pl.BlockSpec((None, ROW_TILE, 3 * d), lambda b, j: (b, j, 0)),
                  pl.BlockSpec((None, None, SUBLANE, d), lambda b, j: (b, jnp.where(j >= n_lat_tiles, 1, 0), 0, 0)),
                  pl.BlockSpec((1, d), lambda b, j: (0, 0)),
                  pl.BlockSpec((3, d, d), lambda b, j: (0, 0, 0)),
                  pl.BlockSpec((d, d), lambda b, j: (0, 0)),
                  pl.BlockSpec((d, LANE), lambda b, j: (0, 0)),
                  pl.BlockSpec((1, LANE), lambda b, j: (0, 0))],
        out_specs=[tok, tok, pl.BlockSpec((None, ROW_TILE, LANE), lambda b, j: (b, j, 0))],
        out_shape=[jax.ShapeDtypeStruct((bsz, l, d), F32), jax.ShapeDtypeStruct((bsz, l, d), BF16),
                   jax.ShapeDtypeStruct((bsz, l, LANE), F32)],
        compiler_params=_cparams("arbitrary", "arbitrary"),
        name="merge",
    )(x, lru, gdn, ssd_lat, ssd_scan, gate, modtab, norm_w, wb, wo, wr, br)


def _moe_kernel(blk_ref, exp_ref, lo_ref, hi_ref, x_ref, w1_ref, w3_ref, w2_ref, o_ref,
                acc_ref, w1b_ref, w3b_ref, w2b_ref):
    i = pl.program_id(0)
    lo = lo_ref[i]
    hi = hi_ref[i]
    base = blk_ref[i] * MOE_ROWS

    @pl.when(jnp.logical_or(i == 0, exp_ref[i] != exp_ref[jnp.maximum(i - 1, 0)]))
    def _():
        w1b_ref[...] = w1_ref[...].astype(BF16)
        w3b_ref[...] = w3_ref[...].astype(BF16)
        w2b_ref[...] = w2_ref[...].astype(BF16)

    @pl.when(lo == base)
    def _():
        acc_ref[...] = jnp.zeros_like(acc_ref)

    @pl.when(hi > lo)
    def _():
        row = base + _iota((MOE_ROWS, 1), 0)
        x = jnp.where(jnp.logical_and(row >= lo, row < hi), x_ref[...], jnp.zeros_like(x_ref))
        mid = _silu(_dot(x, w1b_ref[...])) * _dot(x, w3b_ref[...])
        acc_ref[...] += _dot(mid.astype(BF16), w2b_ref[...])

    o_ref[...] = acc_ref[...].astype(o_ref.dtype)


def _moe_experts(xb, item_block, item_expert, item_lo, item_hi, w1, w3, w2, layer):
    n_rows, d = xb.shape
    ff = w1.shape[-1]
    rows_spec = pl.BlockSpec((MOE_ROWS, d), lambda i, blk, ex, lo, hi: (blk[i], 0))
    grid_spec = pltpu.PrefetchScalarGridSpec(
        num_scalar_prefetch=4,
        grid=(item_block.shape[0],),
        in_specs=[rows_spec,
                  pl.BlockSpec((None, None, d, ff), lambda i, blk, ex, lo, hi: (layer, ex[i], 0, 0)),
                  pl.BlockSpec((None, None, d, ff), lambda i, blk, ex, lo, hi: (layer, ex[i], 0, 0)),
                  pl.BlockSpec((None, None, ff, d), lambda i, blk, ex, lo, hi: (layer, ex[i], 0, 0))],
        out_specs=rows_spec,
        scratch_shapes=[pltpu.VMEM((MOE_ROWS, d), F32), pltpu.VMEM((d, ff), BF16), pltpu.VMEM((d, ff), BF16),
                        pltpu.VMEM((ff, d), BF16)],
    )
    return pl.pallas_call(
        _moe_kernel,
        grid_spec=grid_spec,
        out_shape=jax.ShapeDtypeStruct((n_rows, d), BF16),
        compiler_params=_cparams("arbitrary"),
        name="moe_experts",
    )(item_block, item_expert, item_lo, item_hi, xb, w1, w3, w2)


def _route(logits):
    t = logits.shape[0]
    g_logits = logits[:, :N_GROUPS]
    e_logits = logits[:, N_GROUPS:N_GROUPS + N_EXPERTS].reshape(t, N_GROUPS, EXPERTS_PER_GROUP)
    g_prob = jax.nn.softmax(g_logits, axis=-1)
    g_idx = jnp.argmax(g_logits, axis=-1)
    p_group = jnp.take_along_axis(g_prob, g_idx[:, None], axis=1)[:, 0]
    e_in_group = jnp.take_along_axis(e_logits, g_idx[:, None, None], axis=1)[:, 0]
    top_v, top_i = lax.top_k(e_in_group, TOP_K)
    weights = jax.nn.softmax(top_v, axis=-1) * p_group[:, None]
    expert_id = (g_idx[:, None] * EXPERTS_PER_GROUP + top_i).reshape(-1).astype(jnp.int32)
    n_assign = t * TOP_K
    assert n_assign % MOE_ROWS == 0
    n_blocks = n_assign // MOE_ROWS
    ar = jnp.arange(n_assign, dtype=jnp.int32)
    e_sorted, order = lax.sort((expert_id, ar), num_keys=1)
    _, slot = lax.sort((order, ar), num_keys=1)
    ends = jnp.sum(e_sorted[None, :] <= jnp.arange(N_EXPERTS, dtype=jnp.int32)[:, None], axis=1).astype(jnp.int32)
    starts = jnp.concatenate([jnp.zeros((1,), jnp.int32), ends[:-1]])
    cuts = jnp.sort(jnp.concatenate([jnp.arange(n_blocks, dtype=jnp.int32) * MOE_ROWS, starts]))
    item_lo = cuts
    item_hi = jnp.concatenate([cuts[1:], jnp.full((1,), n_assign, jnp.int32)])
    item_block = jnp.minimum(item_lo // MOE_ROWS, n_blocks - 1).astype(jnp.int32)
    item_expert = jnp.minimum(jnp.sum(item_lo[:, None] >= ends[None, :], axis=1), N_EXPERTS - 1).astype(jnp.int32)
    return order // TOP_K, (item_block, item_expert, item_lo, item_hi), slot.reshape(t, TOP_K), weights


def _combine_kernel(x_ref, ya_ref, yb_ref, wt_ref, mod_ref, nw_ref, o_ref, *, final):
    wt = wt_ref[...]
    y = wt[:, 0:1] * ya_ref[...].astype(F32) + wt[:, 1:2] * yb_ref[...].astype(F32)
    x = x_ref[...] + mod_ref[5:6, :] * y
    if final:
        x = x * lax.rsqrt(jnp.mean(x * x, axis=-1, keepdims=True) + NORM_EPS) * nw_ref[...]
    o_ref[...] = x


def _combine(x, ya, yb, wts, modtab, norm_w, n_lat_tiles, final, out_len):
    bsz, l, d = x.shape
    tok = pl.BlockSpec((None, ROW_TILE, d), lambda b, j: (b, j, 0))
    return pl.pallas_call(
        functools.partial(_combine_kernel, final=final),
        grid=(bsz, out_len // ROW_TILE),
        in_specs=[tok, tok, tok,
                  pl.BlockSpec((None, ROW_TILE, LANE), lambda b, j: (b, j, 0)),
                  pl.BlockSpec((None, None, SUBLANE, d), lambda b, j: (b, jnp.where(j >= n_lat_tiles, 1, 0), 0, 0)),
                  pl.BlockSpec((1, d), lambda b, j: (0, 0))],
        out_specs=tok,
        out_shape=jax.ShapeDtypeStruct((bsz, out_len, d), F32),
        compiler_params=_cparams("arbitrary", "arbitrary"),
        name="combine",
    )(x, ya, yb, wts, modtab, norm_w)


def _pad_lanes(v, n=LANE):
    v = v.reshape(1, -1).astype(F32)
    return jnp.pad(v, ((0, 0), (0, n - v.shape[1])))


def kernel(x, c, ctx, c_ctx, w_mod, b_mod, norm1_w, norm2_w, w_in, lru_conv_w, lru_conv_b, lru_wa, lru_ba,
           lru_wi, lru_bi, lru_lambda, gdn_conv_w, gdn_a_log, gdn_dt_bias, gdn_norm_w, ssd_conv_w, ssd_conv_b,
           ssd_a_log, ssd_dt_bias, ssd_d, ssd_norm_w, w_branch, w_out, router_group_w, router_group_b,
           router_expert_w, router_expert_b, expert_w1, expert_w3, expert_w2, final_norm_w):
    bsz, seq, d = x.shape
    n_ctx_tok = ctx.shape[1]
    depth = w_mod.shape[0]
    l = seq + n_ctx_tok
    rows = seq // GRID_W
    assert seq % LRU_CHUNK == 0 and n_ctx_tok % LRU_CHUNK == 0 and ROW_TILE == LRU_CHUNK
    assert bsz + 1 <= SUBLANE
    assert l % GRID_W == 0 and l % rows == 0
    hw = GDN_HEADS * GDN_DK
    inner = SSD_HEADS * SSD_HEADDIM
    xbc_w = inner + 2 * SSD_GROUPS * SSD_STATE

    act = jnp.concatenate([_silu(c), _silu(c_ctx)[None, :],
                           jnp.zeros((SUBLANE - bsz - 1, d), F32)], axis=0)
    mod_all = _modulation(act, w_mod, b_mod).reshape(depth, SUBLANE, 6, d)
    pad2 = jnp.zeros((bsz, 2, d), F32)

    xs = jnp.concatenate([x, ctx], axis=1)
    nl_t, nc_t = seq // ROW_TILE, n_ctx_tok // ROW_TILE
    nl_s, nc_s = seq // SCAN_CHUNK, n_ctx_tok // SCAN_CHUNK

    o_lx, o_qkv, o_gz, o_gb, o_ga = 0, 2 * d, 2 * d + 3 * hw, 2 * d + 4 * hw, 2 * d + 4 * hw + 2 * GDN_HEADS
    o_sz = o_ga + 2 * GDN_HEADS
    o_xbc = o_sz + inner
    o_sdt = o_xbc + xbc_w
    o_gate = o_sdt + 2 * SSD_HEADS

    for i in range(depth):
        lat = mod_all[i, :bsz]
        cx = jnp.broadcast_to(mod_all[i, bsz][None], (bsz, 6, d))
        modtab = jnp.stack([jnp.concatenate([lat, pad2], axis=1), jnp.concatenate([cx, pad2], axis=1)], axis=1)

        wi_ = w_in[i]
        w_lru = wi_[:, o_lx:o_qkv].astype(BF16)
        zpad = jnp.zeros((d, LANE - 4 * GDN_HEADS), F32)
        w_gdn = jnp.concatenate([wi_[:, o_qkv:o_gb], wi_[:, o_ga:o_sz], wi_[:, o_gb:o_ga], zpad], axis=1).astype(BF16)
        zpad2 = jnp.zeros((d, LANE - 2 * SSD_HEADS), F32)
        w_ssd = jnp.concatenate([wi_[:, o_sz:o_gate], zpad2], axis=1).astype(BF16)
        w_gate = wi_[:, o_gate:].astype(BF16)
        n1 = norm1_w[i].reshape(1, d)

        gate = _norm_matmul(xs, modtab, n1, w_gate, 0, 1, nl_t, "in_gate")

        lru_u, lru_y = _in_lru(xs, modtab, n1, w_lru, lru_conv_w[i], lru_conv_b[i].reshape(1, d), nl_t, nc_t)
        h_f = _lru_scan(lru_u, lru_y, lru_wa[i, 0].astype(BF16), lru_ba[i, 0].reshape(1, d),
                        lru_wi[i, 0].astype(BF16), lru_bi[i, 0].reshape(1, d), lru_lambda[i, 0].reshape(1, d),
                        None, nl_t, nc_t, False)
        lru_out = _lru_scan(lru_u, lru_y, lru_wa[i, 1].astype(BF16), lru_ba[i, 1].reshape(1, d),
                            lru_wi[i, 1].astype(BF16), lru_bi[i, 1].reshape(1, d), lru_lambda[i, 1].reshape(1, d),
                            h_f, nl_t, nc_t, True)

        q, k, v, gz, gcol, grow = _in_gdn(xs, modtab, n1, w_gdn, gdn_conv_w[i], _pad_lanes(gdn_a_log[i]),
                                          _pad_lanes(gdn_dt_bias[i]), nl_t, nc_t)
        o_f = _gdn_scan(q, k, v, gcol, grow, None, None, None, nl_s, nc_s, False)
        gdn_out = _gdn_scan(q, k, v, gcol, grow, o_f, gz, gdn_norm_w[i].reshape(1, GDN_DK), nl_s, nc_s, True)

        x_cm = xs.reshape(bsz, l // GRID_W, GRID_W, d)[:, :rows].swapaxes(1, 2).reshape(bsz, seq, d)
        sz, su, sdt = _in_ssd(x_cm, xs, modtab, n1, w_ssd, ssd_conv_w[i], ssd_conv_b[i].reshape(1, xbc_w),
                              nl_t, nc_t)
        dtb_vec = _pad_lanes(ssd_dt_bias[i])
        a_vec = _pad_lanes(-jnp.exp(ssd_a_log[i].astype(F32)))
        dsk = [jnp.repeat(ssd_d[i, dd], SSD_HEADDIM).reshape(1, inner) for dd in range(2)]
        y_f = _ssd_scan(None, su, sdt, dtb_vec, a_vec, dsk[0], None, None, nl_s, nc_s, False)
        ssd_scan_out = _ssd_scan(sz, su, sdt, dtb_vec, a_vec, dsk[1], y_f,
                                 ssd_norm_w[i].reshape(1, inner), nl_s, nc_s, True)
        ssd_lat = (ssd_scan_out.reshape(bsz, l // rows, rows, inner)[:, :GRID_W].swapaxes(1, 2)
                   .reshape(bsz, seq, inner))

        w_r = jnp.concatenate([router_group_w[i], router_expert_w[i],
                               jnp.zeros((d, LANE - N_GROUPS - N_EXPERTS), F32)], axis=1).astype(BF16)
        b_r = _pad_lanes(jnp.concatenate([router_group_b[i], router_expert_b[i]]))
        x_mid, h2, logits = _merge(xs, lru_out, gdn_out, ssd_lat, ssd_scan_out, gate, modtab, norm2_w[i].reshape(1, d),
                                   w_branch[i].astype(BF16), w_out[i].astype(BF16), w_r, b_r, nl_t)

        t = bsz * l
        tok_sorted, items, slot, weights = _route(logits.reshape(t, LANE))
        xb = h2.reshape(t, d)[tok_sorted]
        yb = _moe_experts(xb, *items, expert_w1, expert_w3, expert_w2, i)
        y0 = yb[slot[:, 0]].reshape(bsz, l, d)
        y1 = yb[slot[:, 1]].reshape(bsz, l, d)
        wts = jnp.pad(weights, ((0, 0), (0, LANE - TOP_K))).reshape(bsz, l, LANE)
        last = i == depth - 1
        xs = _combine(x_mid, y0, y1, wts, modtab, final_norm_w.reshape(1, d), nl_t, last, seq if last else l)

    return xs
```

```python
import functools

import jax
import jax.numpy as jnp
from jax import lax
from jax.experimental import pallas as pl
from jax.experimental.pallas import tpu as pltpu

GRID_W = 64
CONV_K = 4
NORM_EPS = 1e-6
LRU_BLOCKS = 8
LRU_C = 8.0
GDN_HEADS = 8
GDN_DK = 128
SSD_HEADS = 16
SSD_HEADDIM = 64
SSD_GROUPS = 2
SSD_STATE = 128
N_GROUPS = 4
EXPERTS_PER_GROUP = 8
N_EXPERTS = N_GROUPS * EXPERTS_PER_GROUP
TOP_K = 2

LANE = 128
SUBLANE = 8
ROW_TILE = 256
LRU_CHUNK = 256
SCAN_CHUNK = 128
MOE_ROWS = 512
VMEM_LIMIT = 56 * 1024 * 1024

F32 = jnp.float32
BF16 = jnp.bfloat16
NEG_BIG = -1e30


def _cparams(*sem):
    return pltpu.CompilerParams(dimension_semantics=sem, vmem_limit_bytes=VMEM_LIMIT)


def _dot(a, b):
    return jnp.dot(a, b, preferred_element_type=F32)


def _dot_nt(a, b):
    return lax.dot_general(a, b, (((1,), (1,)), ((), ())), preferred_element_type=F32)


def _dot_tn(a, b):
    return lax.dot_general(a, b, (((0,), (0,)), ((), ())), preferred_element_type=F32)


def _split3(x):
    hi = x.astype(BF16)
    r1 = x - hi.astype(F32)
    mid = r1.astype(BF16)
    lo = (r1 - mid.astype(F32)).astype(BF16)
    return hi, mid, lo


def _exact_left(m_bf, x):
    hi, mid, lo = _split3(x)
    return _dot(jnp.concatenate([m_bf, m_bf, m_bf], axis=1), jnp.concatenate([hi, mid, lo], axis=0))


def _exact_right(x, m_bf):
    hi, mid, lo = _split3(x)
    return _dot(jnp.concatenate([hi, mid, lo], axis=1), jnp.concatenate([m_bf, m_bf, m_bf], axis=0))


def _exact_transpose(x, eye_bf):
    hi, mid, lo = _split3(x)
    return _dot_nt(jnp.concatenate([eye_bf, eye_bf, eye_bf], axis=1), jnp.concatenate([hi, mid, lo], axis=1))


def _softplus(x):
    return jnp.maximum(x, 0.0) + jnp.log(1.0 + jnp.exp(-jnp.abs(x)))


def _silu(x):
    return x * jax.nn.sigmoid(x)


def _iota(shape, dim):
    return lax.broadcasted_iota(jnp.int32, shape, dim)


def _fwd_chunk(i, n_lat, n_ctx):
    return jnp.where(i < n_ctx, n_lat + i, i - n_ctx)


def _chunk_of_step(i, n_lat, n_ctx, reverse):
    return (n_lat + n_ctx - 1 - i) if reverse else _fwd_chunk(i, n_lat, n_ctx)


def _segment_edges(c, n_lat, n_ctx):
    first = jnp.logical_or(c == 0, c == n_lat)
    last = jnp.logical_or(c == n_lat - 1, c == n_lat + n_ctx - 1)
    return first, last


def _normed_rows(x, x_prev, x_next, mod_ref, nw_ref):
    shift, scale = mod_ref[0:1, :], mod_ref[1:2, :]
    h_main = _mod_norm(x, nw_ref[...], shift, scale)
    h_prev = _mod_norm(x_prev, nw_ref[...], shift, scale)
    h_next = _mod_norm(x_next, nw_ref[...], shift, scale)
    return h_main.astype(BF16), jnp.concatenate([h_prev, h_main, h_next], axis=0).astype(BF16)


def _proj_conv(xe_ref, h_all, w_ref, cw_ref, col0, width, first, last, rows, col_step=512):
    base = SUBLANE - CONV_K // 2
    for c0 in range(0, width, col_step):
        c1 = min(c0 + col_step, width)
        y = _dot(h_all, w_ref[:, col0 + c0:col0 + c1])
        xe_ref[0:SUBLANE, c0:c1] = jnp.where(first, 0.0, y[0:SUBLANE])
        xe_ref[SUBLANE:SUBLANE + rows, c0:c1] = y[SUBLANE:SUBLANE + rows]
        xe_ref[SUBLANE + rows:2 * SUBLANE + rows, c0:c1] = jnp.where(last, 0.0, y[SUBLANE + rows:])
        u = cw_ref[0:1, c0:c1] * xe_ref[base:base + rows, c0:c1]
        for j in range(1, CONV_K):
            u = u + cw_ref[j:j + 1, c0:c1] * xe_ref[base + j:base + j + rows, c0:c1]
        yield c0, c1, u


def _halo_specs(width, rows, n_rows_total, chunk_fn, col_block=0):
    per = rows // SUBLANE
    last_tile = n_rows_total // SUBLANE - 1
    prev = pl.BlockSpec((None, SUBLANE, width),
                        lambda b, i: (b, jnp.maximum(chunk_fn(i) * per - 1, 0), col_block))
    nxt = pl.BlockSpec((None, SUBLANE, width),
                       lambda b, i: (b, jnp.minimum((chunk_fn(i) + 1) * per, last_tile), col_block))
    return prev, nxt


def _mod_kernel(a_ref, w_ref, b_ref, o_ref):
    o_ref[...] = _dot(a_ref[...].astype(BF16), w_ref[...].astype(BF16)) + b_ref[...]


def _modulation(act, w_mod, b_mod):
    depth, d, n = w_mod.shape
    tn = 1536
    return pl.pallas_call(
        _mod_kernel,
        grid=(depth, n // tn),
        in_specs=[pl.BlockSpec((SUBLANE, d), lambda l, j: (0, 0)),
                  pl.BlockSpec((None, d, tn), lambda l, j: (l, 0, j)),
                  pl.BlockSpec((None, 1, tn), lambda l, j: (l, 0, j))],
        out_specs=pl.BlockSpec((None, SUBLANE, tn), lambda l, j: (l, 0, j)),
        out_shape=jax.ShapeDtypeStruct((depth, SUBLANE, n), F32),
        compiler_params=_cparams("arbitrary", "arbitrary"),
        name="modulation",
    )(act, w_mod, b_mod.reshape(depth, 1, n))


def _mod_norm(x, nw, shift, scale):
    y = x * lax.rsqrt(jnp.mean(x * x, axis=-1, keepdims=True) + NORM_EPS)
    return (y * nw) * (1.0 + scale) + shift


def _row_tile_specs(d, l, n_lat_tiles):
    ident = lambda j: j
    main = pl.BlockSpec((None, ROW_TILE, d), lambda b, j: (b, j, 0))
    prev, nxt = _halo_specs(d, ROW_TILE, l, ident)
    mod = pl.BlockSpec((None, None, SUBLANE, d), lambda b, j: (b, jnp.where(j >= n_lat_tiles, 1, 0), 0, 0))
    return [main, prev, nxt, mod, pl.BlockSpec((1, d), lambda b, j: (0, 0))]


def _in_lru_kernel(x_ref, xp_ref, xn_ref, mod_ref, nw_ref, w_ref, wg_ref, cw_ref, cb_ref, u_ref, y_ref, g_ref,
                   xe_ref, *, n_lat, n_ctx):
    first, last = _segment_edges(pl.program_id(1), n_lat, n_ctx)
    h_main, h_all = _normed_rows(x_ref[...], xp_ref[...], xn_ref[...], mod_ref, nw_ref)
    w = u_ref.shape[1]
    for c0, c1, u in _proj_conv(xe_ref, h_all, w_ref, cw_ref, 0, w, first, last, ROW_TILE):
        u_ref[:, c0:c1] = u + cb_ref[:, c0:c1]
    for c0 in range(0, w, 512):
        y_ref[:, c0:c0 + 512] = _dot(h_main, w_ref[:, w + c0:w + c0 + 512])
    for c0 in range(0, g_ref.shape[1], 512):
        g_ref[:, c0:c0 + 512] = _dot(h_main, wg_ref[:, c0:c0 + 512]).astype(g_ref.dtype)


def _in_lru(x, modtab, norm_w, w, w_gate, conv_w, conv_b, n_lat, n_ctx):
    bsz, l, d = x.shape
    wd = w.shape[1] // 2
    ng = w_gate.shape[1]
    tok = pl.BlockSpec((None, ROW_TILE, wd), lambda b, j: (b, j, 0))
    return pl.pallas_call(
        functools.partial(_in_lru_kernel, n_lat=n_lat, n_ctx=n_ctx),
        grid=(bsz, l // ROW_TILE),
        in_specs=_row_tile_specs(d, l, n_lat) + [pl.BlockSpec((d, 2 * wd), lambda b, j: (0, 0)),
                                                 pl.BlockSpec((d, ng), lambda b, j: (0, 0)),
                                                 pl.BlockSpec((CONV_K, wd), lambda b, j: (0, 0)),
                                                 pl.BlockSpec((1, wd), lambda b, j: (0, 0))],
        out_specs=[tok, tok, pl.BlockSpec((None, ROW_TILE, ng), lambda b, j: (b, j, 0))],
        out_shape=[jax.ShapeDtypeStruct((bsz, l, wd), F32)] * 2 + [jax.ShapeDtypeStruct((bsz, l, ng), BF16)],
        scratch_shapes=[pltpu.VMEM((ROW_TILE + 2 * SUBLANE, wd), F32)],
        compiler_params=_cparams("arbitrary", "arbitrary"),
        name="in_lru",
    )(x, x, x, modtab, norm_w, w, w_gate, conv_w, conv_b)


def _lru_kernel(*refs, reverse, final):
    if final:
        u_ref, wa_ref, ba_ref, wi_ref, bi_ref, lam_ref, hf_ref, y_ref, o_ref, carry_ref = refs
    else:
        u_ref, wa_ref, ba_ref, wi_ref, bi_ref, lam_ref, o_ref, carry_ref = refs
    rows = LRU_CHUNK
    i = pl.program_id(1)

    @pl.when(i == 0)
    def _():
        carry_ref[...] = jnp.zeros_like(carry_ref)

    u = u_ref[...]
    ub = u.astype(BF16)
    bw = u.shape[1] // LRU_BLOCKS
    pre_r = jnp.concatenate([_dot(ub[:, n * bw:(n + 1) * bw], wa_ref[n]) for n in range(LRU_BLOCKS)], axis=1)
    pre_i = jnp.concatenate([_dot(ub[:, n * bw:(n + 1) * bw], wi_ref[n]) for n in range(LRU_BLOCKS)], axis=1)
    r = jax.nn.sigmoid(pre_r + ba_ref[...])
    gi = jax.nn.sigmoid(pre_i + bi_ref[...])
    a = jnp.exp(-LRU_C * r * _softplus(-lam_ref[...]))
    h = jnp.sqrt(1.0 - a * a) * gi * u

    srow = _iota((SUBLANE, a.shape[1]), 0)
    n_tiles = rows // SUBLANE
    carry = carry_ref[...]
    done = {}
    for k in (range(n_tiles - 1, -1, -1) if reverse else range(n_tiles)):
        sl = slice(k * SUBLANE, (k + 1) * SUBLANE)
        at, ht = a[sl], h[sl]
        for s in (1, 2, 4):
            if reverse:
                keep = srow < SUBLANE - s
                sh = SUBLANE - s
            else:
                keep = srow >= s
                sh = s
            h_s = jnp.where(keep, pltpu.roll(ht, sh, axis=0), 0.0)
            a_s = jnp.where(keep, pltpu.roll(at, sh, axis=0), 1.0)
            ht = ht + at * h_s
            at = at * a_s
        hk = ht + at * carry
        carry = hk[0:1, :] if reverse else hk[SUBLANE - 1:SUBLANE, :]
        if final:
            done[k] = (hk + hf_ref[sl, :]) * jax.nn.gelu(y_ref[sl, :], approximate=True)
            if (k ^ 1) in done:
                lo = k & ~1
                pair = jnp.concatenate([done.pop(lo), done.pop(lo + 1)], axis=0)
                o_ref[lo * SUBLANE:(lo + 2) * SUBLANE, :] = pair.astype(o_ref.dtype)
        else:
            o_ref[sl, :] = hk
    carry_ref[...] = carry


def _lru_scan(u, y, wa, ba, wi, bi, lam, hf, n_lat, n_ctx, reverse):
    bsz, l, w = u.shape
    final = hf is not None
    chunk_fn = functools.partial(_chunk_of_step, n_lat=n_lat, n_ctx=n_ctx, reverse=reverse)
    main = pl.BlockSpec((None, LRU_CHUNK, w), lambda b, i: (b, chunk_fn(i), 0))
    vec = pl.BlockSpec((1, w), lambda b, i: (0, 0))
    blk = pl.BlockSpec((LRU_BLOCKS, w // LRU_BLOCKS, w // LRU_BLOCKS), lambda b, i: (0, 0, 0))
    in_specs = [main, blk, vec, blk, vec, vec]
    args = [u, wa, ba, wi, bi, lam]
    if final:
        in_specs += [main, main]
        args += [hf, y]
    return pl.pallas_call(
        functools.partial(_lru_kernel, reverse=reverse, final=final),
        grid=(bsz, l // LRU_CHUNK),
        in_specs=in_specs,
        out_specs=main,
        out_shape=jax.ShapeDtypeStruct((bsz, l, w), BF16 if final else F32),
        scratch_shapes=[pltpu.VMEM((1, w), F32)],
        compiler_params=_cparams("arbitrary", "arbitrary"),
        name="lru_rev" if reverse else "lru_fwd",
    )(*args)


def _in_gdn_kernel(x_ref, xp_ref, xn_ref, mod_ref, nw_ref, w_ref, cw_ref, alog_ref, dtb_ref,
                   q_ref, k_ref, v_ref, z_ref, gcol_ref, grow_ref, xe_ref, *, n_lat, n_ctx):
    rows = ROW_TILE
    first, last = _segment_edges(pl.program_id(1), n_lat, n_ctx)
    h_main, h_all = _normed_rows(x_ref[...], xp_ref[...], xn_ref[...], mod_ref, nw_ref)
    hw = GDN_HEADS * GDN_DK
    for c0, c1, u in _proj_conv(xe_ref, h_all, w_ref, cw_ref, 0, 3 * hw, first, last, rows):
        u = _silu(u)
        for h0 in range(c0, c1, GDN_DK):
            uh = u[:, h0 - c0:h0 - c0 + GDN_DK]
            if h0 < 2 * hw:
                uh = uh * lax.rsqrt(jnp.sum(uh * uh, axis=-1, keepdims=True) + NORM_EPS)
            if h0 < hw:
                q_ref[:, h0:h0 + GDN_DK] = uh * (GDN_DK ** -0.5)
            elif h0 < 2 * hw:
                k_ref[:, h0 - hw:h0 - hw + GDN_DK] = uh
            else:
                v_ref[:, h0 - 2 * hw:h0 - 2 * hw + GDN_DK] = uh
    for c0 in range(0, hw, 512):
        z_ref[:, c0:c0 + 512] = _dot(h_main, w_ref[:, 3 * hw + c0:3 * hw + c0 + 512])

    sm = _dot(h_main, w_ref[:, 4 * hw:4 * hw + LANE])
    g = -jnp.exp(alog_ref[...]) * _softplus(sm + dtb_ref[...])
    beta = jax.nn.sigmoid(sm)
    ri = _iota((rows, rows), 0)
    ci = _iota((rows, rows), 1)
    same = (ri // SCAN_CHUNK) == (ci // SCAN_CHUNK)
    tri_f = jnp.where(jnp.logical_and(same, ci <= ri), 1.0, 0.0).astype(BF16)
    tri_r = jnp.where(jnp.logical_and(same, ci >= ri), 1.0, 0.0).astype(BF16)
    gcs_f = _exact_left(tri_f, g)
    gcs_r = _exact_left(tri_r, g)
    lane = _iota((rows, LANE), 1)
    gcol = jnp.where(lane < GDN_HEADS, gcs_f,
                     jnp.where(lane < 2 * GDN_HEADS, gcs_r, jnp.where(lane < 4 * GDN_HEADS, beta, 0.0)))
    gcol_ref[...] = gcol
    eye = jnp.where(_iota((LANE, LANE), 0) == _iota((LANE, LANE), 1), 1.0, 0.0).astype(BF16)
    grow_ref[...] = _exact_transpose(gcol, eye)[0:4 * GDN_HEADS, :]


def _in_gdn(x, modtab, norm_w, w, conv_w, alog_vec, dtb_vec, n_lat, n_ctx):
    bsz, l, d = x.shape
    hw = GDN_HEADS * GDN_DK
    tok = pl.BlockSpec((None, ROW_TILE, hw), lambda b, i: (b, i, 0))
    return pl.pallas_call(
        functools.partial(_in_gdn_kernel, n_lat=n_lat, n_ctx=n_ctx),
        grid=(bsz, l // ROW_TILE),
        in_specs=_row_tile_specs(d, l, n_lat) + [pl.BlockSpec((d, w.shape[1]), lambda b, i: (0, 0)),
                                                 pl.BlockSpec((CONV_K, 3 * hw), lambda b, i: (0, 0)),
                                                 pl.BlockSpec((1, LANE), lambda b, i: (0, 0)),
                                                 pl.BlockSpec((1, LANE), lambda b, i: (0, 0))],
        out_specs=[tok, tok, tok, tok,
                   pl.BlockSpec((None, ROW_TILE, LANE), lambda b, i: (b, i, 0)),
                   pl.BlockSpec((None, 4 * GDN_HEADS, ROW_TILE), lambda b, i: (b, 0, i))],
        out_shape=[jax.ShapeDtypeStruct((bsz, l, hw), F32)] * 4
        + [jax.ShapeDtypeStruct((bsz, l, LANE), F32), jax.ShapeDtypeStruct((bsz, 4 * GDN_HEADS, l), F32)],
        scratch_shapes=[pltpu.VMEM((ROW_TILE + 2 * SUBLANE, 3 * hw), F32)],
        compiler_params=_cparams("arbitrary", "arbitrary"),
        name="in_gdn",
    )(x, x, x, modtab, norm_w, w, conv_w, alog_vec, dtb_vec)


def _pair_blockdiag(x):
    c = x.shape[0]
    z = jnp.zeros((c, c), x.dtype)
    return jnp.concatenate([jnp.concatenate([x[:, :c], z], axis=1),
                            jnp.concatenate([z, x[:, c:]], axis=1)], axis=0)


def _gdn_scan_kernel(*refs, reverse, final, n_lat, n_ctx):
    if final:
        q_ref, k_ref, v_ref, gcol_ref, grow_ref, of_ref, z_ref, nw_ref, o_ref, s_ref = refs
    else:
        q_ref, k_ref, v_ref, gcol_ref, grow_ref, o_ref, s_ref = refs
    rows = SCAN_CHUNK
    n_pairs = GDN_HEADS // 2
    pw = 2 * GDN_DK
    chains = [(bb, p) for bb in range(q_ref.shape[0]) for p in range(n_pairs)]
    n_chains = len(chains)
    i = pl.program_id(1)

    @pl.when(i == 0)
    def _():
        s_ref[...] = jnp.zeros_like(s_ref)

    ri = _iota((rows, rows), 0)
    ci = _iota((rows, rows), 1)
    incl = (ri <= ci) if reverse else (ri >= ci)
    ri2 = _iota((rows, pw), 0)
    ci2 = jnp.bitwise_and(_iota((rows, pw), 1), rows - 1)
    strict2 = (ri2 < ci2) if reverse else (ri2 > ci2)
    d = 1 if reverse else 0
    edge = 0 if reverse else rows - 1

    def pair_cols(gcol, lane0):
        return jnp.concatenate([jnp.broadcast_to(gcol[:, lane0 + j:lane0 + j + 1], (rows, GDN_DK))
                                for j in range(2)], axis=1)

    qs, ks, kbs, egs, gcs, decs, rhs, sts = [], [], [], [], [], [], [], []
    for c, (bb, p) in enumerate(chains):
        sl = slice(p * pw, (p + 1) * pw)
        li = d * GDN_HEADS + 2 * p
        gcol = gcol_ref[bb]
        grow = grow_ref[bb]
        q2, k2, v2 = q_ref[bb, :, sl], k_ref[bb, :, sl], v_ref[bb, :, sl]
        gc2 = pair_cols(gcol, li)
        beta2 = pair_cols(gcol, 2 * GDN_HEADS + li)
        eg2 = jnp.exp(gc2)
        kb2 = k2 * beta2
        dec2 = jnp.concatenate(
            [jnp.exp(jnp.where(incl, gcol[:, li + j:li + j + 1] - grow[li + j:li + j + 1, :], NEG_BIG))
             for j in range(2)], axis=1)
        qs.append(q2)
        ks.append(k2)
        kbs.append(kb2)
        egs.append(eg2)
        gcs.append(gc2)
        decs.append(dec2)
        rhs.append([jnp.concatenate([v2[:, j * GDN_DK:(j + 1) * GDN_DK] * beta2[:, j * GDN_DK:(j + 1) * GDN_DK],
                                     kb2[:, j * GDN_DK:(j + 1) * GDN_DK] * eg2[:, j * GDN_DK:(j + 1) * GDN_DK]],
                                    axis=1).astype(BF16) for j in range(2)])
        sts.append(s_ref[c])

    a_mats, attns = [], []
    for p in range(n_chains):
        kq = _dot_nt(jnp.concatenate([kbs[p], qs[p]], axis=0).astype(BF16), _pair_blockdiag(ks[p].astype(BF16)))
        a_mats.append(jnp.where(strict2, kq[:rows] * decs[p], 0.0))
        attns.append((kq[rows:] * decs[p]).astype(BF16))

    half = rows // 2
    rq = _iota((half, pw), 0)
    lq = _iota((half, pw), 1)
    cq = jnp.bitwise_and(lq, half - 1)
    low_half = jnp.bitwise_and(lq, rows - 1) < half
    blk_q = lq // half

    def quad_blockdiag(y):
        return jnp.concatenate([jnp.where(blk_q == b, y, jnp.zeros_like(y)) for b in range(pw // half)], axis=0)

    aqs = [jnp.where(low_half, a_mats[p][:half], a_mats[p][half:]) for p in range(n_chains)]
    tqs = [jnp.where(rq == cq, 1.0, 0.0) - jnp.where((rq // 2) == (cq // 2), aqs[p], 0.0) for p in range(n_chains)]
    s = 2
    while s < half:
        off = jnp.logical_and((rq // (2 * s)) == (cq // (2 * s)), (rq // s) != (cq // s))
        xs = [_dot(jnp.where(off, aqs[p], 0.0).astype(BF16), quad_blockdiag(tqs[p].astype(BF16)))
              for p in range(n_chains)]
        tqs = [tqs[p] - _dot(tqs[p].astype(BF16), quad_blockdiag(xs[p].astype(BF16))) for p in range(n_chains)]
        s *= 2
    ts = [jnp.concatenate([jnp.where(low_half, tqs[p], 0.0), jnp.where(low_half, 0.0, tqs[p])], axis=0)
          for p in range(n_chains)]
    off = (ri2 // half) != (ci2 // half)
    xs = [_dot(jnp.where(off, a_mats[p], 0.0).astype(BF16), _pair_blockdiag(ts[p].astype(BF16)))
          for p in range(n_chains)]
    ts = [ts[p] - _dot(ts[p].astype(BF16), _pair_blockdiag(xs[p].astype(BF16))) for p in range(n_chains)]

    us, ws = [], []
    for p in range(n_chains):
        tb = ts[p].astype(BF16)
        sol = [_dot(tb[:, j * GDN_DK:(j + 1) * GDN_DK], rhs[p][j]) for j in range(2)]
        us.append(jnp.concatenate([sol[0][:, :GDN_DK], sol[1][:, :GDN_DK]], axis=1))
        ws.append(jnp.concatenate([sol[0][:, GDN_DK:], sol[1][:, GDN_DK:]], axis=1))

    vns, outs = [], []
    for p in range(n_chains):
        wq = jnp.concatenate([ws[p], qs[p] * egs[p]], axis=0).astype(BF16)
        ws_qs = _dot(wq, _pair_blockdiag(sts[p].astype(BF16)))
        vns.append((us[p] - ws_qs[:rows]).astype(BF16))
        outs.append(ws_qs[rows:])
    for c, (bb, p) in enumerate(chains):
        o2 = outs[c] + _dot(attns[c], _pair_blockdiag(vns[c]))
        g_end = gcs[c][edge:edge + 1, :]
        k_dec = (ks[c] * jnp.exp(g_end - gcs[c])).astype(BF16)
        full = _dot_tn(k_dec, vns[c])
        upd = jnp.concatenate([full[:GDN_DK, :GDN_DK], full[GDN_DK:, GDN_DK:]], axis=1)
        s_ref[c] = sts[c] * jnp.exp(g_end) + upd
        sl = slice(p * pw, (p + 1) * pw)
        if final:
            o2 = o2 + of_ref[bb, :, sl]
            ys = []
            for j in range(2):
                oj = o2[:, j * GDN_DK:(j + 1) * GDN_DK]
                ys.append(oj * lax.rsqrt(jnp.mean(oj * oj, axis=-1, keepdims=True) + NORM_EPS) * nw_ref[...])
            o_ref[bb, :, sl] = (jnp.concatenate(ys, axis=1) * _silu(z_ref[bb, :, sl])).astype(o_ref.dtype)
        else:
            o_ref[bb, :, sl] = o2


def _gdn_scan(q, k, v, gcol, grow, o_fwd, z, norm_w, n_lat, n_ctx, reverse):
    bsz, l, hw = q.shape
    final = o_fwd is not None
    chunk_fn = functools.partial(_chunk_of_step, n_lat=n_lat, n_ctx=n_ctx, reverse=reverse)
    nb = next(n for n in (4, 2, 1) if bsz % n == 0)
    tok = pl.BlockSpec((nb, SCAN_CHUNK, hw), lambda b, i: (b, chunk_fn(i), 0))
    in_specs = [tok, tok, tok,
                pl.BlockSpec((nb, SCAN_CHUNK, LANE), lambda b, i: (b, chunk_fn(i), 0)),
                pl.BlockSpec((nb, 4 * GDN_HEADS, SCAN_CHUNK), lambda b, i: (b, 0, chunk_fn(i)))]
    args = [q, k, v, gcol, grow]
    if final:
        in_specs += [tok, tok, pl.BlockSpec((1, GDN_DK), lambda b, i: (0, 0))]
        args += [o_fwd, z, norm_w]
    return pl.pallas_call(
        functools.partial(_gdn_scan_kernel, reverse=reverse, final=final, n_lat=n_lat, n_ctx=n_ctx),
        grid=(bsz // nb, l // SCAN_CHUNK),
        in_specs=in_specs,
        out_specs=tok,
        out_shape=jax.ShapeDtypeStruct((bsz, l, hw), BF16 if final else F32),
        scratch_shapes=[pltpu.VMEM((nb * GDN_HEADS // 2, GDN_DK, 2 * GDN_DK), F32)],
        compiler_params=_cparams("arbitrary", "arbitrary"),
        name="gdn_rev" if reverse else "gdn_fwd",
    )(*args)


def _in_ssd_kernel(xl_ref, xlp_ref, xln_ref, xc_ref, xcp_ref, xcn_ref, mod_ref, nw_ref, w_ref, cw_ref, cb_ref,
                   z_ref, u_ref, dt_ref, xe_ref, *, n_lat, n_ctx):
    j = pl.program_id(1)
    first, last = _segment_edges(j, n_lat, n_ctx)
    is_ctx = j >= n_lat
    pick = lambda c_ref, l_ref: jnp.where(is_ctx, c_ref[...], l_ref[...])
    h_main, h_all = _normed_rows(pick(xc_ref, xl_ref), pick(xcp_ref, xlp_ref), pick(xcn_ref, xln_ref),
                                 mod_ref, nw_ref)
    inner = z_ref.shape[1]
    xbc_w = u_ref.shape[1]
    for c0 in range(0, inner, 512):
        z_ref[:, c0:c0 + 512] = _dot(h_main, w_ref[:, c0:c0 + 512])
    for c0, c1, u in _proj_conv(xe_ref, h_all, w_ref, cw_ref, inner, xbc_w, first, last, ROW_TILE):
        u_ref[:, c0:c1] = _silu(u + cb_ref[:, c0:c1])
    dt_ref[...] = _dot(h_main, w_ref[:, inner + xbc_w:inner + xbc_w + LANE])


def _in_ssd(x_lat, x, modtab, norm_w, w, conv_w, conv_b, n_lat, n_ctx):
    bsz, l, d = x.shape
    inner = SSD_HEADS * SSD_HEADDIM
    xbc_w = inner + 2 * SSD_GROUPS * SSD_STATE
    tok = lambda n: pl.BlockSpec((None, ROW_TILE, n), lambda b, j: (b, j, 0))
    lat_tile = lambda j: jnp.minimum(j, n_lat - 1)
    ctx_tile = lambda j: jnp.maximum(j, n_lat)
    lat_specs = [pl.BlockSpec((None, ROW_TILE, d), lambda b, j: (b, lat_tile(j), 0)),
                 *_halo_specs(d, ROW_TILE, x_lat.shape[1], lat_tile)]
    ctx_specs = [pl.BlockSpec((None, ROW_TILE, d), lambda b, j: (b, ctx_tile(j), 0)),
                 *_halo_specs(d, ROW_TILE, l, ctx_tile)]
    return pl.pallas_call(
        functools.partial(_in_ssd_kernel, n_lat=n_lat, n_ctx=n_ctx),
        grid=(bsz, l // ROW_TILE),
        in_specs=lat_specs + ctx_specs + _row_tile_specs(d, l, n_lat)[3:] + [
            pl.BlockSpec((d, w.shape[1]), lambda b, j: (0, 0)),
            pl.BlockSpec((CONV_K, xbc_w), lambda b, j: (0, 0)),
            pl.BlockSpec((1, xbc_w), lambda b, j: (0, 0))],
        out_specs=[tok(inner), tok(xbc_w), tok(LANE)],
        out_shape=[jax.ShapeDtypeStruct((bsz, l, inner), F32), jax.ShapeDtypeStruct((bsz, l, xbc_w), F32),
                   jax.ShapeDtypeStruct((bsz, l, LANE), F32)],
        scratch_shapes=[pltpu.VMEM((ROW_TILE + 2 * SUBLANE, xbc_w), F32)],
        compiler_params=_cparams("arbitrary", "arbitrary"),
        name="in_ssd",
    )(x_lat, x_lat, x_lat, x, x, x, modtab, norm_w, w, conv_w, conv_b)


def _ssd_scan_kernel(*refs, reverse, final):
    if final:
        z_ref, u_ref, dt_ref, dtb_ref, a_ref, dsk_ref, yf_ref, nw_ref, o_ref, st_ref = refs
    else:
        u_ref, dt_ref, dtb_ref, a_ref, dsk_ref, o_ref, st_ref = refs
    rows = SCAN_CHUNK
    inner = SSD_HEADS * SSD_HEADDIM
    gw = inner // SSD_GROUPS
    i = pl.program_id(1)

    @pl.when(i == 0)
    def _():
        st_ref[...] = jnp.zeros_like(st_ref)

    ri = _iota((rows, rows), 0)
    ci = _iota((rows, rows), 1)
    incl = (ri <= ci) if reverse else (ri >= ci)
    tri = jnp.where(incl, 1.0, 0.0).astype(BF16)
    eye = jnp.where(ri == ci, 1.0, 0.0).astype(BF16)
    d = 1 if reverse else 0
    edge = 0 if reverse else rows - 1
    er = _iota((LANE, inner), 0)
    ec = _iota((LANE, inner), 1)
    expand = jnp.where(er == d * SSD_HEADS + ec // SSD_HEADDIM, 1.0, 0.0).astype(BF16)
    lane = _iota((rows, LANE), 1)
    heads_per_group = SSD_HEADS // SSD_GROUPS
    samples = range(u_ref.shape[0])
    units = [(bb, g) for bb in samples for g in range(SSD_GROUPS)]

    us, xss, acss, acs_ts, ea_es, xdt_bs, xw_bs = [], [], [], [], [], [], []
    for bb in samples:
        u = u_ref[bb]
        xs = u[:, :inner]
        dt = _softplus(dt_ref[bb] + dtb_ref[...])
        acs = _exact_left(tri, a_ref[...] * dt)
        xdt = xs * _exact_right(dt, expand)
        us.append(u)
        xss.append(xs)
        acss.append(acs)
        acs_ts.append(_exact_transpose(acs, eye))
        ea_es.append(_exact_right(jnp.exp(acs), expand))
        xdt_bs.append(xdt.astype(BF16))
        xw_bs.append((xdt * _exact_right(jnp.exp(acs[edge:edge + 1, :] - acs), expand)).astype(BF16))

    bms, cbs, sts, y_offs = [], [], [], []
    for c, (bb, g) in enumerate(units):
        u = us[bb]
        bm = u[:, inner + g * SSD_STATE:inner + (g + 1) * SSD_STATE].astype(BF16)
        cm = u[:, inner + (SSD_GROUPS + g) * SSD_STATE:inner + (SSD_GROUPS + g + 1) * SSD_STATE].astype(BF16)
        st = st_ref[c]
        bms.append(bm)
        cbs.append(_dot_nt(cm, bm))
        sts.append(st)
        y_offs.append(_dot(cm, st.astype(BF16)) * ea_es[bb][:, g * gw:(g + 1) * gw])

    diags = [[] for _ in units]
    for hp in range(heads_per_group // 2):
        for c, (bb, g) in enumerate(units):
            h0 = g * heads_per_group + 2 * hp
            mats = []
            for hh in (h0, h0 + 1):
                li = d * SSD_HEADS + hh
                lm = jnp.exp(jnp.where(incl, acss[bb][:, li:li + 1] - acs_ts[bb][li:li + 1, :], NEG_BIG))
                mats.append((cbs[c] * lm).astype(BF16))
            xp = xdt_bs[bb][:, h0 * SSD_HEADDIM:(h0 + 2) * SSD_HEADDIM]
            x_lo = jnp.where(lane < SSD_HEADDIM, xp, jnp.zeros_like(xp))
            x_hi = jnp.where(lane >= SSD_HEADDIM, xp, jnp.zeros_like(xp))
            diags[c].append(_dot(jnp.concatenate(mats, axis=1), jnp.concatenate([x_lo, x_hi], axis=0)))

    for c, (bb, g) in enumerate(units):
        st_ref[c] = (sts[c] * ea_es[bb][edge:edge + 1, g * gw:(g + 1) * gw]
                     + _dot_tn(bms[c], xw_bs[bb][:, g * gw:(g + 1) * gw]))
    for bb in samples:
        y_parts = [jnp.concatenate(diags[bb * SSD_GROUPS + g], axis=1) + y_offs[bb * SSD_GROUPS + g]
                   for g in range(SSD_GROUPS)]
        y = jnp.concatenate(y_parts, axis=1) + dsk_ref[...] * xss[bb]
        if final:
            y = y + yf_ref[bb]
            yz = y * _silu(z_ref[bb])
            outs = []
            for g in range(SSD_GROUPS):
                yg = yz[:, g * gw:(g + 1) * gw]
                outs.append(yg * lax.rsqrt(jnp.mean(yg * yg, axis=-1, keepdims=True) + NORM_EPS))
            o_ref[bb] = (jnp.concatenate(outs, axis=1) * nw_ref[...]).astype(o_ref.dtype)
        else:
            o_ref[bb] = y


def _ssd_scan(z, u, dt_raw, dtb_vec, a_vec, dskip, y_fwd, norm_w, n_lat, n_ctx, reverse):
    bsz, l, xbc_w = u.shape
    inner = SSD_HEADS * SSD_HEADDIM
    final = y_fwd is not None
    chunk_fn = functools.partial(_chunk_of_step, n_lat=n_lat, n_ctx=n_ctx, reverse=reverse)
    nb = 2 if bsz % 2 == 0 else 1
    tok = lambda n: pl.BlockSpec((nb, SCAN_CHUNK, n), lambda b, i: (b, chunk_fn(i), 0))
    row = lambda n: pl.BlockSpec((1, n), lambda b, i: (0, 0))
    in_specs = [tok(xbc_w), tok(LANE), row(LANE), row(LANE), row(inner)]
    args = [u, dt_raw, dtb_vec, a_vec, dskip]
    if final:
        in_specs = [tok(inner)] + in_specs + [tok(inner), row(inner)]
        args = [z] + args + [y_fwd, norm_w]
    return pl.pallas_call(
        functools.partial(_ssd_scan_kernel, reverse=reverse, final=final),
        grid=(bsz // nb, l // SCAN_CHUNK),
        in_specs=in_specs,
        out_specs=tok(inner),
        out_shape=jax.ShapeDtypeStruct((bsz, l, inner), BF16 if final else F32),
        scratch_shapes=[pltpu.VMEM((nb * SSD_GROUPS, SSD_STATE, inner // SSD_GROUPS), F32)],
        compiler_params=_cparams("arbitrary", "arbitrary"),
        name="ssd_rev" if reverse else "ssd_fwd",
    )(*args)


def _merge_kernel(x_ref, lru_ref, gdn_ref, ssdl_ref, ssdc_ref, gate_ref, mod_ref, nw_ref, wb_ref, wo_ref, wr_ref,
                  br_ref, xo_ref, h_ref, lg_ref, *, n_lat):
    d = x_ref.shape[1]
    ssd = jnp.where(pl.program_id(1) >= n_lat, ssdc_ref[...], ssdl_ref[...])
    acc = None
    for k, b in enumerate((lru_ref[...], gdn_ref[...], ssd)):
        t = jax.nn.sigmoid(gate_ref[:, k * d:(k + 1) * d].astype(F32)) * _dot(b.astype(BF16), wb_ref[k])
        acc = t if acc is None else acc + t
    mix = _dot(acc.astype(BF16), wo_ref[...])
    x = x_ref[...] + mod_ref[2:3, :] * mix
    xo_ref[...] = x
    h = _mod_norm(x, nw_ref[...], mod_ref[3:4, :], mod_ref[4:5, :])
    hb = h.astype(BF16)
    h_ref[...] = hb
    lg_ref[...] = _dot(hb, wr_ref[...]) + br_ref[...]


def _merge(x, lru, gdn, ssd_lat, ssd_scan, gate, modtab, norm_w, wb, wo, wr, br, n_lat_tiles):
    bsz, l, d = x.shape
    tok = pl.BlockSpec((None, ROW_TILE, d), lambda b, j: (b, j, 0))
    return pl.pallas_call(
        functools.partial(_merge_kernel, n_lat=n_lat_tiles),
        grid=(bsz, l // ROW_TILE),
        in_specs=[tok, tok, tok,
                  pl.BlockSpec((None, ROW_TILE, d), lambda b, j: (b, jnp.minimum(j, n_lat_tiles - 1), 0)),
                  pl.BlockSpec((None, ROW_TILE, d), lambda b, j: (b, jnp.maximum(j, n_lat_tiles), 0)),
                  pl.BlockSpec((None, ROW_TILE, 3 * d), lambda b, j: (b, j, 0)),
                  pl.BlockSpec((None, None, SUBLANE, d), lambda b, j: (b, jnp.where(j >= n_lat_tiles, 1, 0), 0, 0)),
                  pl.BlockSpec((1, d), lambda b, j: (0, 0)),
                  pl.BlockSpec((3, d, d), lambda b, j: (0, 0, 0)),
                  pl.BlockSpec((d, d), lambda b, j: (0, 0)),
                  pl.BlockSpec((d, LANE), lambda b, j: (0, 0)),
                  pl.BlockSpec((1, LANE), lambda b, j: (0, 0))],
        out_specs=[tok, tok, pl.BlockSpec((None, ROW_TILE, LANE), lambda b, j: (b, j, 0))],
        out_shape=[jax.ShapeDtypeStruct((bsz, l, d), F32), jax.ShapeDtypeStruct((bsz, l, d), BF16),
                   jax.ShapeDtypeStruct((bsz, l, LANE), F32)],
        compiler_params=_cparams("arbitrary", "arbitrary"),
        name="merge",
    )(x, lru, gdn, ssd_lat, ssd_scan, gate, modtab, norm_w, wb, wo, wr, br)


def _moe_kernel(blk_ref, exp_ref, lo_ref, hi_ref, x_ref, w1_ref, w3_ref, w2_ref, o_ref,
                acc_ref, w1b_ref, w3b_ref, w2b_ref):
    i = pl.program_id(0)
    lo = lo_ref[i]
    hi = hi_ref[i]
    base = blk_ref[i] * MOE_ROWS

    @pl.when(jnp.logical_or(i == 0, exp_ref[i] != exp_ref[jnp.maximum(i - 1, 0)]))
    def _():
        w1b_ref[...] = w1_ref[...].astype(BF16)
        w3b_ref[...] = w3_ref[...].astype(BF16)
        w2b_ref[...] = w2_ref[...].astype(BF16)

    @pl.when(lo == base)
    def _():
        acc_ref[...] = jnp.zeros_like(acc_ref)

    @pl.when(hi > lo)
    def _():
        row = base + _iota((MOE_ROWS, 1), 0)
        x = jnp.where(jnp.logical_and(row >= lo, row < hi), x_ref[...], jnp.zeros_like(x_ref))
        mid = _silu(_dot(x, w1b_ref[...])) * _dot(x, w3b_ref[...])
        acc_ref[...] += _dot(mid.astype(BF16), w2b_ref[...])

    o_ref[...] = acc_ref[...].astype(o_ref.dtype)


def _moe_experts(xb, item_block, item_expert, item_lo, item_hi, w1, w3, w2, layer):
    n_rows, d = xb.shape
    ff = w1.shape[-1]
    rows_spec = pl.BlockSpec((MOE_ROWS, d), lambda i, blk, ex, lo, hi: (blk[i], 0))
    grid_spec = pltpu.PrefetchScalarGridSpec(
        num_scalar_prefetch=4,
        grid=(item_block.shape[0],),
        in_specs=[rows_spec,
                  pl.BlockSpec((None, None, d, ff), lambda i, blk, ex, lo, hi: (layer, ex[i], 0, 0)),
                  pl.BlockSpec((None, None, d, ff), lambda i, blk, ex, lo, hi: (layer, ex[i], 0, 0)),
                  pl.BlockSpec((None, None, ff, d), lambda i, blk, ex, lo, hi: (layer, ex[i], 0, 0))],
        out_specs=rows_spec,
        scratch_shapes=[pltpu.VMEM((MOE_ROWS, d), F32), pltpu.VMEM((d, ff), BF16), pltpu.VMEM((d, ff), BF16),
                        pltpu.VMEM((ff, d), BF16)],
    )
    return pl.pallas_call(
        _moe_kernel,
        grid_spec=grid_spec,
        out_shape=jax.ShapeDtypeStruct((n_rows, d), BF16),
        compiler_params=_cparams("arbitrary"),
        name="moe_experts",
    )(item_block, item_expert, item_lo, item_hi, xb, w1, w3, w2)


def _route(logits):
    t = logits.shape[0]
    g_logits = logits[:, :N_GROUPS]
    e_logits = logits[:, N_GROUPS:N_GROUPS + N_EXPERTS].reshape(t, N_GROUPS, EXPERTS_PER_GROUP)
    g_prob = jax.nn.softmax(g_logits, axis=-1)
    g_idx = jnp.argmax(g_logits, axis=-1)
    p_group = jnp.take_along_axis(g_prob, g_idx[:, None], axis=1)[:, 0]
    e_in_group = jnp.take_along_axis(e_logits, g_idx[:, None, None], axis=1)[:, 0]
    top_v, top_i = lax.top_k(e_in_group, TOP_K)
    weights = jax.nn.softmax(top_v, axis=-1) * p_group[:, None]
    expert_id = (g_idx[:, None] * EXPERTS_PER_GROUP + top_i).reshape(-1).astype(jnp.int32)
    n_assign = t * TOP_K
    assert n_assign % MOE_ROWS == 0
    n_blocks = n_assign // MOE_ROWS
    ar = jnp.arange(n_assign, dtype=jnp.int32)
    e_sorted, order = lax.sort((expert_id, ar), num_keys=1)
    _, slot = lax.sort((order, ar), num_keys=1)
    ends = jnp.sum(e_sorted[None, :] <= jnp.arange(N_EXPERTS, dtype=jnp.int32)[:, None], axis=1).astype(jnp.int32)
    starts = jnp.concatenate([jnp.zeros((1,), jnp.int32), ends[:-1]])
    cuts = jnp.sort(jnp.concatenate([jnp.arange(n_blocks, dtype=jnp.int32) * MOE_ROWS, starts]))
    item_lo = cuts
    item_hi = jnp.concatenate([cuts[1:], jnp.full((1,), n_assign, jnp.int32)])
    item_block = jnp.minimum(item_lo // MOE_ROWS, n_blocks - 1).astype(jnp.int32)
    item_expert = jnp.minimum(jnp.sum(item_lo[:, None] >= ends[None, :], axis=1), N_EXPERTS - 1).astype(jnp.int32)
    return order // TOP_K, (item_block, item_expert, item_lo, item_hi), slot.reshape(t, TOP_K), weights


def _combine_kernel(x_ref, ya_ref, yb_ref, wt_ref, mod_ref, nw_ref, o_ref, *, final):
    wt = wt_ref[...]
    y = wt[:, 0:1] * ya_ref[...].astype(F32) + wt[:, 1:2] * yb_ref[...].astype(F32)
    x = x_ref[...] + mod_ref[5:6, :] * y
    if final:
        x = x * lax.rsqrt(jnp.mean(x * x, axis=-1, keepdims=True) + NORM_EPS) * nw_ref[...]
    o_ref[...] = x


def _combine(x, ya, yb, wts, modtab, norm_w, n_lat_tiles, final, out_len):
    bsz, l, d = x.shape
    tok = pl.BlockSpec((None, ROW_TILE, d), lambda b, j: (b, j, 0))
    return pl.pallas_call(
        functools.partial(_combine_kernel, final=final),
        grid=(bsz, out_len // ROW_TILE),
        in_specs=[tok, tok, tok,
                  pl.BlockSpec((None, ROW_TILE, LANE), lambda b, j: (b, j, 0)),
                  pl.BlockSpec((None, None, SUBLANE, d), lambda b, j: (b, jnp.where(j >= n_lat_tiles, 1, 0), 0, 0)),
                  pl.BlockSpec((1, d), lambda b, j: (0, 0))],
        out_specs=tok,
        out_shape=jax.ShapeDtypeStruct((bsz, out_len, d), F32),
        compiler_params=_cparams("arbitrary", "arbitrary"),
        name="combine",
    )(x, ya, yb, wts, modtab, norm_w)


def _pad_lanes(v, n=LANE):
    v = v.reshape(1, -1).astype(F32)
    return jnp.pad(v, ((0, 0), (0, n - v.shape[1])))


def kernel(x, c, ctx, c_ctx, w_mod, b_mod, norm1_w, norm2_w, w_in, lru_conv_w, lru_conv_b, lru_wa, lru_ba,
           lru_wi, lru_bi, lru_lambda, gdn_conv_w, gdn_a_log, gdn_dt_bias, gdn_norm_w, ssd_conv_w, ssd_conv_b,
           ssd_a_log, ssd_dt_bias, ssd_d, ssd_norm_w, w_branch, w_out, router_group_w, router_group_b,
           router_expert_w, router_expert_b, expert_w1, expert_w3, expert_w2, final_norm_w):
    bsz, seq, d = x.shape
    n_ctx_tok = ctx.shape[1]
    depth = w_mod.shape[0]
    l = seq + n_ctx_tok
    rows = seq // GRID_W
    assert seq % LRU_CHUNK == 0 and n_ctx_tok % LRU_CHUNK == 0 and ROW_TILE == LRU_CHUNK
    assert bsz + 1 <= SUBLANE
    assert l % GRID_W == 0 and l % rows == 0
    hw = GDN_HEADS * GDN_DK
    inner = SSD_HEADS * SSD_HEADDIM
    xbc_w = inner + 2 * SSD_GROUPS * SSD_STATE

    act = jnp.concatenate([_silu(c), _silu(c_ctx)[None, :],
                           jnp.zeros((SUBLANE - bsz - 1, d), F32)], axis=0)
    mod_all = _modulation(act, w_mod, b_mod).reshape(depth, SUBLANE, 6, d)
    pad2 = jnp.zeros((bsz, 2, d), F32)

    xs = jnp.concatenate([x, ctx], axis=1)
    nl_t, nc_t = seq // ROW_TILE, n_ctx_tok // ROW_TILE
    nl_s, nc_s = seq // SCAN_CHUNK, n_ctx_tok // SCAN_CHUNK

    o_lx, o_qkv, o_gz, o_gb, o_ga = 0, 2 * d, 2 * d + 3 * hw, 2 * d + 4 * hw, 2 * d + 4 * hw + 2 * GDN_HEADS
    o_sz = o_ga + 2 * GDN_HEADS
    o_xbc = o_sz + inner
    o_sdt = o_xbc + xbc_w
    o_gate = o_sdt + 2 * SSD_HEADS

    for i in range(depth):
        lat = mod_all[i, :bsz]
        cx = jnp.broadcast_to(mod_all[i, bsz][None], (bsz, 6, d))
        modtab = jnp.stack([jnp.concatenate([lat, pad2], axis=1), jnp.concatenate([cx, pad2], axis=1)], axis=1)

        wi_ = w_in[i]
        w_lru = wi_[:, o_lx:o_qkv].astype(BF16)
        zpad = jnp.zeros((d, LANE - 4 * GDN_HEADS), F32)
        w_gdn = jnp.concatenate([wi_[:, o_qkv:o_gb], wi_[:, o_ga:o_sz], wi_[:, o_gb:o_ga], zpad], axis=1).astype(BF16)
        zpad2 = jnp.zeros((d, LANE - 2 * SSD_HEADS), F32)
        w_ssd = jnp.concatenate([wi_[:, o_sz:o_gate], zpad2], axis=1).astype(BF16)
        w_gate = wi_[:, o_gate:].astype(BF16)
        n1 = norm1_w[i].reshape(1, d)

        lru_u, lru_y, gate = _in_lru(xs, modtab, n1, w_lru, w_gate, lru_conv_w[i], lru_conv_b[i].reshape(1, d),
                                     nl_t, nc_t)
        h_f = _lru_scan(lru_u, lru_y, lru_wa[i, 0].astype(BF16), lru_ba[i, 0].reshape(1, d),
                        lru_wi[i, 0].astype(BF16), lru_bi[i, 0].reshape(1, d), lru_lambda[i, 0].reshape(1, d),
                        None, nl_t, nc_t, False)
        lru_out = _lru_scan(lru_u, lru_y, lru_wa[i, 1].astype(BF16), lru_ba[i, 1].reshape(1, d),
                            lru_wi[i, 1].astype(BF16), lru_bi[i, 1].reshape(1, d), lru_lambda[i, 1].reshape(1, d),
                            h_f, nl_t, nc_t, True)

        q, k, v, gz, gcol, grow = _in_gdn(xs, modtab, n1, w_gdn, gdn_conv_w[i], _pad_lanes(gdn_a_log[i]),
                                          _pad_lanes(gdn_dt_bias[i]), nl_t, nc_t)
        o_f = _gdn_scan(q, k, v, gcol, grow, None, None, None, nl_s, nc_s, False)
        gdn_out = _gdn_scan(q, k, v, gcol, grow, o_f, gz, gdn_norm_w[i].reshape(1, GDN_DK), nl_s, nc_s, True)

        x_cm = xs.reshape(bsz, l // GRID_W, GRID_W, d)[:, :rows].swapaxes(1, 2).reshape(bsz, seq, d)
        sz, su, sdt = _in_ssd(x_cm, xs, modtab, n1, w_ssd, ssd_conv_w[i], ssd_conv_b[i].reshape(1, xbc_w),
                              nl_t, nc_t)
        dtb_vec = _pad_lanes(ssd_dt_bias[i])
        a_vec = _pad_lanes(-jnp.exp(ssd_a_log[i].astype(F32)))
        dsk = [jnp.repeat(ssd_d[i, dd], SSD_HEADDIM).reshape(1, inner) for dd in range(2)]
        y_f = _ssd_scan(None, su, sdt, dtb_vec, a_vec, dsk[0], None, None, nl_s, nc_s, False)
        ssd_scan_out = _ssd_scan(sz, su, sdt, dtb_vec, a_vec, dsk[1], y_f,
                                 ssd_norm_w[i].reshape(1, inner), nl_s, nc_s, True)
        ssd_lat = (ssd_scan_out.reshape(bsz, l // rows, rows, inner)[:, :GRID_W].swapaxes(1, 2)
                   .reshape(bsz, seq, inner))

        w_r = jnp.concatenate([router_group_w[i], router_expert_w[i],
                               jnp.zeros((d, LANE - N_GROUPS - N_EXPERTS), F32)], axis=1).astype(BF16)
        b_r = _pad_lanes(jnp.concatenate([router_group_b[i], router_expert_b[i]]))
        x_mid, h2, logits = _merge(xs, lru_out, gdn_out, ssd_lat, ssd_scan_out, gate, modtab, norm2_w[i].reshape(1, d),
                                   w_branch[i].astype(BF16), w_out[i].astype(BF16), w_r, b_r, nl_t)

        t = bsz * l
        tok_sorted, items, slot, weights = _route(logits.reshape(t, LANE))
        xb = h2.reshape(t, d)[tok_sorted]
        yb = _moe_experts(xb, *items, expert_w1, expert_w3, expert_w2, i)
        y0 = yb[slot[:, 0]].reshape(bsz, l, d)
        y1 = yb[slot[:, 1]].reshape(bsz, l, d)
        wts = jnp.pad(weights, ((0, 0), (0, LANE - TOP_K))).reshape(bsz, l, LANE)
        last = i == depth - 1
        xs = _combine(x_mid, y0, y1, wts, modtab, final_norm_w.reshape(1, d), nl_t, last, seq if last else l)

    return xs
```

```python
import functools

import jax
import jax.numpy as jnp
from jax import lax
from jax.experimental import pallas as pl
from jax.experimental.pallas import tpu as pltpu

GRID_W = 64
CONV_K = 4
NORM_EPS = 1e-6
LRU_BLOCKS = 8
LRU_C = 8.0
GDN_HEADS = 8
GDN_DK = 128
SSD_HEADS = 16
SSD_HEADDIM = 64
SSD_GROUPS = 2
SSD_STATE = 128
N_GROUPS = 4
EXPERTS_PER_GROUP = 8
N_EXPERTS = N_GROUPS * EXPERTS_PER_GROUP
TOP_K = 2

LANE = 128
SUBLANE = 8
ROW_TILE = 256
LRU_CHUNK = 256
SCAN_CHUNK = 128
MOE_ROWS = 512
CONV_COLS = 512
MOD_COLS = 1536
VMEM_LIMIT = 56 * 1024 * 1024

F32 = jnp.float32
BF16 = jnp.bfloat16
NEG_BIG = -1e30


def _cparams(*sem):
    return pltpu.CompilerParams(dimension_semantics=sem, vmem_limit_bytes=VMEM_LIMIT)


def _dot(a, b):
    return jnp.dot(a, b, preferred_element_type=F32)


def _dot_nt(a, b):
    return lax.dot_general(a, b, (((1,), (1,)), ((), ())), preferred_element_type=F32)


def _dot_tn(a, b):
    return lax.dot_general(a, b, (((0,), (0,)), ((), ())), preferred_element_type=F32)


def _split3(x):
    hi = x.astype(BF16)
    r1 = x - hi.astype(F32)
    mid = r1.astype(BF16)
    lo = (r1 - mid.astype(F32)).astype(BF16)
    return hi, mid, lo


def _exact_left(m_bf, x):
    hi, mid, lo = _split3(x)
    return _dot(jnp.concatenate([m_bf, m_bf, m_bf], axis=1), jnp.concatenate([hi, mid, lo], axis=0))


def _exact_right(x, m_bf):
    hi, mid, lo = _split3(x)
    return _dot(jnp.concatenate([hi, mid, lo], axis=1), jnp.concatenate([m_bf, m_bf, m_bf], axis=0))


def _exact_transpose(x, eye_bf):
    hi, mid, lo = _split3(x)
    return _dot_nt(jnp.concatenate([eye_bf, eye_bf, eye_bf], axis=1), jnp.concatenate([hi, mid, lo], axis=1))


def _softplus(x):
    return jnp.maximum(x, 0.0) + jnp.log(1.0 + jnp.exp(-jnp.abs(x)))


def _silu(x):
    return x * jax.nn.sigmoid(x)


def _iota(shape, dim):
    return lax.broadcasted_iota(jnp.int32, shape, dim)


def _fwd_chunk(i, n_lat, n_ctx):
    return jnp.where(i < n_ctx, n_lat + i, i - n_ctx)


def _chunk_of_step(i, n_lat, n_ctx, reverse):
    return (n_lat + n_ctx - 1 - i) if reverse else _fwd_chunk(i, n_lat, n_ctx)


def _segment_edges(c, n_lat, n_ctx):
    first = jnp.logical_or(c == 0, c == n_lat)
    last = jnp.logical_or(c == n_lat - 1, c == n_lat + n_ctx - 1)
    return first, last


def _normed_rows(x, x_prev, x_next, mod_ref, nw_ref):
    shift, scale = mod_ref[0:1, :], mod_ref[1:2, :]
    h_main = _mod_norm(x, nw_ref[...], shift, scale)
    h_prev = _mod_norm(x_prev, nw_ref[...], shift, scale)
    h_next = _mod_norm(x_next, nw_ref[...], shift, scale)
    return h_main.astype(BF16), jnp.concatenate([h_prev, h_main, h_next], axis=0).astype(BF16)


def _proj_conv(xe_refs, h_all, w_ref, cw_ref, col0, width, first, last, rows):
    base = SUBLANE - CONV_K // 2
    for xe_ref, c0 in zip(xe_refs, range(0, width, CONV_COLS), strict=True):
        c1 = c0 + CONV_COLS
        y = _dot(h_all, w_ref[:, col0 + c0:col0 + c1])
        xe_ref[0:SUBLANE, :] = jnp.where(first, 0.0, y[0:SUBLANE])
        xe_ref[SUBLANE:SUBLANE + rows, :] = y[SUBLANE:SUBLANE + rows]
        xe_ref[SUBLANE + rows:2 * SUBLANE + rows, :] = jnp.where(last, 0.0, y[SUBLANE + rows:])
        u = cw_ref[0:1, c0:c1] * xe_ref[base:base + rows, :]
        for j in range(1, CONV_K):
            u = u + cw_ref[j:j + 1, c0:c1] * xe_ref[base + j:base + j + rows, :]
        yield c0, c1, u


def _conv_scratch(width):
    assert width % CONV_COLS == 0
    return [pltpu.VMEM((ROW_TILE + 2 * SUBLANE, CONV_COLS), F32)] * (width // CONV_COLS)


def _halo_specs(width, rows, n_rows_total, chunk_fn, col_block=0):
    per = rows // SUBLANE
    last_tile = n_rows_total // SUBLANE - 1
    prev = pl.BlockSpec((None, SUBLANE, width),
                        lambda b, i: (b, jnp.maximum(chunk_fn(i) * per - 1, 0), col_block))
    nxt = pl.BlockSpec((None, SUBLANE, width),
                       lambda b, i: (b, jnp.minimum((chunk_fn(i) + 1) * per, last_tile), col_block))
    return prev, nxt


def _mod_kernel(a_ref, w_ref, b_ref, o_ref):
    o_ref[...] = _dot(a_ref[...].astype(BF16), w_ref[...].astype(BF16)) + b_ref[...]


def _modulation(act, w_mod, b_mod):
    depth, d, n = w_mod.shape
    tn = MOD_COLS
    assert n % tn == 0
    return pl.pallas_call(
        _mod_kernel,
        grid=(depth, n // tn),
        in_specs=[pl.BlockSpec((SUBLANE, d), lambda l, j: (0, 0)),
                  pl.BlockSpec((None, d, tn), lambda l, j: (l, 0, j)),
                  pl.BlockSpec((None, 1, tn), lambda l, j: (l, 0, j))],
        out_specs=pl.BlockSpec((None, SUBLANE, tn), lambda l, j: (l, 0, j)),
        out_shape=jax.ShapeDtypeStruct((depth, SUBLANE, n), F32),
        compiler_params=_cparams("arbitrary", "arbitrary"),
        name="modulation",
    )(act, w_mod, b_mod.reshape(depth, 1, n))


def _mod_norm(x, nw, shift, scale):
    y = x * lax.rsqrt(jnp.mean(x * x, axis=-1, keepdims=True) + NORM_EPS)
    return (y * nw) * (1.0 + scale) + shift


def _row_tile_specs(d, l, n_lat_tiles):
    ident = lambda j: j
    main = pl.BlockSpec((None, ROW_TILE, d), lambda b, j: (b, j, 0))
    prev, nxt = _halo_specs(d, ROW_TILE, l, ident)
    mod = pl.BlockSpec((None, None, SUBLANE, d), lambda b, j: (b, jnp.where(j >= n_lat_tiles, 1, 0), 0, 0))
    return [main, prev, nxt, mod, pl.BlockSpec((1, d), lambda b, j: (0, 0))]


def _in_lru_kernel(x_ref, xp_ref, xn_ref, mod_ref, nw_ref, w_ref, wg_ref, cw_ref, cb_ref, u_ref, y_ref, g_ref,
                   *xe_refs, n_lat, n_ctx):
    first, last = _segment_edges(pl.program_id(1), n_lat, n_ctx)
    h_main, h_all = _normed_rows(x_ref[...], xp_ref[...], xn_ref[...], mod_ref, nw_ref)
    w = u_ref.shape[1]
    for c0, c1, u in _proj_conv(xe_refs, h_all, w_ref, cw_ref, 0, w, first, last, ROW_TILE):
        u_ref[:, c0:c1] = u + cb_ref[:, c0:c1]
    for c0 in range(0, w, CONV_COLS):
        y_ref[:, c0:c0 + CONV_COLS] = _dot(h_main, w_ref[:, w + c0:w + c0 + CONV_COLS])
    for c0 in range(0, g_ref.shape[1], CONV_COLS):
        g_ref[:, c0:c0 + CONV_COLS] = _dot(h_main, wg_ref[:, c0:c0 + CONV_COLS]).astype(g_ref.dtype)


def _in_lru(x, modtab, norm_w, w, w_gate, conv_w, conv_b, n_lat, n_ctx):
    bsz, l, d = x.shape
    wd = w.shape[1] // 2
    ng = w_gate.shape[1]
    tok = pl.BlockSpec((None, ROW_TILE, wd), lambda b, j: (b, j, 0))
    return pl.pallas_call(
        functools.partial(_in_lru_kernel, n_lat=n_lat, n_ctx=n_ctx),
        grid=(bsz, l // ROW_TILE),
        in_specs=_row_tile_specs(d, l, n_lat) + [pl.BlockSpec((d, 2 * wd), lambda b, j: (0, 0)),
                                                 pl.BlockSpec((d, ng), lambda b, j: (0, 0)),
                                                 pl.BlockSpec((CONV_K, wd), lambda b, j: (0, 0)),
                                                 pl.BlockSpec((1, wd), lambda b, j: (0, 0))],
        out_specs=[tok, tok, pl.BlockSpec((None, ROW_TILE, ng), lambda b, j: (b, j, 0))],
        out_shape=[jax.ShapeDtypeStruct((bsz, l, wd), F32)] * 2 + [jax.ShapeDtypeStruct((bsz, l, ng), BF16)],
        scratch_shapes=_conv_scratch(wd),
        compiler_params=_cparams("arbitrary", "arbitrary"),
        name="in_lru",
    )(x, x, x, modtab, norm_w, w, w_gate, conv_w, conv_b)


def _lru_kernel(*refs, reverse, final):
    if final:
        u_ref, wa_ref, ba_ref, wi_ref, bi_ref, lam_ref, hf_ref, y_ref, o_ref, carry_ref = refs
    else:
        u_ref, wa_ref, ba_ref, wi_ref, bi_ref, lam_ref, o_ref, carry_ref = refs
    rows = LRU_CHUNK
    i = pl.program_id(1)

    @pl.when(i == 0)
    def _():
        carry_ref[...] = jnp.zeros_like(carry_ref)

    u = u_ref[...]
    ub = u.astype(BF16)
    bw = u.shape[1] // LRU_BLOCKS
    pre_r = jnp.concatenate([_dot(ub[:, n * bw:(n + 1) * bw], wa_ref[n]) for n in range(LRU_BLOCKS)], axis=1)
    pre_i = jnp.concatenate([_dot(ub[:, n * bw:(n + 1) * bw], wi_ref[n]) for n in range(LRU_BLOCKS)], axis=1)
    r = jax.nn.sigmoid(pre_r + ba_ref[...])
    gi = jax.nn.sigmoid(pre_i + bi_ref[...])
    a = jnp.exp(-LRU_C * r * _softplus(-lam_ref[...]))
    h = jnp.sqrt(1.0 - a * a) * gi * u

    srow = _iota((SUBLANE, a.shape[1]), 0)
    n_tiles = rows // SUBLANE
    carry = carry_ref[...]
    done = {}
    for k in (range(n_tiles - 1, -1, -1) if reverse else range(n_tiles)):
        sl = slice(k * SUBLANE, (k + 1) * SUBLANE)
        at, ht = a[sl], h[sl]
        for s in (1, 2, 4):
            if reverse:
                keep = srow < SUBLANE - s
                sh = SUBLANE - s
            else:
                keep = srow >= s
                sh = s
            h_s = jnp.where(keep, pltpu.roll(ht, sh, axis=0), 0.0)
            a_s = jnp.where(keep, pltpu.roll(at, sh, axis=0), 1.0)
            ht = ht + at * h_s
            at = at * a_s
        hk = ht + at * carry
        carry = hk[0:1, :] if reverse else hk[SUBLANE - 1:SUBLANE, :]
        if final:
            done[k] = (hk + hf_ref[sl, :]) * jax.nn.gelu(y_ref[sl, :], approximate=True)
            if (k ^ 1) in done:
                lo = k & ~1
                pair = jnp.concatenate([done.pop(lo), done.pop(lo + 1)], axis=0)
                o_ref[lo * SUBLANE:(lo + 2) * SUBLANE, :] = pair.astype(o_ref.dtype)
        else:
            o_ref[sl, :] = hk
    carry_ref[...] = carry


def _lru_scan(u, y, wa, ba, wi, bi, lam, hf, n_lat, n_ctx, reverse):
    bsz, l, w = u.shape
    final = hf is not None
    chunk_fn = functools.partial(_chunk_of_step, n_lat=n_lat, n_ctx=n_ctx, reverse=reverse)
    main = pl.BlockSpec((None, LRU_CHUNK, w), lambda b, i: (b, chunk_fn(i), 0))
    vec = pl.BlockSpec((1, w), lambda b, i: (0, 0))
    blk = pl.BlockSpec((LRU_BLOCKS, w // LRU_BLOCKS, w // LRU_BLOCKS), lambda b, i: (0, 0, 0))
    in_specs = [main, blk, vec, blk, vec, vec]
    args = [u, wa, ba, wi, bi, lam]
    if final:
        in_specs += [main, main]
        args += [hf, y]
    return pl.pallas_call(
        functools.partial(_lru_kernel, reverse=reverse, final=final),
        grid=(bsz, l // LRU_CHUNK),
        in_specs=in_specs,
        out_specs=main,
        out_shape=jax.ShapeDtypeStruct((bsz, l, w), BF16 if final else F32),
        scratch_shapes=[pltpu.VMEM((1, w), F32)],
        compiler_params=_cparams("arbitrary", "arbitrary"),
        name="lru_rev" if reverse else "lru_fwd",
    )(*args)


def _in_gdn_kernel(x_ref, xp_ref, xn_ref, mod_ref, nw_ref, w_ref, cw_ref, alog_ref, dtb_ref,
                   q_ref, k_ref, v_ref, z_ref, gcol_ref, grow_ref, *xe_refs, n_lat, n_ctx):
    rows = ROW_TILE
    first, last = _segment_edges(pl.program_id(1), n_lat, n_ctx)
    h_main, h_all = _normed_rows(x_ref[...], xp_ref[...], xn_ref[...], mod_ref, nw_ref)
    hw = GDN_HEADS * GDN_DK
    for c0, c1, u in _proj_conv(xe_refs, h_all, w_ref, cw_ref, 0, 3 * hw, first, last, rows):
        u = _silu(u)
        for h0 in range(c0, c1, GDN_DK):
            uh = u[:, h0 - c0:h0 - c0 + GDN_DK]
            if h0 < 2 * hw:
                uh = uh * lax.rsqrt(jnp.sum(uh * uh, axis=-1, keepdims=True) + NORM_EPS)
            if h0 < hw:
                q_ref[:, h0:h0 + GDN_DK] = uh * (GDN_DK ** -0.5)
            elif h0 < 2 * hw:
                k_ref[:, h0 - hw:h0 - hw + GDN_DK] = uh
            else:
                v_ref[:, h0 - 2 * hw:h0 - 2 * hw + GDN_DK] = uh
    for c0 in range(0, hw, CONV_COLS):
        z_ref[:, c0:c0 + CONV_COLS] = _dot(h_main, w_ref[:, 3 * hw + c0:3 * hw + c0 + CONV_COLS])

    sm = _dot(h_main, w_ref[:, 4 * hw:4 * hw + LANE])
    g = -jnp.exp(alog_ref[...]) * _softplus(sm + dtb_ref[...])
    beta = jax.nn.sigmoid(sm)
    ri = _iota((rows, rows), 0)
    ci = _iota((rows, rows), 1)
    same = (ri // SCAN_CHUNK) == (ci // SCAN_CHUNK)
    tri_f = jnp.where(jnp.logical_and(same, ci <= ri), 1.0, 0.0).astype(BF16)
    tri_r = jnp.where(jnp.logical_and(same, ci >= ri), 1.0, 0.0).astype(BF16)
    gcs_f = _exact_left(tri_f, g)
    gcs_r = _exact_left(tri_r, g)
    lane = _iota((rows, LANE), 1)
    gcol = jnp.where(lane < GDN_HEADS, gcs_f,
                     jnp.where(lane < 2 * GDN_HEADS, gcs_r, jnp.where(lane < 4 * GDN_HEADS, beta, 0.0)))
    gcol_ref[...] = gcol
    eye = jnp.where(_iota((LANE, LANE), 0) == _iota((LANE, LANE), 1), 1.0, 0.0).astype(BF16)
    grow_ref[...] = _exact_transpose(gcol, eye)[0:4 * GDN_HEADS, :]


def _in_gdn(x, modtab, norm_w, w, conv_w, alog_vec, dtb_vec, n_lat, n_ctx):
    bsz, l, d = x.shape
    hw = GDN_HEADS * GDN_DK
    tok = pl.BlockSpec((None, ROW_TILE, hw), lambda b, i: (b, i, 0))
    return pl.pallas_call(
        functools.partial(_in_gdn_kernel, n_lat=n_lat, n_ctx=n_ctx),
        grid=(bsz, l // ROW_TILE),
        in_specs=_row_tile_specs(d, l, n_lat) + [pl.BlockSpec((d, w.shape[1]), lambda b, i: (0, 0)),
                                                 pl.BlockSpec((CONV_K, 3 * hw), lambda b, i: (0, 0)),
                                                 pl.BlockSpec((1, LANE), lambda b, i: (0, 0)),
                                                 pl.BlockSpec((1, LANE), lambda b, i: (0, 0))],
        out_specs=[tok, tok, tok, tok,
                   pl.BlockSpec((None, ROW_TILE, LANE), lambda b, i: (b, i, 0)),
                   pl.BlockSpec((None, 4 * GDN_HEADS, ROW_TILE), lambda b, i: (b, 0, i))],
        out_shape=[jax.ShapeDtypeStruct((bsz, l, hw), F32)] * 4
        + [jax.ShapeDtypeStruct((bsz, l, LANE), F32), jax.ShapeDtypeStruct((bsz, 4 * GDN_HEADS, l), F32)],
        scratch_shapes=_conv_scratch(3 * hw),
        compiler_params=_cparams("arbitrary", "arbitrary"),
        name="in_gdn",
    )(x, x, x, modtab, norm_w, w, conv_w, alog_vec, dtb_vec)


def _pair_blockdiag(x):
    c = x.shape[0]
    z = jnp.zeros((c, c), x.dtype)
    return jnp.concatenate([jnp.concatenate([x[:, :c], z], axis=1),
                            jnp.concatenate([z, x[:, c:]], axis=1)], axis=0)


def _gdn_scan_kernel(*refs, reverse, final, n_lat, n_ctx):
    if final:
        q_ref, k_ref, v_ref, gcol_ref, grow_ref, of_ref, z_ref, nw_ref, o_ref, s_ref = refs
    else:
        q_ref, k_ref, v_ref, gcol_ref, grow_ref, o_ref, s_ref = refs
    rows = SCAN_CHUNK
    n_pairs = GDN_HEADS // 2
    pw = 2 * GDN_DK
    chains = [(bb, p) for bb in range(q_ref.shape[0]) for p in range(n_pairs)]
    n_chains = len(chains)
    i = pl.program_id(1)

    @pl.when(i == 0)
    def _():
        s_ref[...] = jnp.zeros_like(s_ref)

    ri = _iota((rows, rows), 0)
    ci = _iota((rows, rows), 1)
    incl = (ri <= ci) if reverse else (ri >= ci)
    ri2 = _iota((rows, pw), 0)
    ci2 = jnp.bitwise_and(_iota((rows, pw), 1), rows - 1)
    strict2 = (ri2 < ci2) if reverse else (ri2 > ci2)
    d = 1 if reverse else 0
    edge = 0 if reverse else rows - 1

    def pair_cols(gcol, lane0):
        return jnp.concatenate([jnp.broadcast_to(gcol[:, lane0 + j:lane0 + j + 1], (rows, GDN_DK))
                                for j in range(2)], axis=1)

    qs, ks, kbs, egs, gcs, decs, rhs, sts = [], [], [], [], [], [], [], []
    for c, (bb, p) in enumerate(chains):
        sl = slice(p * pw, (p + 1) * pw)
        li = d * GDN_HEADS + 2 * p
        gcol = gcol_ref[bb]
        grow = grow_ref[bb]
        q2, k2, v2 = q_ref[bb, :, sl], k_ref[bb, :, sl], v_ref[bb, :, sl]
        gc2 = pair_cols(gcol, li)
        beta2 = pair_cols(gcol, 2 * GDN_HEADS + li)
        eg2 = jnp.exp(gc2)
        kb2 = k2 * beta2
        dec2 = jnp.concatenate(
            [jnp.exp(jnp.where(incl, gcol[:, li + j:li + j + 1] - grow[li + j:li + j + 1, :], NEG_BIG))
             for j in range(2)], axis=1)
        qs.append(q2)
        ks.append(k2)
        kbs.append(kb2)
        egs.append(eg2)
        gcs.append(gc2)
        decs.append(dec2)
        rhs.append([jnp.concatenate([v2[:, j * GDN_DK:(j + 1) * GDN_DK] * beta2[:, j * GDN_DK:(j + 1) * GDN_DK],
                                     kb2[:, j * GDN_DK:(j + 1) * GDN_DK] * eg2[:, j * GDN_DK:(j + 1) * GDN_DK]],
                                    axis=1).astype(BF16) for j in range(2)])
        sts.append(s_ref[c])

    a_mats, attns = [], []
    for p in range(n_chains):
        kq = _dot_nt(jnp.concatenate([kbs[p], qs[p]], axis=0).astype(BF16), _pair_blockdiag(ks[p].astype(BF16)))
        a_mats.append(jnp.where(strict2, kq[:rows] * decs[p], 0.0))
        attns.append((kq[rows:] * decs[p]).astype(BF16))

    half = rows // 2
    rq = _iota((half, pw), 0)
    lq = _iota((half, pw), 1)
    cq = jnp.bitwise_and(lq, half - 1)
    low_half = jnp.bitwise_and(lq, rows - 1) < half
    blk_q = lq // half

    def quad_blockdiag(y):
        return jnp.concatenate([jnp.where(blk_q == b, y, jnp.zeros_like(y)) for b in range(pw // half)], axis=0)

    aqs = [jnp.where(low_half, a_mats[p][:half], a_mats[p][half:]) for p in range(n_chains)]
    tqs = [jnp.where(rq == cq, 1.0, 0.0) - jnp.where((rq // 2) == (cq // 2), aqs[p], 0.0) for p in range(n_chains)]
    s = 2
    while s < half:
        off = jnp.logical_and((rq // (2 * s)) == (cq // (2 * s)), (rq // s) != (cq // s))
        xs = [_dot(jnp.where(off, aqs[p], 0.0).astype(BF16), quad_blockdiag(tqs[p].astype(BF16)))
              for p in range(n_chains)]
        tqs = [tqs[p] - _dot(tqs[p].astype(BF16), quad_blockdiag(xs[p].astype(BF16))) for p in range(n_chains)]
        s *= 2
    ts = [jnp.concatenate([jnp.where(low_half, tqs[p], 0.0), jnp.where(low_half, 0.0, tqs[p])], axis=0)
          for p in range(n_chains)]
    off = (ri2 // half) != (ci2 // half)
    xs = [_dot(jnp.where(off, a_mats[p], 0.0).astype(BF16), _pair_blockdiag(ts[p].astype(BF16)))
          for p in range(n_chains)]
    ts = [ts[p] - _dot(ts[p].astype(BF16), _pair_blockdiag(xs[p].astype(BF16))) for p in range(n_chains)]

    us, ws = [], []
    for p in range(n_chains):
        tb = ts[p].astype(BF16)
        sol = [_dot(tb[:, j * GDN_DK:(j + 1) * GDN_DK], rhs[p][j]) for j in range(2)]
        us.append(jnp.concatenate([sol[0][:, :GDN_DK], sol[1][:, :GDN_DK]], axis=1))
        ws.append(jnp.concatenate([sol[0][:, GDN_DK:], sol[1][:, GDN_DK:]], axis=1))

    vns, outs = [], []
    for p in range(n_chains):
        wq = jnp.concatenate([ws[p], qs[p] * egs[p]], axis=0).astype(BF16)
        ws_qs = _dot(wq, _pair_blockdiag(sts[p].astype(BF16)))
        vns.append((us[p] - ws_qs[:rows]).astype(BF16))
        outs.append(ws_qs[rows:])
    for c, (bb, p) in enumerate(chains):
        o2 = outs[c] + _dot(attns[c], _pair_blockdiag(vns[c]))
        g_end = gcs[c][edge:edge + 1, :]
        k_dec = (ks[c] * jnp.exp(g_end - gcs[c])).astype(BF16)
        full = _dot_tn(k_dec, vns[c])
        upd = jnp.concatenate([full[:GDN_DK, :GDN_DK], full[GDN_DK:, GDN_DK:]], axis=1)
        s_ref[c] = sts[c] * jnp.exp(g_end) + upd
        sl = slice(p * pw, (p + 1) * pw)
        if final:
            o2 = o2 + of_ref[bb, :, sl]
            ys = []
            for j in range(2):
                oj = o2[:, j * GDN_DK:(j + 1) * GDN_DK]
                ys.append(oj * lax.rsqrt(jnp.mean(oj * oj, axis=-1, keepdims=True) + NORM_EPS) * nw_ref[...])
            o_ref[bb, :, sl] = (jnp.concatenate(ys, axis=1) * _silu(z_ref[bb, :, sl])).astype(o_ref.dtype)
        else:
            o_ref[bb, :, sl] = o2


def _gdn_scan(q, k, v, gcol, grow, o_fwd, z, norm_w, n_lat, n_ctx, reverse):
    bsz, l, hw = q.shape
    final = o_fwd is not None
    chunk_fn = functools.partial(_chunk_of_step, n_lat=n_lat, n_ctx=n_ctx, reverse=reverse)
    nb = next(n for n in (4, 2, 1) if bsz % n == 0)
    tok = pl.BlockSpec((nb, SCAN_CHUNK, hw), lambda b, i: (b, chunk_fn(i), 0))
    in_specs = [tok, tok, tok,
                pl.BlockSpec((nb, SCAN_CHUNK, LANE), lambda b, i: (b, chunk_fn(i), 0)),
                pl.BlockSpec((nb, 4 * GDN_HEADS, SCAN_CHUNK), lambda b, i: (b, 0, chunk_fn(i)))]
    args = [q, k, v, gcol, grow]
    if final:
        in_specs += [tok, tok, pl.BlockSpec((1, GDN_DK), lambda b, i: (0, 0))]
        args += [o_fwd, z, norm_w]
    return pl.pallas_call(
        functools.partial(_gdn_scan_kernel, reverse=reverse, final=final, n_lat=n_lat, n_ctx=n_ctx),
        grid=(bsz // nb, l // SCAN_CHUNK),
        in_specs=in_specs,
        out_specs=tok,
        out_shape=jax.ShapeDtypeStruct((bsz, l, hw), BF16 if final else F32),
        scratch_shapes=[pltpu.VMEM((nb * GDN_HEADS // 2, GDN_DK, 2 * GDN_DK), F32)],
        compiler_params=_cparams("arbitrary", "arbitrary"),
        name="gdn_rev" if reverse else "gdn_fwd",
    )(*args)


def _in_ssd_kernel(xl_ref, xlp_ref, xln_ref, xc_ref, xcp_ref, xcn_ref, mod_ref, nw_ref, w_ref, cw_ref, cb_ref,
                   z_ref, u_ref, dt_ref, *xe_refs, n_lat, n_ctx):
    j = pl.program_id(1)
    first, last = _segment_edges(j, n_lat, n_ctx)
    is_ctx = j >= n_lat
    pick = lambda c_ref, l_ref: jnp.where(is_ctx, c_ref[...], l_ref[...])
    h_main, h_all = _normed_rows(pick(xc_ref, xl_ref), pick(xcp_ref, xlp_ref), pick(xcn_ref, xln_ref),
                                 mod_ref, nw_ref)
    inner = z_ref.shape[1]
    xbc_w = u_ref.shape[1]
    for c0 in range(0, inner, CONV_COLS):
        z_ref[:, c0:c0 + CONV_COLS] = _dot(h_main, w_ref[:, c0:c0 + CONV_COLS])
    for c0, c1, u in _proj_conv(xe_refs, h_all, w_ref, cw_ref, inner, xbc_w, first, last, ROW_TILE):
        u_ref[:, c0:c1] = _silu(u + cb_ref[:, c0:c1])
    dt_ref[...] = _dot(h_main, w_ref[:, inner + xbc_w:inner + xbc_w + LANE])


def _in_ssd(x_lat, x, modtab, norm_w, w, conv_w, conv_b, n_lat, n_ctx):
    bsz, l, d = x.shape
    inner = SSD_HEADS * SSD_HEADDIM
    xbc_w = inner + 2 * SSD_GROUPS * SSD_STATE
    tok = lambda n: pl.BlockSpec((None, ROW_TILE, n), lambda b, j: (b, j, 0))
    lat_tile = lambda j: jnp.minimum(j, n_lat - 1)
    ctx_tile = lambda j: jnp.maximum(j, n_lat)
    lat_specs = [pl.BlockSpec((None, ROW_TILE, d), lambda b, j: (b, lat_tile(j), 0)),
                 *_halo_specs(d, ROW_TILE, x_lat.shape[1], lat_tile)]
    ctx_specs = [pl.BlockSpec((None, ROW_TILE, d), lambda b, j: (b, ctx_tile(j), 0)),
                 *_halo_specs(d, ROW_TILE, l, ctx_tile)]
    return pl.pallas_call(
        functools.partial(_in_ssd_kernel, n_lat=n_lat, n_ctx=n_ctx),
        grid=(bsz, l // ROW_TILE),
        in_specs=lat_specs + ctx_specs + _row_tile_specs(d, l, n_lat)[3:] + [
            pl.BlockSpec((d, w.shape[1]), lambda b, j: (0, 0)),
            pl.BlockSpec((CONV_K, xbc_w), lambda b, j: (0, 0)),
            pl.BlockSpec((1, xbc_w), lambda b, j: (0, 0))],
        out_specs=[tok(inner), tok(xbc_w), tok(LANE)],
        out_shape=[jax.ShapeDtypeStruct((bsz, l, inner), F32), jax.ShapeDtypeStruct((bsz, l, xbc_w), F32),
                   jax.ShapeDtypeStruct((bsz, l, LANE), F32)],
        scratch_shapes=_conv_scratch(xbc_w),
        compiler_params=_cparams("arbitrary", "arbitrary"),
        name="in_ssd",
    )(x_lat, x_lat, x_lat, x, x, x, modtab, norm_w, w, conv_w, conv_b)


def _ssd_scan_kernel(*refs, reverse, final):
    if final:
        z_ref, u_ref, dt_ref, dtb_ref, a_ref, dsk_ref, yf_ref, nw_ref, o_ref, st_ref = refs
    else:
        u_ref, dt_ref, dtb_ref, a_ref, dsk_ref, o_ref, st_ref = refs
    rows = SCAN_CHUNK
    inner = SSD_HEADS * SSD_HEADDIM
    gw = inner // SSD_GROUPS
    i = pl.program_id(1)

    @pl.when(i == 0)
    def _():
        st_ref[...] = jnp.zeros_like(st_ref)

    ri = _iota((rows, rows), 0)
    ci = _iota((rows, rows), 1)
    incl = (ri <= ci) if reverse else (ri >= ci)
    tri = jnp.where(incl, 1.0, 0.0).astype(BF16)
    eye = jnp.where(ri == ci, 1.0, 0.0).astype(BF16)
    d = 1 if reverse else 0
    edge = 0 if reverse else rows - 1
    er = _iota((LANE, inner), 0)
    ec = _iota((LANE, inner), 1)
    expand = jnp.where(er == d * SSD_HEADS + ec // SSD_HEADDIM, 1.0, 0.0).astype(BF16)
    lane = _iota((rows, LANE), 1)
    heads_per_group = SSD_HEADS // SSD_GROUPS
    samples = range(u_ref.shape[0])
    units = [(bb, g) for bb in samples for g in range(SSD_GROUPS)]

    us, xss, acss, acs_ts, ea_es, xdt_bs, xw_bs = [], [], [], [], [], [], []
    for bb in samples:
        u = u_ref[bb]
        xs = u[:, :inner]
        dt = _softplus(dt_ref[bb] + dtb_ref[...])
        acs = _exact_left(tri, a_ref[...] * dt)
        xdt = xs * _exact_right(dt, expand)
        us.append(u)
        xss.append(xs)
        acss.append(acs)
        acs_ts.append(_exact_transpose(acs, eye))
        ea_es.append(_exact_right(jnp.exp(acs), expand))
        xdt_bs.append(xdt.astype(BF16))
        xw_bs.append((xdt * _exact_right(jnp.exp(acs[edge:edge + 1, :] - acs), expand)).astype(BF16))

    bms, cbs, sts, y_offs = [], [], [], []
    for c, (bb, g) in enumerate(units):
        u = us[bb]
        bm = u[:, inner + g * SSD_STATE:inner + (g + 1) * SSD_STATE].astype(BF16)
        cm = u[:, inner + (SSD_GROUPS + g) * SSD_STATE:inner + (SSD_GROUPS + g + 1) * SSD_STATE].astype(BF16)
        st = st_ref[c]
        bms.append(bm)
        cbs.append(_dot_nt(cm, bm))
        sts.append(st)
        y_offs.append(_dot(cm, st.astype(BF16)) * ea_es[bb][:, g * gw:(g + 1) * gw])

    diags = [[] for _ in units]
    for hp in range(heads_per_group // 2):
        for c, (bb, g) in enumerate(units):
            h0 = g * heads_per_group + 2 * hp
            mats = []
            for hh in (h0, h0 + 1):
                li = d * SSD_HEADS + hh
                lm = jnp.exp(jnp.where(incl, acss[bb][:, li:li + 1] - acs_ts[bb][li:li + 1, :], NEG_BIG))
                mats.append((cbs[c] * lm).astype(BF16))
            xp = xdt_bs[bb][:, h0 * SSD_HEADDIM:(h0 + 2) * SSD_HEADDIM]
            x_lo = jnp.where(lane < SSD_HEADDIM, xp, jnp.zeros_like(xp))
            x_hi = jnp.where(lane >= SSD_HEADDIM, xp, jnp.zeros_like(xp))
            diags[c].append(_dot(jnp.concatenate(mats, axis=1), jnp.concatenate([x_lo, x_hi], axis=0)))

    for c, (bb, g) in enumerate(units):
        st_ref[c] = (sts[c] * ea_es[bb][edge:edge + 1, g * gw:(g + 1) * gw]
                     + _dot_tn(bms[c], xw_bs[bb][:, g * gw:(g + 1) * gw]))
    for bb in samples:
        y_parts = [jnp.concatenate(diags[bb * SSD_GROUPS + g], axis=1) + y_offs[bb * SSD_GROUPS + g]
                   for g in range(SSD_GROUPS)]
        y = jnp.concatenate(y_parts, axis=1) + dsk_ref[...] * xss[bb]
        if final:
            y = y + yf_ref[bb]
            yz = y * _silu(z_ref[bb])
            outs = []
            for g in range(SSD_GROUPS):
                yg = yz[:, g * gw:(g + 1) * gw]
                outs.append(yg * lax.rsqrt(jnp.mean(yg * yg, axis=-1, keepdims=True) + NORM_EPS))
            o_ref[bb] = (jnp.concatenate(outs, axis=1) * nw_ref[...]).astype(o_ref.dtype)
        else:
            o_ref[bb] = y


def _ssd_scan(z, u, dt_raw, dtb_vec, a_vec, dskip, y_fwd, norm_w, n_lat, n_ctx, reverse):
    bsz, l, xbc_w = u.shape
    inner = SSD_HEADS * SSD_HEADDIM
    final = y_fwd is not None
    chunk_fn = functools.partial(_chunk_of_step, n_lat=n_lat, n_ctx=n_ctx, reverse=reverse)
    nb = 2 if bsz % 2 == 0 else 1
    tok = lambda n: pl.BlockSpec((nb, SCAN_CHUNK, n), lambda b, i: (b, chunk_fn(i), 0))
    row = lambda n: pl.BlockSpec((1, n), lambda b, i: (0, 0))
    in_specs = [tok(xbc_w), tok(LANE), row(LANE), row(LANE), row(inner)]
    args = [u, dt_raw, dtb_vec, a_vec, dskip]
    if final:
        in_specs = [tok(inner)] + in_specs + [tok(inner), row(inner)]
        args = [z] + args + [y_fwd, norm_w]
    return pl.pallas_call(
        functools.partial(_ssd_scan_kernel, reverse=reverse, final=final),
        grid=(bsz // nb, l // SCAN_CHUNK),
        in_specs=in_specs,
        out_specs=tok(inner),
        out_shape=jax.ShapeDtypeStruct((bsz, l, inner), BF16 if final else F32),
        scratch_shapes=[pltpu.VMEM((nb * SSD_GROUPS, SSD_STATE, inner // SSD_GROUPS), F32)],
        compiler_params=_cparams("arbitrary", "arbitrary"),
        name="ssd_rev" if reverse else "ssd_fwd",
    )(*args)


def _merge_kernel(x_ref, lru_ref, gdn_ref, ssdl_ref, ssdc_ref, gate_ref, mod_ref, nw_ref, wb_ref, wo_ref, wr_ref,
                  br_ref, xo_ref, h_ref, lg_ref, *, n_lat):
    d = x_ref.shape[1]
    ssd = jnp.where(pl.program_id(1) >= n_lat, ssdc_ref[...], ssdl_ref[...])
    acc = None
    for k, b in enumerate((lru_ref[...], gdn_ref[...], ssd)):
        t = jax.nn.sigmoid(gate_ref[:, k * d:(k + 1) * d].astype(F32)) * _dot(b.astype(BF16), wb_ref[k])
        acc = t if acc is None else acc + t
    mix = _dot(acc.astype(BF16), wo_ref[...])
    x = x_ref[...] + mod_ref[2:3, :] * mix
    xo_ref[...] = x
    h = _mod_norm(x, nw_ref[...], mod_ref[3:4, :], mod_ref[4:5, :])
    hb = h.astype(BF16)
    h_ref[...] = hb
    lg_ref[...] = _dot(hb, wr_ref[...]) + br_ref[...]


def _merge(x, lru, gdn, ssd_lat, ssd_scan, gate, modtab, norm_w, wb, wo, wr, br, n_lat_tiles):
    bsz, l, d = x.shape
    tok = pl.BlockSpec((None, ROW_TILE, d), lambda b, j: (b, j, 0))
    return pl.pallas_call(
        functools.partial(_merge_kernel, n_lat=n_lat_tiles),
        grid=(bsz, l // ROW_TILE),
        in_specs=[tok, tok, tok,
                  pl.BlockSpec((None, ROW_TILE, d), lambda b, j: (b, jnp.minimum(j, n_lat_tiles - 1), 0)),
                  pl.BlockSpec((None, ROW_TILE, d), lambda b, j: (b, jnp.maximum(j, n_lat_tiles), 0)),
                  pl.BlockSpec((None, ROW_TILE, 3 * d), lambda b, j: (b, j, 0)),
                  pl.BlockSpec((None, None, SUBLANE, d), lambda b, j: (b, jnp.where(j >= n_lat_tiles, 1, 0), 0, 0)),
                  pl.BlockSpec((1, d), lambda b, j: (0, 0)),
                  pl.BlockSpec((3, d, d), lambda b, j: (0, 0, 0)),
                  pl.BlockSpec((d, d), lambda b, j: (0, 0)),
                  pl.BlockSpec((d, LANE), lambda b, j: (0, 0)),
                  pl.BlockSpec((1, LANE), lambda b, j: (0, 0))],
        out_specs=[tok, tok, pl.BlockSpec((None, ROW_TILE, LANE), lambda b, j: (b, j, 0))],
        out_shape=[jax.ShapeDtypeStruct((bsz, l, d), F32), jax.ShapeDtypeStruct((bsz, l, d), BF16),
                   jax.ShapeDtypeStruct((bsz, l, LANE), F32)],
        compiler_params=_cparams("arbitrary", "arbitrary"),
        name="merge",
    )(x, lru, gdn, ssd_lat, ssd_scan, gate, modtab, norm_w, wb, wo, wr, br)


def _moe_kernel(blk_ref, exp_ref, lo_ref, hi_ref, x_ref, w1_ref, w3_ref, w2_ref, o_ref,
                acc_ref, w1b_ref, w3b_ref, w2b_ref):
    i = pl.program_id(0)
    lo = lo_ref[i]
    hi = hi_ref[i]
    base = blk_ref[i] * MOE_ROWS

    @pl.when(jnp.logical_or(i == 0, exp_ref[i] != exp_ref[jnp.maximum(i - 1, 0)]))
    def _():
        w1b_ref[...] = w1_ref[...].astype(BF16)
        w3b_ref[...] = w3_ref[...].astype(BF16)
        w2b_ref[...] = w2_ref[...].astype(BF16)

    @pl.when(lo == base)
    def _():
        acc_ref[...] = jnp.zeros_like(acc_ref)

    @pl.when(hi > lo)
    def _():
        row = base + _iota((MOE_ROWS, 1), 0)
        x = jnp.where(jnp.logical_and(row >= lo, row < hi), x_ref[...], jnp.zeros_like(x_ref))
        mid = _silu(_dot(x, w1b_ref[...])) * _dot(x, w3b_ref[...])
        acc_ref[...] += _dot(mid.astype(BF16), w2b_ref[...])

    o_ref[...] = acc_ref[...].astype(o_ref.dtype)


def _moe_experts(xb, item_block, item_expert, item_lo, item_hi, w1, w3, w2, layer):
    n_rows, d = xb.shape
    ff = w1.shape[-1]
    rows_spec = pl.BlockSpec((MOE_ROWS, d), lambda i, blk, ex, lo, hi: (blk[i], 0))
    grid_spec = pltpu.PrefetchScalarGridSpec(
        num_scalar_prefetch=4,
        grid=(item_block.shape[0],),
        in_specs=[rows_spec,
                  pl.BlockSpec((None, None, d, ff), lambda i, blk, ex, lo, hi: (layer, ex[i], 0, 0)),
                  pl.BlockSpec((None, None, d, ff), lambda i, blk, ex, lo, hi: (layer, ex[i], 0, 0)),
                  pl.BlockSpec((None, None, ff, d), lambda i, blk, ex, lo, hi: (layer, ex[i], 0, 0))],
        out_specs=rows_spec,
        scratch_shapes=[pltpu.VMEM((MOE_ROWS, d), F32), pltpu.VMEM((d, ff), BF16), pltpu.VMEM((d, ff), BF16),
                        pltpu.VMEM((ff, d), BF16)],
    )
    return pl.pallas_call(
        _moe_kernel,
        grid_spec=grid_spec,
        out_shape=jax.ShapeDtypeStruct((n_rows, d), BF16),
        compiler_params=_cparams("arbitrary"),
        name="moe_experts",
    )(item_block, item_expert, item_lo, item_hi, xb, w1, w3, w2)


def _route(logits):
    t = logits.shape[0]
    g_logits = logits[:, :N_GROUPS]
    e_logits = logits[:, N_GROUPS:N_GROUPS + N_EXPERTS].reshape(t, N_GROUPS, EXPERTS_PER_GROUP)
    g_prob = jax.nn.softmax(g_logits, axis=-1)
    g_idx = jnp.argmax(g_logits, axis=-1)
    p_group = jnp.take_along_axis(g_prob, g_idx[:, None], axis=1)[:, 0]
    e_in_group = jnp.take_along_axis(e_logits, g_idx[:, None, None], axis=1)[:, 0]
    top_v, top_i = lax.top_k(e_in_group, TOP_K)
    weights = jax.nn.softmax(top_v, axis=-1) * p_group[:, None]
    expert_id = (g_idx[:, None] * EXPERTS_PER_GROUP + top_i).reshape(-1).astype(jnp.int32)
    n_assign = t * TOP_K
    assert n_assign % MOE_ROWS == 0
    n_blocks = n_assign // MOE_ROWS
    ar = jnp.arange(n_assign, dtype=jnp.int32)
    e_sorted, order = lax.sort((expert_id, ar), num_keys=1)
    _, slot = lax.sort((order, ar), num_keys=1)
    ends = jnp.sum(e_sorted[None, :] <= jnp.arange(N_EXPERTS, dtype=jnp.int32)[:, None], axis=1).astype(jnp.int32)
    starts = jnp.concatenate([jnp.zeros((1,), jnp.int32), ends[:-1]])
    cuts = jnp.sort(jnp.concatenate([jnp.arange(n_blocks, dtype=jnp.int32) * MOE_ROWS, starts]))
    item_lo = cuts
    item_hi = jnp.concatenate([cuts[1:], jnp.full((1,), n_assign, jnp.int32)])
    item_block = jnp.minimum(item_lo // MOE_ROWS, n_blocks - 1).astype(jnp.int32)
    item_expert = jnp.minimum(jnp.sum(item_lo[:, None] >= ends[None, :], axis=1), N_EXPERTS - 1).astype(jnp.int32)
    return order // TOP_K, (item_block, item_expert, item_lo, item_hi), slot.reshape(t, TOP_K), weights


def _combine_kernel(x_ref, ya_ref, yb_ref, wt_ref, mod_ref, nw_ref, o_ref, *, final):
    wt = wt_ref[...]
    y = wt[:, 0:1] * ya_ref[...].astype(F32) + wt[:, 1:2] * yb_ref[...].astype(F32)
    x = x_ref[...] + mod_ref[5:6, :] * y
    if final:
        x = x * lax.rsqrt(jnp.mean(x * x, axis=-1, keepdims=True) + NORM_EPS) * nw_ref[...]
    o_ref[...] = x


def _combine(x, ya, yb, wts, modtab, norm_w, n_lat_tiles, final, out_len):
    bsz, l, d = x.shape
    tok = pl.BlockSpec((None, ROW_TILE, d), lambda b, j: (b, j, 0))
    return pl.pallas_call(
        functools.partial(_combine_kernel, final=final),
        grid=(bsz, out_len // ROW_TILE),
        in_specs=[tok, tok, tok,
                  pl.BlockSpec((None, ROW_TILE, LANE), lambda b, j: (b, j, 0)),
                  pl.BlockSpec((None, None, SUBLANE, d), lambda b, j: (b, jnp.where(j >= n_lat_tiles, 1, 0), 0, 0)),
                  pl.BlockSpec((1, d), lambda b, j: (0, 0))],
        out_specs=tok,
        out_shape=jax.ShapeDtypeStruct((bsz, out_len, d), F32),
        compiler_params=_cparams("arbitrary", "arbitrary"),
        name="combine",
    )(x, ya, yb, wts, modtab, norm_w)


def _pad_lanes(v, n=LANE):
    v = v.reshape(1, -1).astype(F32)
    return jnp.pad(v, ((0, 0), (0, n - v.shape[1])))


def kernel(x, c, ctx, c_ctx, w_mod, b_mod, norm1_w, norm2_w, w_in, lru_conv_w, lru_conv_b, lru_wa, lru_ba,
           lru_wi, lru_bi, lru_lambda, gdn_conv_w, gdn_a_log, gdn_dt_bias, gdn_norm_w, ssd_conv_w, ssd_conv_b,
           ssd_a_log, ssd_dt_bias, ssd_d, ssd_norm_w, w_branch, w_out, router_group_w, router_group_b,
           router_expert_w, router_expert_b, expert_w1, expert_w3, expert_w2, final_norm_w):
    bsz, seq, d = x.shape
    n_ctx_tok = ctx.shape[1]
    depth = w_mod.shape[0]
    l = seq + n_ctx_tok
    rows = seq // GRID_W
    assert seq % LRU_CHUNK == 0 and n_ctx_tok % LRU_CHUNK == 0 and ROW_TILE == LRU_CHUNK
    assert bsz + 1 <= SUBLANE
    assert l % GRID_W == 0 and l % rows == 0
    hw = GDN_HEADS * GDN_DK
    inner = SSD_HEADS * SSD_HEADDIM
    xbc_w = inner + 2 * SSD_GROUPS * SSD_STATE

    act = jnp.concatenate([_silu(c), _silu(c_ctx)[None, :],
                           jnp.zeros((SUBLANE - bsz - 1, d), F32)], axis=0)
    mod_all = _modulation(act, w_mod, b_mod).reshape(depth, SUBLANE, 6, d)
    pad2 = jnp.zeros((bsz, 2, d), F32)

    xs = jnp.concatenate([x, ctx], axis=1)
    nl_t, nc_t = seq // ROW_TILE, n_ctx_tok // ROW_TILE
    nl_s, nc_s = seq // SCAN_CHUNK, n_ctx_tok // SCAN_CHUNK

    o_lx, o_qkv, o_gz, o_gb, o_ga = 0, 2 * d, 2 * d + 3 * hw, 2 * d + 4 * hw, 2 * d + 4 * hw + 2 * GDN_HEADS
    o_sz = o_ga + 2 * GDN_HEADS
    o_xbc = o_sz + inner
    o_sdt = o_xbc + xbc_w
    o_gate = o_sdt + 2 * SSD_HEADS

    for i in range(depth):
        lat = mod_all[i, :bsz]
        cx = jnp.broadcast_to(mod_all[i, bsz][None], (bsz, 6, d))
        modtab = jnp.stack([jnp.concatenate([lat, pad2], axis=1), jnp.concatenate([cx, pad2], axis=1)], axis=1)

        wi_ = w_in[i]
        w_lru = wi_[:, o_lx:o_qkv].astype(BF16)
        zpad = jnp.zeros((d, LANE - 4 * GDN_HEADS), F32)
        w_gdn = jnp.concatenate([wi_[:, o_qkv:o_gb], wi_[:, o_ga:o_sz], wi_[:, o_gb:o_ga], zpad], axis=1).astype(BF16)
        zpad2 = jnp.zeros((d, LANE - 2 * SSD_HEADS), F32)
        w_ssd = jnp.concatenate([wi_[:, o_sz:o_gate], zpad2], axis=1).astype(BF16)
        w_gate = wi_[:, o_gate:].astype(BF16)
        n1 = norm1_w[i].reshape(1, d)

        lru_u, lru_y, gate = _in_lru(xs, modtab, n1, w_lru, w_gate, lru_conv_w[i], lru_conv_b[i].reshape(1, d),
                                     nl_t, nc_t)
        h_f = _lru_scan(lru_u, lru_y, lru_wa[i, 0].astype(BF16), lru_ba[i, 0].reshape(1, d),
                        lru_wi[i, 0].astype(BF16), lru_bi[i, 0].reshape(1, d), lru_lambda[i, 0].reshape(1, d),
                        None, nl_t, nc_t, False)
        lru_out = _lru_scan(lru_u, lru_y, lru_wa[i, 1].astype(BF16), lru_ba[i, 1].reshape(1, d),
                            lru_wi[i, 1].astype(BF16), lru_bi[i, 1].reshape(1, d), lru_lambda[i, 1].reshape(1, d),
                            h_f, nl_t, nc_t, True)

        q, k, v, gz, gcol, grow = _in_gdn(xs, modtab, n1, w_gdn, gdn_conv_w[i], _pad_lanes(gdn_a_log[i]),
                                          _pad_lanes(gdn_dt_bias[i]), nl_t, nc_t)
        o_f = _gdn_scan(q, k, v, gcol, grow, None, None, None, nl_s, nc_s, False)
        gdn_out = _gdn_scan(q, k, v, gcol, grow, o_f, gz, gdn_norm_w[i].reshape(1, GDN_DK), nl_s, nc_s, True)

        x_cm = xs.reshape(bsz, l // GRID_W, GRID_W, d)[:, :rows].swapaxes(1, 2).reshape(bsz, seq, d)
        sz, su, sdt = _in_ssd(x_cm, xs, modtab, n1, w_ssd, ssd_conv_w[i], ssd_conv_b[i].reshape(1, xbc_w),
                              nl_t, nc_t)
        dtb_vec = _pad_lanes(ssd_dt_bias[i])
        a_vec = _pad_lanes(-jnp.exp(ssd_a_log[i].astype(F32)))
        dsk = [jnp.repeat(ssd_d[i, dd], SSD_HEADDIM).reshape(1, inner) for dd in range(2)]
        y_f = _ssd_scan(None, su, sdt, dtb_vec, a_vec, dsk[0], None, None, nl_s, nc_s, False)
        ssd_scan_out = _ssd_scan(sz, su, sdt, dtb_vec, a_vec, dsk[1], y_f,
                                 ssd_norm_w[i].reshape(1, inner), nl_s, nc_s, True)
        ssd_lat = (ssd_scan_out.reshape(bsz, l // rows, rows, inner)[:, :GRID_W].swapaxes(1, 2)
                   .reshape(bsz, seq, inner))

        w_r = jnp.concatenate([router_group_w[i], router_expert_w[i],
                               jnp.zeros((d, LANE - N_GROUPS - N_EXPERTS), F32)], axis=1).astype(BF16)
        b_r = _pad_lanes(jnp.concatenate([router_group_b[i], router_expert_b[i]]))
        x_mid, h2, logits = _merge(xs, lru_out, gdn_out, ssd_lat, ssd_scan_out, gate, modtab, norm2_w[i].reshape(1, d),
                                   w_branch[i].astype(BF16), w_out[i].astype(BF16), w_r, b_r, nl_t)

        t = bsz * l
        tok_sorted, items, slot, weights = _route(logits.reshape(t, LANE))
        xb = h2.reshape(t, d)[tok_sorted]
        yb = _moe_experts(xb, *items, expert_w1, expert_w3, expert_w2, i)
        y0 = yb[slot[:, 0]].reshape(bsz, l, d)
        y1 = yb[slot[:, 1]].reshape(bsz, l, d)
        wts = jnp.pad(weights, ((0, 0), (0, LANE - TOP_K))).reshape(bsz, l, LANE)
        last = i == depth - 1
        xs = _combine(x_mid, y0, y1, wts, modtab, final_norm_w.reshape(1, d), nl_t, last, seq if last else l)

    return xs
```

```python
import functools

import jax
import jax.numpy as jnp
from jax import lax
from jax.experimental import pallas as pl
from jax.experimental.pallas import tpu as pltpu

GRID_W = 64
CONV_K = 4
NORM_EPS = 1e-6
LRU_BLOCKS = 8
LRU_C = 8.0
GDN_HEADS = 8
GDN_DK = 128
SSD_HEADS = 16
SSD_HEADDIM = 64
SSD_GROUPS = 2
SSD_STATE = 128
N_GROUPS = 4
EXPERTS_PER_GROUP = 8
N_EXPERTS = N_GROUPS * EXPERTS_PER_GROUP
TOP_K = 2

LANE = 128
SUBLANE = 8
ROW_TILE = 256
LRU_CHUNK = 256
SCAN_CHUNK = 128
MOE_ROWS = 512
CONV_COLS = 512
MOD_COLS = 1536
VMEM_LIMIT = 56 * 1024 * 1024

F32 = jnp.float32
BF16 = jnp.bfloat16
NEG_BIG = -1e30


def _cparams(*sem):
    return pltpu.CompilerParams(dimension_semantics=sem, vmem_limit_bytes=VMEM_LIMIT)


def _dot(a, b):
    return jnp.dot(a, b, preferred_element_type=F32)


def _dot_nt(a, b):
    return lax.dot_general(a, b, (((1,), (1,)), ((), ())), preferred_element_type=F32)


def _dot_tn(a, b):
    return lax.dot_general(a, b, (((0,), (0,)), ((), ())), preferred_element_type=F32)


def _split3(x):
    hi = x.astype(BF16)
    r1 = x - hi.astype(F32)
    mid = r1.astype(BF16)
    lo = (r1 - mid.astype(F32)).astype(BF16)
    return hi, mid, lo


def _exact_left(m_bf, x):
    hi, mid, lo = _split3(x)
    return _dot(jnp.concatenate([m_bf, m_bf, m_bf], axis=1), jnp.concatenate([hi, mid, lo], axis=0))


def _exact_right(x, m_bf):
    hi, mid, lo = _split3(x)
    return _dot(jnp.concatenate([hi, mid, lo], axis=1), jnp.concatenate([m_bf, m_bf, m_bf], axis=0))


def _exact_transpose(x, eye_bf):
    hi, mid, lo = _split3(x)
    return _dot_nt(jnp.concatenate([eye_bf, eye_bf, eye_bf], axis=1), jnp.concatenate([hi, mid, lo], axis=1))


def _softplus(x):
    return jnp.maximum(x, 0.0) + jnp.log(1.0 + jnp.exp(-jnp.abs(x)))


def _sigmoid(x):
    return 0.5 + 0.5 * jnp.tanh(0.5 * x)


def _silu(x):
    hx = 0.5 * x
    return hx + hx * jnp.tanh(hx)


def _iota(shape, dim):
    return lax.broadcasted_iota(jnp.int32, shape, dim)


def _fwd_chunk(i, n_lat, n_ctx):
    return jnp.where(i < n_ctx, n_lat + i, i - n_ctx)


def _chunk_of_step(i, n_lat, n_ctx, reverse):
    return (n_lat + n_ctx - 1 - i) if reverse else _fwd_chunk(i, n_lat, n_ctx)


def _segment_edges(c, n_lat, n_ctx):
    first = jnp.logical_or(c == 0, c == n_lat)
    last = jnp.logical_or(c == n_lat - 1, c == n_lat + n_ctx - 1)
    return first, last


def _normed_rows(x, x_prev, x_next, mod_ref, nw_ref):
    shift, scale = mod_ref[0:1, :], mod_ref[1:2, :]
    h_main = _mod_norm(x, nw_ref[...], shift, scale)
    h_prev = _mod_norm(x_prev, nw_ref[...], shift, scale)
    h_next = _mod_norm(x_next, nw_ref[...], shift, scale)
    return h_main.astype(BF16), jnp.concatenate([h_prev, h_main, h_next], axis=0).astype(BF16)


def _proj_conv(xe_refs, h_all, w_ref, cw_ref, col0, width, first, last, rows):
    base = SUBLANE - CONV_K // 2
    for xe_ref, c0 in zip(xe_refs, range(0, width, CONV_COLS), strict=True):
        c1 = c0 + CONV_COLS
        y = _dot(h_all, w_ref[:, col0 + c0:col0 + c1])
        xe_ref[0:SUBLANE, :] = jnp.where(first, 0.0, y[0:SUBLANE])
        xe_ref[SUBLANE:SUBLANE + rows, :] = y[SUBLANE:SUBLANE + rows]
        xe_ref[SUBLANE + rows:2 * SUBLANE + rows, :] = jnp.where(last, 0.0, y[SUBLANE + rows:])
        u = cw_ref[0:1, c0:c1] * xe_ref[base:base + rows, :]
        for j in range(1, CONV_K):
            u = u + cw_ref[j:j + 1, c0:c1] * xe_ref[base + j:base + j + rows, :]
        yield c0, c1, u


def _conv_scratch(width):
    assert width % CONV_COLS == 0
    return [pltpu.VMEM((ROW_TILE + 2 * SUBLANE, CONV_COLS), F32)] * (width // CONV_COLS)


def _halo_specs(width, rows, n_rows_total, chunk_fn, col_block=0):
    per = rows // SUBLANE
    last_tile = n_rows_total // SUBLANE - 1
    prev = pl.BlockSpec((None, SUBLANE, width),
                        lambda b, i: (b, jnp.maximum(chunk_fn(i) * per - 1, 0), col_block))
    nxt = pl.BlockSpec((None, SUBLANE, width),
                       lambda b, i: (b, jnp.minimum((chunk_fn(i) + 1) * per, last_tile), col_block))
    return prev, nxt


def _mod_kernel(a_ref, w_ref, b_ref, o_ref):
    o_ref[...] = _dot(a_ref[...].astype(BF16), w_ref[...].astype(BF16)) + b_ref[...]


def _modulation(act, w_mod, b_mod):
    depth, d, n = w_mod.shape
    tn = MOD_COLS
    assert n % tn == 0
    return pl.pallas_call(
        _mod_kernel,
        grid=(depth, n // tn),
        in_specs=[pl.BlockSpec((SUBLANE, d), lambda l, j: (0, 0)),
                  pl.BlockSpec((None, d, tn), lambda l, j: (l, 0, j)),
                  pl.BlockSpec((None, 1, tn), lambda l, j: (l, 0, j))],
        out_specs=pl.BlockSpec((None, SUBLANE, tn), lambda l, j: (l, 0, j)),
        out_shape=jax.ShapeDtypeStruct((depth, SUBLANE, n), F32),
        compiler_params=_cparams("arbitrary", "arbitrary"),
        name="modulation",
    )(act, w_mod, b_mod.reshape(depth, 1, n))


def _mod_norm(x, nw, shift, scale):
    y = x * lax.rsqrt(jnp.mean(x * x, axis=-1, keepdims=True) + NORM_EPS)
    return (y * nw) * (1.0 + scale) + shift


def _row_tile_specs(d, l, n_lat_tiles):
    ident = lambda j: j
    main = pl.BlockSpec((None, ROW_TILE, d), lambda b, j: (b, j, 0))
    prev, nxt = _halo_specs(d, ROW_TILE, l, ident)
    mod = pl.BlockSpec((None, None, SUBLANE, d), lambda b, j: (b, jnp.where(j >= n_lat_tiles, 1, 0), 0, 0))
    return [main, prev, nxt, mod, pl.BlockSpec((1, d), lambda b, j: (0, 0))]


def _in_lru_kernel(x_ref, xp_ref, xn_ref, mod_ref, nw_ref, w_ref, wg_ref, cw_ref, cb_ref, u_ref, y_ref, g_ref,
                   *xe_refs, n_lat, n_ctx):
    first, last = _segment_edges(pl.program_id(1), n_lat, n_ctx)
    h_main, h_all = _normed_rows(x_ref[...], xp_ref[...], xn_ref[...], mod_ref, nw_ref)
    w = u_ref.shape[1]
    for c0, c1, u in _proj_conv(xe_refs, h_all, w_ref, cw_ref, 0, w, first, last, ROW_TILE):
        u_ref[:, c0:c1] = u + cb_ref[:, c0:c1]
    for c0 in range(0, w, CONV_COLS):
        y_ref[:, c0:c0 + CONV_COLS] = _dot(h_main, w_ref[:, w + c0:w + c0 + CONV_COLS])
    for c0 in range(0, g_ref.shape[1], CONV_COLS):
        g_ref[:, c0:c0 + CONV_COLS] = _dot(h_main, wg_ref[:, c0:c0 + CONV_COLS]).astype(g_ref.dtype)


def _in_lru(x, modtab, norm_w, w, w_gate, conv_w, conv_b, n_lat, n_ctx):
    bsz, l, d = x.shape
    wd = w.shape[1] // 2
    ng = w_gate.shape[1]
    tok = pl.BlockSpec((None, ROW_TILE, wd), lambda b, j: (b, j, 0))
    return pl.pallas_call(
        functools.partial(_in_lru_kernel, n_lat=n_lat, n_ctx=n_ctx),
        grid=(bsz, l // ROW_TILE),
        in_specs=_row_tile_specs(d, l, n_lat) + [pl.BlockSpec((d, 2 * wd), lambda b, j: (0, 0)),
                                                 pl.BlockSpec((d, ng), lambda b, j: (0, 0)),
                                                 pl.BlockSpec((CONV_K, wd), lambda b, j: (0, 0)),
                                                 pl.BlockSpec((1, wd), lambda b, j: (0, 0))],
        out_specs=[tok, tok, pl.BlockSpec((None, ROW_TILE, ng), lambda b, j: (b, j, 0))],
        out_shape=[jax.ShapeDtypeStruct((bsz, l, wd), F32)] * 2 + [jax.ShapeDtypeStruct((bsz, l, ng), BF16)],
        scratch_shapes=_conv_scratch(wd),
        compiler_params=_cparams("arbitrary", "arbitrary"),
        name="in_lru",
    )(x, x, x, modtab, norm_w, w, w_gate, conv_w, conv_b)


def _lru_kernel(*refs, reverse, final):
    if final:
        u_ref, wa_ref, ba_ref, wi_ref, bi_ref, lam_ref, hf_ref, y_ref, o_ref, carry_ref = refs
    else:
        u_ref, wa_ref, ba_ref, wi_ref, bi_ref, lam_ref, o_ref, carry_ref = refs
    rows = LRU_CHUNK
    i = pl.program_id(1)

    @pl.when(i == 0)
    def _():
        carry_ref[...] = jnp.zeros_like(carry_ref)

    u = u_ref[...]
    ub = u.astype(BF16)
    bw = u.shape[1] // LRU_BLOCKS
    pre_r = jnp.concatenate([_dot(ub[:, n * bw:(n + 1) * bw], wa_ref[n]) for n in range(LRU_BLOCKS)], axis=1)
    pre_i = jnp.concatenate([_dot(ub[:, n * bw:(n + 1) * bw], wi_ref[n]) for n in range(LRU_BLOCKS)], axis=1)
    r = _sigmoid(pre_r + ba_ref[...])
    gi = _sigmoid(pre_i + bi_ref[...])
    a = jnp.exp(-LRU_C * r * _softplus(-lam_ref[...]))
    h = jnp.sqrt(1.0 - a * a) * gi * u

    srow = _iota((SUBLANE, a.shape[1]), 0)
    n_tiles = rows // SUBLANE
    carry = carry_ref[...]
    done = {}
    for k in (range(n_tiles - 1, -1, -1) if reverse else range(n_tiles)):
        sl = slice(k * SUBLANE, (k + 1) * SUBLANE)
        at, ht = a[sl], h[sl]
        for s in (1, 2, 4):
            if reverse:
                keep = srow < SUBLANE - s
                sh = SUBLANE - s
            else:
                keep = srow >= s
                sh = s
            h_s = jnp.where(keep, pltpu.roll(ht, sh, axis=0), 0.0)
            a_s = jnp.where(keep, pltpu.roll(at, sh, axis=0), 1.0)
            ht = ht + at * h_s
            at = at * a_s
        hk = ht + at * carry
        carry = hk[0:1, :] if reverse else hk[SUBLANE - 1:SUBLANE, :]
        if final:
            done[k] = (hk + hf_ref[sl, :]) * jax.nn.gelu(y_ref[sl, :], approximate=True)
            if (k ^ 1) in done:
                lo = k & ~1
                pair = jnp.concatenate([done.pop(lo), done.pop(lo + 1)], axis=0)
                o_ref[lo * SUBLANE:(lo + 2) * SUBLANE, :] = pair.astype(o_ref.dtype)
        else:
            o_ref[sl, :] = hk
    carry_ref[...] = carry


def _lru_scan(u, y, wa, ba, wi, bi, lam, hf, n_lat, n_ctx, reverse):
    bsz, l, w = u.shape
    final = hf is not None
    chunk_fn = functools.partial(_chunk_of_step, n_lat=n_lat, n_ctx=n_ctx, reverse=reverse)
    main = pl.BlockSpec((None, LRU_CHUNK, w), lambda b, i: (b, chunk_fn(i), 0))
    vec = pl.BlockSpec((1, w), lambda b, i: (0, 0))
    blk = pl.BlockSpec((LRU_BLOCKS, w // LRU_BLOCKS, w // LRU_BLOCKS), lambda b, i: (0, 0, 0))
    in_specs = [main, blk, vec, blk, vec, vec]
    args = [u, wa, ba, wi, bi, lam]
    if final:
        in_specs += [main, main]
        args += [hf, y]
    return pl.pallas_call(
        functools.partial(_lru_kernel, reverse=reverse, final=final),
        grid=(bsz, l // LRU_CHUNK),
        in_specs=in_specs,
        out_specs=main,
        out_shape=jax.ShapeDtypeStruct((bsz, l, w), BF16 if final else F32),
        scratch_shapes=[pltpu.VMEM((1, w), F32)],
        compiler_params=_cparams("arbitrary", "arbitrary"),
        name="lru_rev" if reverse else "lru_fwd",
    )(*args)


def _in_gdn_kernel(x_ref, xp_ref, xn_ref, mod_ref, nw_ref, w_ref, cw_ref, alog_ref, dtb_ref,
                   q_ref, k_ref, v_ref, z_ref, gcol_ref, grow_ref, *xe_refs, n_lat, n_ctx):
    rows = ROW_TILE
    first, last = _segment_edges(pl.program_id(1), n_lat, n_ctx)
    h_main, h_all = _normed_rows(x_ref[...], xp_ref[...], xn_ref[...], mod_ref, nw_ref)
    hw = GDN_HEADS * GDN_DK
    for c0, c1, u in _proj_conv(xe_refs, h_all, w_ref, cw_ref, 0, 3 * hw, first, last, rows):
        u = _silu(u)
        for h0 in range(c0, c1, GDN_DK):
            uh = u[:, h0 - c0:h0 - c0 + GDN_DK]
            if h0 < 2 * hw:
                uh = uh * lax.rsqrt(jnp.sum(uh * uh, axis=-1, keepdims=True) + NORM_EPS)
            if h0 < hw:
                q_ref[:, h0:h0 + GDN_DK] = uh * (GDN_DK ** -0.5)
            elif h0 < 2 * hw:
                k_ref[:, h0 - hw:h0 - hw + GDN_DK] = uh
            else:
                v_ref[:, h0 - 2 * hw:h0 - 2 * hw + GDN_DK] = uh
    for c0 in range(0, hw, CONV_COLS):
        z_ref[:, c0:c0 + CONV_COLS] = _dot(h_main, w_ref[:, 3 * hw + c0:3 * hw + c0 + CONV_COLS])

    sm = _dot(h_main, w_ref[:, 4 * hw:4 * hw + LANE])
    g = -jnp.exp(alog_ref[...]) * _softplus(sm + dtb_ref[...])
    beta = _sigmoid(sm)
    ri = _iota((rows, rows), 0)
    ci = _iota((rows, rows), 1)
    same = (ri // SCAN_CHUNK) == (ci // SCAN_CHUNK)
    tri_f = jnp.where(jnp.logical_and(same, ci <= ri), 1.0, 0.0).astype(BF16)
    tri_r = jnp.where(jnp.logical_and(same, ci >= ri), 1.0, 0.0).astype(BF16)
    gcs_f = _exact_left(tri_f, g)
    gcs_r = _exact_left(tri_r, g)
    lane = _iota((rows, LANE), 1)
    gcol = jnp.where(lane < GDN_HEADS, gcs_f,
                     jnp.where(lane < 2 * GDN_HEADS, gcs_r, jnp.where(lane < 4 * GDN_HEADS, beta, 0.0)))
    gcol_ref[...] = gcol
    eye = jnp.where(_iota((LANE, LANE), 0) == _iota((LANE, LANE), 1), 1.0, 0.0).astype(BF16)
    grow_ref[...] = _exact_transpose(gcol, eye)[0:4 * GDN_HEADS, :]


def _in_gdn(x, modtab, norm_w, w, conv_w, alog_vec, dtb_vec, n_lat, n_ctx):
    bsz, l, d = x.shape
    hw = GDN_HEADS * GDN_DK
    tok = pl.BlockSpec((None, ROW_TILE, hw), lambda b, i: (b, i, 0))
    return pl.pallas_call(
        functools.partial(_in_gdn_kernel, n_lat=n_lat, n_ctx=n_ctx),
        grid=(bsz, l // ROW_TILE),
        in_specs=_row_tile_specs(d, l, n_lat) + [pl.BlockSpec((d, w.shape[1]), lambda b, i: (0, 0)),
                                                 pl.BlockSpec((CONV_K, 3 * hw), lambda b, i: (0, 0)),
                                                 pl.BlockSpec((1, LANE), lambda b, i: (0, 0)),
                                                 pl.BlockSpec((1, LANE), lambda b, i: (0, 0))],
        out_specs=[tok, tok, tok, tok,
                   pl.BlockSpec((None, ROW_TILE, LANE), lambda b, i: (b, i, 0)),
                   pl.BlockSpec((None, 4 * GDN_HEADS, ROW_TILE), lambda b, i: (b, 0, i))],
        out_shape=[jax.ShapeDtypeStruct((bsz, l, hw), F32)] * 4
        + [jax.ShapeDtypeStruct((bsz, l, LANE), F32), jax.ShapeDtypeStruct((bsz, 4 * GDN_HEADS, l), F32)],
        scratch_shapes=_conv_scratch(3 * hw),
        compiler_params=_cparams("arbitrary", "arbitrary"),
        name="in_gdn",
    )(x, x, x, modtab, norm_w, w, conv_w, alog_vec, dtb_vec)


def _pair_blockdiag(x):
    c = x.shape[0]
    z = jnp.zeros((c, c), x.dtype)
    return jnp.concatenate([jnp.concatenate([x[:, :c], z], axis=1),
                            jnp.concatenate([z, x[:, c:]], axis=1)], axis=0)


def _gdn_scan_kernel(*refs, reverse, final, n_lat, n_ctx):
    if final:
        q_ref, k_ref, v_ref, gcol_ref, grow_ref, of_ref, z_ref, nw_ref, o_ref, s_ref = refs
    else:
        q_ref, k_ref, v_ref, gcol_ref, grow_ref, o_ref, s_ref = refs
    rows = SCAN_CHUNK
    n_pairs = GDN_HEADS // 2
    pw = 2 * GDN_DK
    chains = [(bb, p) for bb in range(q_ref.shape[0]) for p in range(n_pairs)]
    n_chains = len(chains)
    i = pl.program_id(1)

    @pl.when(i == 0)
    def _():
        s_ref[...] = jnp.zeros_like(s_ref)

    ri = _iota((rows, rows), 0)
    ci = _iota((rows, rows), 1)
    incl = (ri <= ci) if reverse else (ri >= ci)
    ri2 = _iota((rows, pw), 0)
    ci2 = jnp.bitwise_and(_iota((rows, pw), 1), rows - 1)
    strict2 = (ri2 < ci2) if reverse else (ri2 > ci2)
    d = 1 if reverse else 0
    edge = 0 if reverse else rows - 1

    def pair_cols(gcol, lane0):
        return jnp.concatenate([jnp.broadcast_to(gcol[:, lane0 + j:lane0 + j + 1], (rows, GDN_DK))
                                for j in range(2)], axis=1)

    qs, ks, kbs, egs, gcs, decs, rhs, sts = [], [], [], [], [], [], [], []
    for c, (bb, p) in enumerate(chains):
        sl = slice(p * pw, (p + 1) * pw)
        li = d * GDN_HEADS + 2 * p
        gcol = gcol_ref[bb]
        grow = grow_ref[bb]
        q2, k2, v2 = q_ref[bb, :, sl], k_ref[bb, :, sl], v_ref[bb, :, sl]
        gc2 = pair_cols(gcol, li)
        beta2 = pair_cols(gcol, 2 * GDN_HEADS + li)
        eg2 = jnp.exp(gc2)
        kb2 = k2 * beta2
        dec2 = jnp.concatenate(
            [jnp.exp(jnp.where(incl, gcol[:, li + j:li + j + 1] - grow[li + j:li + j + 1, :], NEG_BIG))
             for j in range(2)], axis=1)
        qs.append(q2)
        ks.append(k2)
        kbs.append(kb2)
        egs.append(eg2)
        gcs.append(gc2)
        decs.append(dec2)
        rhs.append([jnp.concatenate([v2[:, j * GDN_DK:(j + 1) * GDN_DK] * beta2[:, j * GDN_DK:(j + 1) * GDN_DK],
                                     kb2[:, j * GDN_DK:(j + 1) * GDN_DK] * eg2[:, j * GDN_DK:(j + 1) * GDN_DK]],
                                    axis=1).astype(BF16) for j in range(2)])
        sts.append(s_ref[c])

    a_mats, attns = [], []
    for p in range(n_chains):
        kq = _dot_nt(jnp.concatenate([kbs[p], qs[p]], axis=0).astype(BF16), _pair_blockdiag(ks[p].astype(BF16)))
        a_mats.append(jnp.where(strict2, kq[:rows] * decs[p], 0.0))
        attns.append((kq[rows:] * decs[p]).astype(BF16))

    half = rows // 2
    rq = _iota((half, pw), 0)
    lq = _iota((half, pw), 1)
    cq = jnp.bitwise_and(lq, half - 1)
    low_half = jnp.bitwise_and(lq, rows - 1) < half
    blk_q = lq // half

    def quad_blockdiag(y):
        return jnp.concatenate([jnp.where(blk_q == b, y, jnp.zeros_like(y)) for b in range(pw // half)], axis=0)

    aqs = [jnp.where(low_half, a_mats[p][:half], a_mats[p][half:]) for p in range(n_chains)]
    tqs = [jnp.where(rq == cq, 1.0, 0.0) - jnp.where((rq // 2) == (cq // 2), aqs[p], 0.0) for p in range(n_chains)]
    s = 2
    while s < half:
        off = jnp.logical_and((rq // (2 * s)) == (cq // (2 * s)), (rq // s) != (cq // s))
        xs = [_dot(jnp.where(off, aqs[p], 0.0).astype(BF16), quad_blockdiag(tqs[p].astype(BF16)))
              for p in range(n_chains)]
        tqs = [tqs[p] - _dot(tqs[p].astype(BF16), quad_blockdiag(xs[p].astype(BF16))) for p in range(n_chains)]
        s *= 2
    ts = [jnp.concatenate([jnp.where(low_half, tqs[p], 0.0), jnp.where(low_half, 0.0, tqs[p])], axis=0)
          for p in range(n_chains)]
    off = (ri2 // half) != (ci2 // half)
    xs = [_dot(jnp.where(off, a_mats[p], 0.0).astype(BF16), _pair_blockdiag(ts[p].astype(BF16)))
          for p in range(n_chains)]
    ts = [ts[p] - _dot(ts[p].astype(BF16), _pair_blockdiag(xs[p].astype(BF16))) for p in range(n_chains)]

    us, ws = [], []
    for p in range(n_chains):
        tb = ts[p].astype(BF16)
        sol = [_dot(tb[:, j * GDN_DK:(j + 1) * GDN_DK], rhs[p][j]) for j in range(2)]
        us.append(jnp.concatenate([sol[0][:, :GDN_DK], sol[1][:, :GDN_DK]], axis=1))
        ws.append(jnp.concatenate([sol[0][:, GDN_DK:], sol[1][:, GDN_DK:]], axis=1))

    vns, outs = [], []
    for p in range(n_chains):
        wq = jnp.concatenate([ws[p], qs[p] * egs[p]], axis=0).astype(BF16)
        ws_qs = _dot(wq, _pair_blockdiag(sts[p].astype(BF16)))
        vns.append((us[p] - ws_qs[:rows]).astype(BF16))
        outs.append(ws_qs[rows:])
    for c, (bb, p) in enumerate(chains):
        o2 = outs[c] + _dot(attns[c], _pair_blockdiag(vns[c]))
        g_end = gcs[c][edge:edge + 1, :]
        k_dec = (ks[c] * jnp.exp(g_end - gcs[c])).astype(BF16)
        full = _dot_tn(k_dec, vns[c])
        upd = jnp.concatenate([full[:GDN_DK, :GDN_DK], full[GDN_DK:, GDN_DK:]], axis=1)
        s_ref[c] = sts[c] * jnp.exp(g_end) + upd
        sl = slice(p * pw, (p + 1) * pw)
        if final:
            o2 = o2 + of_ref[bb, :, sl]
            ys = []
            for j in range(2):
                oj = o2[:, j * GDN_DK:(j + 1) * GDN_DK]
                ys.append(oj * lax.rsqrt(jnp.mean(oj * oj, axis=-1, keepdims=True) + NORM_EPS) * nw_ref[...])
            o_ref[bb, :, sl] = (jnp.concatenate(ys, axis=1) * _silu(z_ref[bb, :, sl])).astype(o_ref.dtype)
        else:
            o_ref[bb, :, sl] = o2


def _gdn_scan(q, k, v, gcol, grow, o_fwd, z, norm_w, n_lat, n_ctx, reverse):
    bsz, l, hw = q.shape
    final = o_fwd is not None
    chunk_fn = functools.partial(_chunk_of_step, n_lat=n_lat, n_ctx=n_ctx, reverse=reverse)
    nb = next(n for n in (4, 2, 1) if bsz % n == 0)
    tok = pl.BlockSpec((nb, SCAN_CHUNK, hw), lambda b, i: (b, chunk_fn(i), 0))
    in_specs = [tok, tok, tok,
                pl.BlockSpec((nb, SCAN_CHUNK, LANE), lambda b, i: (b, chunk_fn(i), 0)),
                pl.BlockSpec((nb, 4 * GDN_HEADS, SCAN_CHUNK), lambda b, i: (b, 0, chunk_fn(i)))]
    args = [q, k, v, gcol, grow]
    if final:
        in_specs += [tok, tok, pl.BlockSpec((1, GDN_DK), lambda b, i: (0, 0))]
        args += [o_fwd, z, norm_w]
    return pl.pallas_call(
        functools.partial(_gdn_scan_kernel, reverse=reverse, final=final, n_lat=n_lat, n_ctx=n_ctx),
        grid=(bsz // nb, l // SCAN_CHUNK),
        in_specs=in_specs,
        out_specs=tok,
        out_shape=jax.ShapeDtypeStruct((bsz, l, hw), BF16 if final else F32),
        scratch_shapes=[pltpu.VMEM((nb * GDN_HEADS // 2, GDN_DK, 2 * GDN_DK), F32)],
        compiler_params=_cparams("arbitrary", "arbitrary"),
        name="gdn_rev" if reverse else "gdn_fwd",
    )(*args)


def _in_ssd_kernel(xl_ref, xlp_ref, xln_ref, xc_ref, xcp_ref, xcn_ref, mod_ref, nw_ref, w_ref, cw_ref, cb_ref,
                   z_ref, u_ref, dt_ref, *xe_refs, n_lat, n_ctx):
    j = pl.program_id(1)
    first, last = _segment_edges(j, n_lat, n_ctx)
    is_ctx = j >= n_lat
    pick = lambda c_ref, l_ref: jnp.where(is_ctx, c_ref[...], l_ref[...])
    h_main, h_all = _normed_rows(pick(xc_ref, xl_ref), pick(xcp_ref, xlp_ref), pick(xcn_ref, xln_ref),
                                 mod_ref, nw_ref)
    inner = z_ref.shape[1]
    xbc_w = u_ref.shape[1]
    for c0 in range(0, inner, CONV_COLS):
        z_ref[:, c0:c0 + CONV_COLS] = _dot(h_main, w_ref[:, c0:c0 + CONV_COLS])
    for c0, c1, u in _proj_conv(xe_refs, h_all, w_ref, cw_ref, inner, xbc_w, first, last, ROW_TILE):
        u_ref[:, c0:c1] = _silu(u + cb_ref[:, c0:c1])
    dt_ref[...] = _dot(h_main, w_ref[:, inner + xbc_w:inner + xbc_w + LANE])


def _in_ssd(x_lat, x, modtab, norm_w, w, conv_w, conv_b, n_lat, n_ctx):
    bsz, l, d = x.shape
    inner = SSD_HEADS * SSD_HEADDIM
    xbc_w = inner + 2 * SSD_GROUPS * SSD_STATE
    tok = lambda n: pl.BlockSpec((None, ROW_TILE, n), lambda b, j: (b, j, 0))
    lat_tile = lambda j: jnp.minimum(j, n_lat - 1)
    ctx_tile = lambda j: jnp.maximum(j, n_lat)
    lat_specs = [pl.BlockSpec((None, ROW_TILE, d), lambda b, j: (b, lat_tile(j), 0)),
                 *_halo_specs(d, ROW_TILE, x_lat.shape[1], lat_tile)]
    ctx_specs = [pl.BlockSpec((None, ROW_TILE, d), lambda b, j: (b, ctx_tile(j), 0)),
                 *_halo_specs(d, ROW_TILE, l, ctx_tile)]
    return pl.pallas_call(
        functools.partial(_in_ssd_kernel, n_lat=n_lat, n_ctx=n_ctx),
        grid=(bsz, l // ROW_TILE),
        in_specs=lat_specs + ctx_specs + _row_tile_specs(d, l, n_lat)[3:] + [
            pl.BlockSpec((d, w.shape[1]), lambda b, j: (0, 0)),
            pl.BlockSpec((CONV_K, xbc_w), lambda b, j: (0, 0)),
            pl.BlockSpec((1, xbc_w), lambda b, j: (0, 0))],
        out_specs=[tok(inner), tok(xbc_w), tok(LANE)],
        out_shape=[jax.ShapeDtypeStruct((bsz, l, inner), F32), jax.ShapeDtypeStruct((bsz, l, xbc_w), F32),
                   jax.ShapeDtypeStruct((bsz, l, LANE), F32)],
        scratch_shapes=_conv_scratch(xbc_w),
        compiler_params=_cparams("arbitrary", "arbitrary"),
        name="in_ssd",
    )(x_lat, x_lat, x_lat, x, x, x, modtab, norm_w, w, conv_w, conv_b)


def _ssd_scan_kernel(*refs, reverse, final):
    if final:
        z_ref, u_ref, dt_ref, dtb_ref, a_ref, dsk_ref, yf_ref, nw_ref, o_ref, st_ref = refs
    else:
        u_ref, dt_ref, dtb_ref, a_ref, dsk_ref, o_ref, st_ref = refs
    rows = SCAN_CHUNK
    inner = SSD_HEADS * SSD_HEADDIM
    gw = inner // SSD_GROUPS
    i = pl.program_id(1)

    @pl.when(i == 0)
    def _():
        st_ref[...] = jnp.zeros_like(st_ref)

    ri = _iota((rows, rows), 0)
    ci = _iota((rows, rows), 1)
    incl = (ri <= ci) if reverse else (ri >= ci)
    tri = jnp.where(incl, 1.0, 0.0).astype(BF16)
    eye = jnp.where(ri == ci, 1.0, 0.0).astype(BF16)
    d = 1 if reverse else 0
    edge = 0 if reverse else rows - 1
    er = _iota((LANE, inner), 0)
    ec = _iota((LANE, inner), 1)
    expand = jnp.where(er == d * SSD_HEADS + ec // SSD_HEADDIM, 1.0, 0.0).astype(BF16)
    lane = _iota((rows, LANE), 1)
    heads_per_group = SSD_HEADS // SSD_GROUPS
    samples = range(u_ref.shape[0])
    units = [(bb, g) for bb in samples for g in range(SSD_GROUPS)]

    us, xss, acss, acs_ts, ea_es, xdt_bs, xw_bs = [], [], [], [], [], [], []
    for bb in samples:
        u = u_ref[bb]
        xs = u[:, :inner]
        dt = _softplus(dt_ref[bb] + dtb_ref[...])
        acs = _exact_left(tri, a_ref[...] * dt)
        xdt = xs * _exact_right(dt, expand)
        us.append(u)
        xss.append(xs)
        acss.append(acs)
        acs_ts.append(_exact_transpose(acs, eye))
        ea_es.append(_exact_right(jnp.exp(acs), expand))
        xdt_bs.append(xdt.astype(BF16))
        xw_bs.append((xdt * _exact_right(jnp.exp(acs[edge:edge + 1, :] - acs), expand)).astype(BF16))

    bms, cbs, sts, y_offs = [], [], [], []
    for c, (bb, g) in enumerate(units):
        u = us[bb]
        bm = u[:, inner + g * SSD_STATE:inner + (g + 1) * SSD_STATE].astype(BF16)
        cm = u[:, inner + (SSD_GROUPS + g) * SSD_STATE:inner + (SSD_GROUPS + g + 1) * SSD_STATE].astype(BF16)
        st = st_ref[c]
        bms.append(bm)
        cbs.append(_dot_nt(cm, bm))
        sts.append(st)
        y_offs.append(_dot(cm, st.astype(BF16)) * ea_es[bb][:, g * gw:(g + 1) * gw])

    diags = [[] for _ in units]
    for hp in range(heads_per_group // 2):
        for c, (bb, g) in enumerate(units):
            h0 = g * heads_per_group + 2 * hp
            mats = []
            for hh in (h0, h0 + 1):
                li = d * SSD_HEADS + hh
                lm = jnp.exp(jnp.where(incl, acss[bb][:, li:li + 1] - acs_ts[bb][li:li + 1, :], NEG_BIG))
                mats.append((cbs[c] * lm).astype(BF16))
            xp = xdt_bs[bb][:, h0 * SSD_HEADDIM:(h0 + 2) * SSD_HEADDIM]
            x_lo = jnp.where(lane < SSD_HEADDIM, xp, jnp.zeros_like(xp))
            x_hi = jnp.where(lane >= SSD_HEADDIM, xp, jnp.zeros_like(xp))
            diags[c].append(_dot(jnp.concatenate(mats, axis=1), jnp.concatenate([x_lo, x_hi], axis=0)))

    for c, (bb, g) in enumerate(units):
        st_ref[c] = (sts[c] * ea_es[bb][edge:edge + 1, g * gw:(g + 1) * gw]
                     + _dot_tn(bms[c], xw_bs[bb][:, g * gw:(g + 1) * gw]))
    for bb in samples:
        y_parts = [jnp.concatenate(diags[bb * SSD_GROUPS + g], axis=1) + y_offs[bb * SSD_GROUPS + g]
                   for g in range(SSD_GROUPS)]
        y = jnp.concatenate(y_parts, axis=1) + dsk_ref[...] * xss[bb]
        if final:
            y = y + yf_ref[bb]
            yz = y * _silu(z_ref[bb])
            outs = []
            for g in range(SSD_GROUPS):
                yg = yz[:, g * gw:(g + 1) * gw]
                outs.append(yg * lax.rsqrt(jnp.mean(yg * yg, axis=-1, keepdims=True) + NORM_EPS))
            o_ref[bb] = (jnp.concatenate(outs, axis=1) * nw_ref[...]).astype(o_ref.dtype)
        else:
            o_ref[bb] = y


def _ssd_scan(z, u, dt_raw, dtb_vec, a_vec, dskip, y_fwd, norm_w, n_lat, n_ctx, reverse):
    bsz, l, xbc_w = u.shape
    inner = SSD_HEADS * SSD_HEADDIM
    final = y_fwd is not None
    chunk_fn = functools.partial(_chunk_of_step, n_lat=n_lat, n_ctx=n_ctx, reverse=reverse)
    nb = 2 if bsz % 2 == 0 else 1
    tok = lambda n: pl.BlockSpec((nb, SCAN_CHUNK, n), lambda b, i: (b, chunk_fn(i), 0))
    row = lambda n: pl.BlockSpec((1, n), lambda b, i: (0, 0))
    in_specs = [tok(xbc_w), tok(LANE), row(LANE), row(LANE), row(inner)]
    args = [u, dt_raw, dtb_vec, a_vec, dskip]
    if final:
        in_specs = [tok(inner)] + in_specs + [tok(inner), row(inner)]
        args = [z] + args + [y_fwd, norm_w]
    return pl.pallas_call(
        functools.partial(_ssd_scan_kernel, reverse=reverse, final=final),
        grid=(bsz // nb, l // SCAN_CHUNK),
        in_specs=in_specs,
        out_specs=tok(inner),
        out_shape=jax.ShapeDtypeStruct((bsz, l, inner), BF16 if final else F32),
        scratch_shapes=[pltpu.VMEM((nb * SSD_GROUPS, SSD_STATE, inner // SSD_GROUPS), F32)],
        compiler_params=_cparams("arbitrary", "arbitrary"),
        name="ssd_rev" if reverse else "ssd_fwd",
    )(*args)


def _merge_kernel(x_ref, lru_ref, gdn_ref, ssdl_ref, ssdc_ref, gate_ref, mod_ref, nw_ref, wb_ref, wo_ref, wr_ref,
                  br_ref, xo_ref, h_ref, lg_ref, *, n_lat):
    d = x_ref.shape[1]
    ssd = jnp.where(pl.program_id(1) >= n_lat, ssdc_ref[...], ssdl_ref[...])
    acc = None
    for k, b in enumerate((lru_ref[...], gdn_ref[...], ssd)):
        t = _sigmoid(gate_ref[:, k * d:(k + 1) * d].astype(F32)) * _dot(b.astype(BF16), wb_ref[k])
        acc = t if acc is None else acc + t
    mix = _dot(acc.astype(BF16), wo_ref[...])
    x = x_ref[...] + mod_ref[2:3, :] * mix
    xo_ref[...] = x
    h = _mod_norm(x, nw_ref[...], mod_ref[3:4, :], mod_ref[4:5, :])
    hb = h.astype(BF16)
    h_ref[...] = hb
    lg_ref[...] = _dot(hb, wr_ref[...]) + br_ref[...]


def _merge(x, lru, gdn, ssd_lat, ssd_scan, gate, modtab, norm_w, wb, wo, wr, br, n_lat_tiles):
    bsz, l, d = x.shape
    tok = pl.BlockSpec((None, ROW_TILE, d), lambda b, j: (b, j, 0))
    return pl.pallas_call(
        functools.partial(_merge_kernel, n_lat=n_lat_tiles),
        grid=(bsz, l // ROW_TILE),
        in_specs=[tok, tok, tok,
                  pl.BlockSpec((None, ROW_TILE, d), lambda b, j: (b, jnp.minimum(j, n_lat_tiles - 1), 0)),
                  pl.BlockSpec((None, ROW_TILE, d), lambda b, j: (b, jnp.maximum(j, n_lat_tiles), 0)),
                  pl.BlockSpec((None, ROW_TILE, 3 * d), lambda b, j: (b, j, 0)),
                  pl.BlockSpec((None, None, SUBLANE, d), lambda b, j: (b, jnp.where(j >= n_lat_tiles, 1, 0), 0, 0)),
                  pl.BlockSpec((1, d), lambda b, j: (0, 0)),
                  pl.BlockSpec((3, d, d), lambda b, j: (0, 0, 0)),
                  pl.BlockSpec((d, d), lambda b, j: (0, 0)),
                  pl.BlockSpec((d, LANE), lambda b, j: (0, 0)),
                  pl.BlockSpec((1, LANE), lambda b, j: (0, 0))],
        out_specs=[tok, tok, pl.BlockSpec((None, ROW_TILE, LANE), lambda b, j: (b, j, 0))],
        out_shape=[jax.ShapeDtypeStruct((bsz, l, d), F32), jax.ShapeDtypeStruct((bsz, l, d), BF16),
                   jax.ShapeDtypeStruct((bsz, l, LANE), F32)],
        compiler_params=_cparams("arbitrary", "arbitrary"),
        name="merge",
    )(x, lru, gdn, ssd_lat, ssd_scan, gate, modtab, norm_w, wb, wo, wr, br)


def _moe_kernel(blk_ref, exp_ref, lo_ref, hi_ref, x_ref, w1_ref, w3_ref, w2_ref, o_ref,
                acc_ref, w1b_ref, w3b_ref, w2b_ref):
    i = pl.program_id(0)
    lo = lo_ref[i]
    hi = hi_ref[i]
    base = blk_ref[i] * MOE_ROWS

    @pl.when(jnp.logical_or(i == 0, exp_ref[i] != exp_ref[jnp.maximum(i - 1, 0)]))
    def _():
        w1b_ref[...] = w1_ref[...].astype(BF16)
        w3b_ref[...] = w3_ref[...].astype(BF16)
        w2b_ref[...] = w2_ref[...].astype(BF16)

    @pl.when(lo == base)
    def _():
        acc_ref[...] = jnp.zeros_like(acc_ref)

    @pl.when(hi > lo)
    def _():
        row = base + _iota((MOE_ROWS, 1), 0)
        x = jnp.where(jnp.logical_and(row >= lo, row < hi), x_ref[...], jnp.zeros_like(x_ref))
        mid = _silu(_dot(x, w1b_ref[...])) * _dot(x, w3b_ref[...])
        acc_ref[...] += _dot(mid.astype(BF16), w2b_ref[...])

    o_ref[...] = acc_ref[...].astype(o_ref.dtype)


def _moe_experts(xb, item_block, item_expert, item_lo, item_hi, w1, w3, w2, layer):
    n_rows, d = xb.shape
    ff = w1.shape[-1]
    rows_spec = pl.BlockSpec((MOE_ROWS, d), lambda i, blk, ex, lo, hi: (blk[i], 0))
    grid_spec = pltpu.PrefetchScalarGridSpec(
        num_scalar_prefetch=4,
        grid=(item_block.shape[0],),
        in_specs=[rows_spec,
                  pl.BlockSpec((None, None, d, ff), lambda i, blk, ex, lo, hi: (layer, ex[i], 0, 0)),
                  pl.BlockSpec((None, None, d, ff), lambda i, blk, ex, lo, hi: (layer, ex[i], 0, 0)),
                  pl.BlockSpec((None, None, ff, d), lambda i, blk, ex, lo, hi: (layer, ex[i], 0, 0))],
        out_specs=rows_spec,
        scratch_shapes=[pltpu.VMEM((MOE_ROWS, d), F32), pltpu.VMEM((d, ff), BF16), pltpu.VMEM((d, ff), BF16),
                        pltpu.VMEM((ff, d), BF16)],
    )
    return pl.pallas_call(
        _moe_kernel,
        grid_spec=grid_spec,
        out_shape=jax.ShapeDtypeStruct((n_rows, d), BF16),
        compiler_params=_cparams("arbitrary"),
        name="moe_experts",
    )(item_block, item_expert, item_lo, item_hi, xb, w1, w3, w2)


def _route(logits):
    t = logits.shape[0]
    g_logits = logits[:, :N_GROUPS]
    e_logits = logits[:, N_GROUPS:N_GROUPS + N_EXPERTS].reshape(t, N_GROUPS, EXPERTS_PER_GROUP)
    g_prob = jax.nn.softmax(g_logits, axis=-1)
    g_idx = jnp.argmax(g_logits, axis=-1)
    p_group = jnp.take_along_axis(g_prob, g_idx[:, None], axis=1)[:, 0]
    e_in_group = jnp.take_along_axis(e_logits, g_idx[:, None, None], axis=1)[:, 0]
    top_v, top_i = lax.top_k(e_in_group, TOP_K)
    weights = jax.nn.softmax(top_v, axis=-1) * p_group[:, None]
    expert_id = (g_idx[:, None] * EXPERTS_PER_GROUP + top_i).reshape(-1).astype(jnp.int32)
    n_assign = t * TOP_K
    assert n_assign % MOE_ROWS == 0
    n_blocks = n_assign // MOE_ROWS
    ar = jnp.arange(n_assign, dtype=jnp.int32)
    e_sorted, order = lax.sort((expert_id, ar), num_keys=1)
    _, slot = lax.sort((order, ar), num_keys=1)
    ends = jnp.sum(e_sorted[None, :] <= jnp.arange(N_EXPERTS, dtype=jnp.int32)[:, None], axis=1).astype(jnp.int32)
    starts = jnp.concatenate([jnp.zeros((1,), jnp.int32), ends[:-1]])
    cuts = jnp.sort(jnp.concatenate([jnp.arange(n_blocks, dtype=jnp.int32) * MOE_ROWS, starts]))
    item_lo = cuts
    item_hi = jnp.concatenate([cuts[1:], jnp.full((1,), n_assign, jnp.int32)])
    item_block = jnp.minimum(item_lo // MOE_ROWS, n_blocks - 1).astype(jnp.int32)
    item_expert = jnp.minimum(jnp.sum(item_lo[:, None] >= ends[None, :], axis=1), N_EXPERTS - 1).astype(jnp.int32)
    return order // TOP_K, (item_block, item_expert, item_lo, item_hi), slot.reshape(t, TOP_K), weights


def _combine_kernel(x_ref, ya_ref, yb_ref, wt_ref, mod_ref, nw_ref, o_ref, *, final):
    wt = wt_ref[...]
    y = wt[:, 0:1] * ya_ref[...].astype(F32) + wt[:, 1:2] * yb_ref[...].astype(F32)
    x = x_ref[...] + mod_ref[5:6, :] * y
    if final:
        x = x * lax.rsqrt(jnp.mean(x * x, axis=-1, keepdims=True) + NORM_EPS) * nw_ref[...]
    o_ref[...] = x


def _combine(x, ya, yb, wts, modtab, norm_w, n_lat_tiles, final, out_len):
    bsz, l, d = x.shape
    tok = pl.BlockSpec((None, ROW_TILE, d), lambda b, j: (b, j, 0))
    return pl.pallas_call(
        functools.partial(_combine_kernel, final=final),
        grid=(bsz, out_len // ROW_TILE),
        in_specs=[tok, tok, tok,
                  pl.BlockSpec((None, ROW_TILE, LANE), lambda b, j: (b, j, 0)),
                  pl.BlockSpec((None, None, SUBLANE, d), lambda b, j: (b, jnp.where(j >= n_lat_tiles, 1, 0), 0, 0)),
                  pl.BlockSpec((1, d), lambda b, j: (0, 0))],
        out_specs=tok,
        out_shape=jax.ShapeDtypeStruct((bsz, out_len, d), F32),
        compiler_params=_cparams("arbitrary", "arbitrary"),
        name="combine",
    )(x, ya, yb, wts, modtab, norm_w)


def _pad_lanes(v, n=LANE):
    v = v.reshape(1, -1).astype(F32)
    return jnp.pad(v, ((0, 0), (0, n - v.shape[1])))


def kernel(x, c, ctx, c_ctx, w_mod, b_mod, norm1_w, norm2_w, w_in, lru_conv_w, lru_conv_b, lru_wa, lru_ba,
           lru_wi, lru_bi, lru_lambda, gdn_conv_w, gdn_a_log, gdn_dt_bias, gdn_norm_w, ssd_conv_w, ssd_conv_b,
           ssd_a_log, ssd_dt_bias, ssd_d, ssd_norm_w, w_branch, w_out, router_group_w, router_group_b,
           router_expert_w, router_expert_b, expert_w1, expert_w3, expert_w2, final_norm_w):
    bsz, seq, d = x.shape
    n_ctx_tok = ctx.shape[1]
    depth = w_mod.shape[0]
    l = seq + n_ctx_tok
    rows = seq // GRID_W
    assert seq % LRU_CHUNK == 0 and n_ctx_tok % LRU_CHUNK == 0 and ROW_TILE == LRU_CHUNK
    assert bsz + 1 <= SUBLANE
    assert l % GRID_W == 0 and l % rows == 0
    hw = GDN_HEADS * GDN_DK
    inner = SSD_HEADS * SSD_HEADDIM
    xbc_w = inner + 2 * SSD_GROUPS * SSD_STATE

    act = jnp.concatenate([_silu(c), _silu(c_ctx)[None, :],
                           jnp.zeros((SUBLANE - bsz - 1, d), F32)], axis=0)
    mod_all = _modulation(act, w_mod, b_mod).reshape(depth, SUBLANE, 6, d)
    pad2 = jnp.zeros((bsz, 2, d), F32)

    xs = jnp.concatenate([x, ctx], axis=1)
    nl_t, nc_t = seq // ROW_TILE, n_ctx_tok // ROW_TILE
    nl_s, nc_s = seq // SCAN_CHUNK, n_ctx_tok // SCAN_CHUNK

    o_lx, o_qkv, o_gz, o_gb, o_ga = 0, 2 * d, 2 * d + 3 * hw, 2 * d + 4 * hw, 2 * d + 4 * hw + 2 * GDN_HEADS
    o_sz = o_ga + 2 * GDN_HEADS
    o_xbc = o_sz + inner
    o_sdt = o_xbc + xbc_w
    o_gate = o_sdt + 2 * SSD_HEADS

    for i in range(depth):
        lat = mod_all[i, :bsz]
        cx = jnp.broadcast_to(mod_all[i, bsz][None], (bsz, 6, d))
        modtab = jnp.stack([jnp.concatenate([lat, pad2], axis=1), jnp.concatenate([cx, pad2], axis=1)], axis=1)

        wi_ = w_in[i]
        w_lru = wi_[:, o_lx:o_qkv].astype(BF16)
        zpad = jnp.zeros((d, LANE - 4 * GDN_HEADS), F32)
        w_gdn = jnp.concatenate([wi_[:, o_qkv:o_gb], wi_[:, o_ga:o_sz], wi_[:, o_gb:o_ga], zpad], axis=1).astype(BF16)
        zpad2 = jnp.zeros((d, LANE - 2 * SSD_HEADS), F32)
        w_ssd = jnp.concatenate([wi_[:, o_sz:o_gate], zpad2], axis=1).astype(BF16)
        w_gate = wi_[:, o_gate:].astype(BF16)
        n1 = norm1_w[i].reshape(1, d)

        lru_u, lru_y, gate = _in_lru(xs, modtab, n1, w_lru, w_gate, lru_conv_w[i], lru_conv_b[i].reshape(1, d),
                                     nl_t, nc_t)
        h_f = _lru_scan(lru_u, lru_y, lru_wa[i, 0].astype(BF16), lru_ba[i, 0].reshape(1, d),
                        lru_wi[i, 0].astype(BF16), lru_bi[i, 0].reshape(1, d), lru_lambda[i, 0].reshape(1, d),
                        None, nl_t, nc_t, False)
        lru_out = _lru_scan(lru_u, lru_y, lru_wa[i, 1].astype(BF16), lru_ba[i, 1].reshape(1, d),
                            lru_wi[i, 1].astype(BF16), lru_bi[i, 1].reshape(1, d), lru_lambda[i, 1].reshape(1, d),
                            h_f, nl_t, nc_t, True)

        q, k, v, gz, gcol, grow = _in_gdn(xs, modtab, n1, w_gdn, gdn_conv_w[i], _pad_lanes(gdn_a_log[i]),
                                          _pad_lanes(gdn_dt_bias[i]), nl_t, nc_t)
        o_f = _gdn_scan(q, k, v, gcol, grow, None, None, None, nl_s, nc_s, False)
        gdn_out = _gdn_scan(q, k, v, gcol, grow, o_f, gz, gdn_norm_w[i].reshape(1, GDN_DK), nl_s, nc_s, True)

        x_cm = xs.reshape(bsz, l // GRID_W, GRID_W, d)[:, :rows].swapaxes(1, 2).reshape(bsz, seq, d)
        sz, su, sdt = _in_ssd(x_cm, xs, modtab, n1, w_ssd, ssd_conv_w[i], ssd_conv_b[i].reshape(1, xbc_w),
                              nl_t, nc_t)
        dtb_vec = _pad_lanes(ssd_dt_bias[i])
        a_vec = _pad_lanes(-jnp.exp(ssd_a_log[i].astype(F32)))
        dsk = [jnp.repeat(ssd_d[i, dd], SSD_HEADDIM).reshape(1, inner) for dd in range(2)]
        y_f = _ssd_scan(None, su, sdt, dtb_vec, a_vec, dsk[0], None, None, nl_s, nc_s, False)
        ssd_scan_out = _ssd_scan(sz, su, sdt, dtb_vec, a_vec, dsk[1], y_f,
                                 ssd_norm_w[i].reshape(1, inner), nl_s, nc_s, True)
        ssd_lat = (ssd_scan_out.reshape(bsz, l // rows, rows, inner)[:, :GRID_W].swapaxes(1, 2)
                   .reshape(bsz, seq, inner))

        w_r = jnp.concatenate([router_group_w[i], router_expert_w[i],
                               jnp.zeros((d, LANE - N_GROUPS - N_EXPERTS), F32)], axis=1).astype(BF16)
        b_r = _pad_lanes(jnp.concatenate([router_group_b[i], router_expert_b[i]]))
        x_mid, h2, logits = _merge(xs, lru_out, gdn_out, ssd_lat, ssd_scan_out, gate, modtab, norm2_w[i].reshape(1, d),
                                   w_branch[i].astype(BF16), w_out[i].astype(BF16), w_r, b_r, nl_t)

        t = bsz * l
        tok_sorted, items, slot, weights = _route(logits.reshape(t, LANE))
        xb = h2.reshape(t, d)[tok_sorted]
        yb = _moe_experts(xb, *items, expert_w1, expert_w3, expert_w2, i)
        y0 = yb[slot[:, 0]].reshape(bsz, l, d)
        y1 = yb[slot[:, 1]].reshape(bsz, l, d)
        wts = jnp.pad(weights, ((0, 0), (0, LANE - TOP_K))).reshape(bsz, l, LANE)
        last = i == depth - 1
        xs = _combine(x_mid, y0, y1, wts, modtab, final_norm_w.reshape(1, d), nl_t, last, seq if last else l)

    return xs
```
